```python
import jax, jax.numpy as jnp
from jax import lax
import numpy as np

D_MODEL = 2048
BATCH = 8
SEQ = 2048
DEPTH = 4

D_MIX = D_MODEL
HEAD_DIM = 128
ATTN_WIDTH = D_MIX // 2
N_ATTN_HEADS = ATTN_WIDTH // HEAD_DIM
GMLP_WIDTH = D_MIX // 4
N_GMLP_HEADS = 4
GMLP_HEAD_DIM = GMLP_WIDTH // N_GMLP_HEADS
POOL_WIDTH = D_MIX - ATTN_WIDTH - GMLP_WIDTH
N_POOL_GROUPS = 4
POOL_GROUP_DIM = POOL_WIDTH // N_POOL_GROUPS
POOL_WINDOWS = (2, 4, 8, 16)
CHUNK = 128
Q_BLOCK = 128
D_PLE = 256
D_FF = -(-8 * D_MODEL // (3 * 256)) * 256
EPS = 1e-6

PROJ_SIZES = (ATTN_WIDTH, ATTN_WIDTH, ATTN_WIDTH, N_ATTN_HEADS, GMLP_WIDTH, GMLP_WIDTH, POOL_WIDTH)
D_PROJ = 3 * ATTN_WIDTH + N_ATTN_HEADS + 2 * GMLP_WIDTH + POOL_WIDTH
SPLIT_POINTS = (
    ATTN_WIDTH,
    2 * ATTN_WIDTH,
    3 * ATTN_WIDTH,
    3 * ATTN_WIDTH + N_ATTN_HEADS,
    3 * ATTN_WIDTH + N_ATTN_HEADS + GMLP_WIDTH,
    3 * ATTN_WIDTH + N_ATTN_HEADS + 2 * GMLP_WIDTH,
)

kernel_name = "hybrid_parallel_fox_gmlp_pool_block"


def rms_norm(x, gain):
    xf = x.astype(jnp.float32)
    y = xf * lax.rsqrt(jnp.mean(xf * xf, axis=-1, keepdims=True) + EPS)
    return (y * gain.astype(jnp.float32)).astype(x.dtype)


def fox_attention(q, k, v, log_f):
    B, S, H, Dh = q.shape
    nb = S // Q_BLOCK
    c = jnp.cumsum(log_f, axis=1)
    q_blocks = q.reshape(B, nb, Q_BLOCK, H, Dh).transpose(1, 0, 3, 2, 4)
    cq_blocks = c.reshape(B, nb, Q_BLOCK, H).transpose(1, 0, 3, 2)
    pos_blocks = jnp.arange(S, dtype=jnp.int32).reshape(nb, Q_BLOCK)
    kh = k.transpose(0, 2, 1, 3)
    vh = v.transpose(0, 2, 1, 3)
    ck = c.transpose(0, 2, 1)
    kpos = jnp.arange(S, dtype=jnp.int32)
    scale = Dh ** -0.5

    def one_block(args):
        q_blk, cq_blk, qpos = args
        s = jnp.einsum('bhqd,bhkd->bhqk', q_blk, kh, preferred_element_type=jnp.float32) * scale
        s = s + (cq_blk[..., :, None] - ck[:, :, None, :])
        s = jnp.where(qpos[:, None] >= kpos[None, :], s, -jnp.inf)
        w = jax.nn.softmax(s, axis=-1)
        return jnp.einsum('bhqk,bhkd->bhqd', w.astype(vh.dtype), vh)

    out = lax.map(one_block, (q_blocks, cq_blocks, pos_blocks))
    return out.transpose(1, 0, 3, 2, 4).reshape(B, S, H * Dh)


def gmlp_mixer(u, v, v_gain, w_s, b_s):
    B, S, _ = u.shape
    nc = S // CHUNK
    u = jax.nn.gelu(u)
    v = jax.nn.gelu(v).reshape(B, S, N_GMLP_HEADS, GMLP_HEAD_DIM)
    v = rms_norm(v, v_gain).reshape(B, nc, CHUNK, N_GMLP_HEADS, GMLP_HEAD_DIM)
    w = w_s * jnp.tril(jnp.ones((CHUNK, CHUNK), w_s.dtype))[None]
    mixed = jnp.einsum('gts,bnsgc->bntgc', w, v) + b_s.T[None, None, :, :, None]
    return u * mixed.reshape(B, S, GMLP_WIDTH)


def pool_mixer(xp, w_pool, scale):
    B, S, _ = xp.shape
    x4 = xp.reshape(B, S, N_POOL_GROUPS, POOL_GROUP_DIM)
    cs = jnp.cumsum(x4.astype(jnp.float32), axis=1)
    cpad = jnp.concatenate([jnp.zeros((B, 1, N_POOL_GROUPS, POOL_GROUP_DIM), jnp.float32), cs], axis=1)
    t = jnp.arange(S, dtype=jnp.int32)[:, None]
    win = jnp.asarray(POOL_WINDOWS, dtype=jnp.int32)[None, :]
    lo = jnp.maximum(t + 1 - win, 0)
    cnt = (t + 1 - lo).astype(jnp.float32)
    g_idx = jnp.arange(N_POOL_GROUPS, dtype=jnp.int32)[None, :]
    window_sum = cs - cpad[:, lo, g_idx, :]
    d = (window_sum / cnt[None, :, :, None] - x4.astype(jnp.float32)).astype(xp.dtype)
    y = jnp.einsum('bsgc,gcd->bsgd', d, w_pool)
    return y.reshape(B, S, POOL_WIDTH) * scale


def _fwd_setup_inputs(seed: int = 0) -> dict:
    key = jax.random.key(seed)
    ks = jax.random.split(key, 20)
    f32 = jnp.float32

    def nrm(k, shape, s):
        return jax.random.normal(k, shape, f32) * s

    def gain(k, shape):
        return 1.0 + 0.05 * jax.random.normal(k, shape, f32)

    return {
        "x": jax.random.normal(ks[0], (BATCH, SEQ, D_MODEL), f32),
        "p": jax.random.normal(ks[1], (DEPTH, BATCH, SEQ, D_PLE), f32),
        "norm_mix": gain(ks[2], (DEPTH, D_MODEL)),
        "w_in": nrm(ks[3], (DEPTH, D_MODEL, D_PROJ), D_MODEL ** -0.5),
        "q_norm": gain(ks[4], (DEPTH, HEAD_DIM)),
        "k_norm": gain(ks[5], (DEPTH, HEAD_DIM)),
        "forget_bias": 3.0 + 0.5 * jax.random.normal(ks[6], (DEPTH, N_ATTN_HEADS), f32),
        "gmlp_v_norm": gain(ks[7], (DEPTH, N_GMLP_HEADS, GMLP_HEAD_DIM)),
        "gmlp_w_s": nrm(ks[8], (DEPTH, N_GMLP_HEADS, CHUNK, CHUNK), CHUNK ** -0.5),
        "gmlp_b_s": 1.0 + 0.1 * jax.random.normal(ks[9], (DEPTH, N_GMLP_HEADS, CHUNK), f32),
        "pool_w": nrm(ks[10], (DEPTH, N_POOL_GROUPS, POOL_GROUP_DIM, POOL_GROUP_DIM), POOL_GROUP_DIM ** -0.5),
        "pool_scale": 1.0 + 0.1 * jax.random.normal(ks[11], (DEPTH, POOL_WIDTH), f32),
        "w_out": nrm(ks[12], (DEPTH, D_MIX, D_MODEL), D_MIX ** -0.5),
        "norm_ffn": gain(ks[13], (DEPTH, D_MODEL)),
        "w_ffn_gate": nrm(ks[14], (DEPTH, D_MODEL, D_FF), D_MODEL ** -0.5),
        "w_ffn_up": nrm(ks[15], (DEPTH, D_MODEL, D_FF), D_MODEL ** -0.5),
        "w_ffn_down": nrm(ks[16], (DEPTH, D_FF, D_MODEL), D_FF ** -0.5),
        "norm_ple": gain(ks[17], (DEPTH, D_MODEL)),
        "w_ple_gate": nrm(ks[18], (DEPTH, D_MODEL, D_MODEL), D_MODEL ** -0.5),
        "w_ple_proj": nrm(ks[19], (DEPTH, D_PLE, D_MODEL), D_PLE ** -0.5),
    }


def _fwd_reference(x, p, norm_mix, w_in, q_norm, k_norm, forget_bias, gmlp_v_norm, gmlp_w_s, gmlp_b_s,
              pool_w, pool_scale, w_out, norm_ffn, w_ffn_gate, w_ffn_up, w_ffn_down,
              norm_ple, w_ple_gate, w_ple_proj):
    B, S, _ = x.shape
    h = x
    for i in range(DEPTH):
        xn = rms_norm(h, norm_mix[i])
        proj = xn @ w_in[i]
        q, k, v, f_logit, gu, gv, xp = jnp.split(proj, SPLIT_POINTS, axis=-1)
        q = rms_norm(q.reshape(B, S, N_ATTN_HEADS, HEAD_DIM), q_norm[i])
        k = rms_norm(k.reshape(B, S, N_ATTN_HEADS, HEAD_DIM), k_norm[i])
        v = v.reshape(B, S, N_ATTN_HEADS, HEAD_DIM)
        log_f = jax.nn.log_sigmoid((f_logit + forget_bias[i]).astype(jnp.float32))
        y_attn = fox_attention(q, k, v, log_f)
        y_gmlp = gmlp_mixer(gu, gv, gmlp_v_norm[i], gmlp_w_s[i], gmlp_b_s[i])
        y_pool = pool_mixer(xp, pool_w[i], pool_scale[i])
        mix = jnp.concatenate([y_attn, y_gmlp, y_pool], axis=-1)
        h = h + mix @ w_out[i]
        xn = rms_norm(h, norm_ffn[i])
        h = h + (jax.nn.silu(xn @ w_ffn_gate[i]) * (xn @ w_ffn_up[i])) @ w_ffn_down[i]
        gate = jax.nn.sigmoid(rms_norm(h, norm_ple[i]) @ w_ple_gate[i])
        h = h + (p[i] @ w_ple_proj[i]) * gate
    return h


import jax as _jax
import jax.numpy as _jnp

TWIN_FORMAT = 'train_step'
FWD_PARAMS = ['x', 'p', 'norm_mix', 'w_in', 'q_norm', 'k_norm', 'forget_bias', 'gmlp_v_norm', 'gmlp_w_s', 'gmlp_b_s', 'pool_w', 'pool_scale', 'w_out', 'norm_ffn', 'w_ffn_gate', 'w_ffn_up', 'w_ffn_down', 'norm_ple', 'w_ple_gate', 'w_ple_proj']
TWIN_WEIGHTS = ['norm_mix', 'w_in', 'q_norm', 'k_norm', 'forget_bias', 'gmlp_v_norm', 'gmlp_w_s', 'gmlp_b_s', 'pool_w', 'pool_scale', 'w_out', 'norm_ffn', 'w_ffn_gate', 'w_ffn_up', 'w_ffn_down', 'norm_ple', 'w_ple_gate', 'w_ple_proj']
TWIN_DIFF_INPUT = 'x'
TWIN_INPUTS = ['x', 'p', 'norm_mix', 'w_in', 'q_norm', 'k_norm', 'forget_bias', 'gmlp_v_norm', 'gmlp_w_s', 'gmlp_b_s', 'pool_w', 'pool_scale', 'w_out', 'norm_ffn', 'w_ffn_gate', 'w_ffn_up', 'w_ffn_down', 'norm_ple', 'w_ple_gate', 'w_ple_proj', 'loss_target', 'm_norm_mix', 'm_w_in', 'm_q_norm', 'm_k_norm', 'm_forget_bias', 'm_gmlp_v_norm', 'm_gmlp_w_s', 'm_gmlp_b_s', 'm_pool_w', 'm_pool_scale', 'm_w_out', 'm_norm_ffn', 'm_w_ffn_gate', 'm_w_ffn_up', 'm_w_ffn_down', 'm_norm_ple', 'm_w_ple_gate', 'm_w_ple_proj', 'v_norm_mix', 'v_w_in', 'v_q_norm', 'v_k_norm', 'v_forget_bias', 'v_gmlp_v_norm', 'v_gmlp_w_s', 'v_gmlp_b_s', 'v_pool_w', 'v_pool_scale', 'v_w_out', 'v_norm_ffn', 'v_w_ffn_gate', 'v_w_ffn_up', 'v_w_ffn_down', 'v_norm_ple', 'v_w_ple_gate', 'v_w_ple_proj']
TWIN_OUTPUTS = ['loss', 'grad_x', 'grad_norm_mix', 'grad_w_in', 'grad_q_norm', 'grad_k_norm', 'grad_forget_bias', 'grad_gmlp_v_norm', 'grad_gmlp_w_s', 'grad_gmlp_b_s', 'grad_pool_w', 'grad_pool_scale', 'grad_w_out', 'grad_norm_ffn', 'grad_w_ffn_gate', 'grad_w_ffn_up', 'grad_w_ffn_down', 'grad_norm_ple', 'grad_w_ple_gate', 'grad_w_ple_proj', 'delta_norm_mix', 'delta_w_in', 'delta_q_norm', 'delta_k_norm', 'delta_forget_bias', 'delta_gmlp_v_norm', 'delta_gmlp_w_s', 'delta_gmlp_b_s', 'delta_pool_w', 'delta_pool_scale', 'delta_w_out', 'delta_norm_ffn', 'delta_w_ffn_gate', 'delta_w_ffn_up', 'delta_w_ffn_down', 'delta_norm_ple', 'delta_w_ple_gate', 'delta_w_ple_proj', 'new_m_norm_mix', 'new_m_w_in', 'new_m_q_norm', 'new_m_k_norm', 'new_m_forget_bias', 'new_m_gmlp_v_norm', 'new_m_gmlp_w_s', 'new_m_gmlp_b_s', 'new_m_pool_w', 'new_m_pool_scale', 'new_m_w_out', 'new_m_norm_ffn', 'new_m_w_ffn_gate', 'new_m_w_ffn_up', 'new_m_w_ffn_down', 'new_m_norm_ple', 'new_m_w_ple_gate', 'new_m_w_ple_proj', 'new_v_norm_mix', 'new_v_w_in', 'new_v_q_norm', 'new_v_k_norm', 'new_v_forget_bias', 'new_v_gmlp_v_norm', 'new_v_gmlp_w_s', 'new_v_gmlp_b_s', 'new_v_pool_w', 'new_v_pool_scale', 'new_v_w_out', 'new_v_norm_ffn', 'new_v_w_ffn_gate', 'new_v_w_ffn_up', 'new_v_w_ffn_down', 'new_v_norm_ple', 'new_v_w_ple_gate', 'new_v_w_ple_proj']
TWIN_LEAF_KINDS = {'loss': 'loss', 'grad_x': 'grad_x', 'grad_norm_mix': 'grad_w', 'grad_w_in': 'grad_w', 'grad_q_norm': 'grad_w', 'grad_k_norm': 'grad_w', 'grad_forget_bias': 'grad_w', 'grad_gmlp_v_norm': 'grad_w', 'grad_gmlp_w_s': 'grad_w', 'grad_gmlp_b_s': 'grad_w', 'grad_pool_w': 'grad_w', 'grad_pool_scale': 'grad_w', 'grad_w_out': 'grad_w', 'grad_norm_ffn': 'grad_w', 'grad_w_ffn_gate': 'grad_w', 'grad_w_ffn_up': 'grad_w', 'grad_w_ffn_down': 'grad_w', 'grad_norm_ple': 'grad_w', 'grad_w_ple_gate': 'grad_w', 'grad_w_ple_proj': 'grad_w', 'delta_norm_mix': 'delta_w', 'delta_w_in': 'delta_w', 'delta_q_norm': 'delta_w', 'delta_k_norm': 'delta_w', 'delta_forget_bias': 'delta_w', 'delta_gmlp_v_norm': 'delta_w', 'delta_gmlp_w_s': 'delta_w', 'delta_gmlp_b_s': 'delta_w', 'delta_pool_w': 'delta_w', 'delta_pool_scale': 'delta_w', 'delta_w_out': 'delta_w', 'delta_norm_ffn': 'delta_w', 'delta_w_ffn_gate': 'delta_w', 'delta_w_ffn_up': 'delta_w', 'delta_w_ffn_down': 'delta_w', 'delta_norm_ple': 'delta_w', 'delta_w_ple_gate': 'delta_w', 'delta_w_ple_proj': 'delta_w', 'new_m_norm_mix': 'new_m', 'new_m_w_in': 'new_m', 'new_m_q_norm': 'new_m', 'new_m_k_norm': 'new_m', 'new_m_forget_bias': 'new_m', 'new_m_gmlp_v_norm': 'new_m', 'new_m_gmlp_w_s': 'new_m', 'new_m_gmlp_b_s': 'new_m', 'new_m_pool_w': 'new_m', 'new_m_pool_scale': 'new_m', 'new_m_w_out': 'new_m', 'new_m_norm_ffn': 'new_m', 'new_m_w_ffn_gate': 'new_m', 'new_m_w_ffn_up': 'new_m', 'new_m_w_ffn_down': 'new_m', 'new_m_norm_ple': 'new_m', 'new_m_w_ple_gate': 'new_m', 'new_m_w_ple_proj': 'new_m', 'new_v_norm_mix': 'new_v', 'new_v_w_in': 'new_v', 'new_v_q_norm': 'new_v', 'new_v_k_norm': 'new_v', 'new_v_forget_bias': 'new_v', 'new_v_gmlp_v_norm': 'new_v', 'new_v_gmlp_w_s': 'new_v', 'new_v_gmlp_b_s': 'new_v', 'new_v_pool_w': 'new_v', 'new_v_pool_scale': 'new_v', 'new_v_w_out': 'new_v', 'new_v_norm_ffn': 'new_v', 'new_v_w_ffn_gate': 'new_v', 'new_v_w_ffn_up': 'new_v', 'new_v_w_ffn_down': 'new_v', 'new_v_norm_ple': 'new_v', 'new_v_w_ple_gate': 'new_v', 'new_v_w_ple_proj': 'new_v'}


def _forward(args):
    return _fwd_reference(*[args[k] for k in FWD_PARAMS])


def _output_shape():
    out = _jax.eval_shape(lambda: _forward(_fwd_setup_inputs(0)))
    return out.shape, out.dtype

N_MICROBATCH = 1
ADAM_LR = 0.001
ADAM_B1 = 0.9
ADAM_B2 = 0.999
ADAM_EPS = 1e-08
ADAM_WD = 0.01
ADAM_STEP = 10
PER_EXAMPLE_BATCH_AXIS = {'x': 0, 'p': 1, 'loss_target': 0}
SHARED_INPUTS = []
_WEIGHT_DTYPES = {'norm_mix': _jnp.float32, 'w_in': _jnp.float32, 'q_norm': _jnp.float32, 'k_norm': _jnp.float32, 'forget_bias': _jnp.float32, 'gmlp_v_norm': _jnp.float32, 'gmlp_w_s': _jnp.float32, 'gmlp_b_s': _jnp.float32, 'pool_w': _jnp.float32, 'pool_scale': _jnp.float32, 'w_out': _jnp.float32, 'norm_ffn': _jnp.float32, 'w_ffn_gate': _jnp.float32, 'w_ffn_up': _jnp.float32, 'w_ffn_down': _jnp.float32, 'norm_ple': _jnp.float32, 'w_ple_gate': _jnp.float32, 'w_ple_proj': _jnp.float32}
MOMENT_SCALE = {'norm_mix': 3.464326e+00, 'w_in': 2.812338e-01, 'q_norm': 2.694442e+00, 'k_norm': 2.696570e+00, 'forget_bias': 3.069090e+01, 'gmlp_v_norm': 1.752243e+00, 'gmlp_w_s': 1.090916e+00, 'gmlp_b_s': 3.553917e+00, 'pool_w': 5.988911e-01, 'pool_scale': 6.274855e+00, 'w_out': 5.959270e-01, 'norm_ffn': 6.172596e+00, 'w_ffn_gate': 1.271044e-01, 'w_ffn_up': 1.116862e-01, 'w_ffn_down': 1.788762e-01, 'norm_ple': 2.490047e-01, 'w_ple_gate': 9.625786e-02, 'w_ple_proj': 1.706124e-01}


def _to_microbatches(a, axis):
    t = _jnp.moveaxis(a, axis, 0)
    t = t.reshape((N_MICROBATCH, t.shape[0] // N_MICROBATCH) + t.shape[1:])
    return _jnp.moveaxis(t, 1, axis + 1)


def setup_inputs(seed: int = 0) -> dict:
    inp = _fwd_setup_inputs(seed)
    key = _jax.random.fold_in(_jax.random.key(seed), 7919)
    shape, _ = _output_shape()
    out = dict(inp)
    out["loss_target"] = _jax.random.normal(_jax.random.fold_in(key, 0), shape, _jnp.float32)
    for i, name in enumerate(TWIN_WEIGHTS):
        w = inp[name].astype(_jnp.float32)
        if MOMENT_SCALE is None:
            s = _jnp.sqrt(_jnp.mean(_jnp.square(w)) + 1e-30)
        else:
            s = MOMENT_SCALE[name]
        km, kv = _jax.random.split(_jax.random.fold_in(key, i + 1))
        out[name] = w
        out["m_" + name] = s * _jax.random.normal(km, w.shape, _jnp.float32)
        out["v_" + name] = (s * s) * _jax.random.uniform(kv, w.shape, _jnp.float32, 0.5, 1.5)
    if N_MICROBATCH > 1:
        for name, axis in PER_EXAMPLE_BATCH_AXIS.items():
            out[name] = _to_microbatches(out[name], axis)
    return {'x': out['x'], 'p': out['p'], 'norm_mix': out['norm_mix'], 'w_in': out['w_in'], 'q_norm': out['q_norm'], 'k_norm': out['k_norm'], 'forget_bias': out['forget_bias'], 'gmlp_v_norm': out['gmlp_v_norm'], 'gmlp_w_s': out['gmlp_w_s'], 'gmlp_b_s': out['gmlp_b_s'], 'pool_w': out['pool_w'], 'pool_scale': out['pool_scale'], 'w_out': out['w_out'], 'norm_ffn': out['norm_ffn'], 'w_ffn_gate': out['w_ffn_gate'], 'w_ffn_up': out['w_ffn_up'], 'w_ffn_down': out['w_ffn_down'], 'norm_ple': out['norm_ple'], 'w_ple_gate': out['w_ple_gate'], 'w_ple_proj': out['w_ple_proj'], 'loss_target': out['loss_target'], 'm_norm_mix': out['m_norm_mix'], 'm_w_in': out['m_w_in'], 'm_q_norm': out['m_q_norm'], 'm_k_norm': out['m_k_norm'], 'm_forget_bias': out['m_forget_bias'], 'm_gmlp_v_norm': out['m_gmlp_v_norm'], 'm_gmlp_w_s': out['m_gmlp_w_s'], 'm_gmlp_b_s': out['m_gmlp_b_s'], 'm_pool_w': out['m_pool_w'], 'm_pool_scale': out['m_pool_scale'], 'm_w_out': out['m_w_out'], 'm_norm_ffn': out['m_norm_ffn'], 'm_w_ffn_gate': out['m_w_ffn_gate'], 'm_w_ffn_up': out['m_w_ffn_up'], 'm_w_ffn_down': out['m_w_ffn_down'], 'm_norm_ple': out['m_norm_ple'], 'm_w_ple_gate': out['m_w_ple_gate'], 'm_w_ple_proj': out['m_w_ple_proj'], 'v_norm_mix': out['v_norm_mix'], 'v_w_in': out['v_w_in'], 'v_q_norm': out['v_q_norm'], 'v_k_norm': out['v_k_norm'], 'v_forget_bias': out['v_forget_bias'], 'v_gmlp_v_norm': out['v_gmlp_v_norm'], 'v_gmlp_w_s': out['v_gmlp_w_s'], 'v_gmlp_b_s': out['v_gmlp_b_s'], 'v_pool_w': out['v_pool_w'], 'v_pool_scale': out['v_pool_scale'], 'v_w_out': out['v_w_out'], 'v_norm_ffn': out['v_norm_ffn'], 'v_w_ffn_gate': out['v_w_ffn_gate'], 'v_w_ffn_up': out['v_w_ffn_up'], 'v_w_ffn_down': out['v_w_ffn_down'], 'v_norm_ple': out['v_norm_ple'], 'v_w_ple_gate': out['v_w_ple_gate'], 'v_w_ple_proj': out['v_w_ple_proj']}


def _loss(weights, diff, rest, loss_target):
    with _jax.named_scope("forward"):
        args = {**rest, TWIN_DIFF_INPUT: diff, **{k: w.astype(_WEIGHT_DTYPES[k]) for k, w in weights.items()}}
        y = _forward(args)
    with _jax.named_scope("loss_head"):
        err = _jnp.square(y.astype(_jnp.float32) - loss_target)
        return 0.5 * _jnp.sum(_jnp.mean(err, axis=-1)) if err.ndim else 0.5 * err


def _adamw(w, g, m, v):
    m = ADAM_B1 * m + (1.0 - ADAM_B1) * g
    v = ADAM_B2 * v + (1.0 - ADAM_B2) * _jnp.square(g)
    m_hat = m / (1.0 - ADAM_B1 ** ADAM_STEP)
    v_hat = v / (1.0 - ADAM_B2 ** ADAM_STEP)
    delta = -ADAM_LR * (m_hat / (_jnp.sqrt(v_hat) + ADAM_EPS) + ADAM_WD * w)
    return delta, m, v


def reference(x, p, norm_mix, w_in, q_norm, k_norm, forget_bias, gmlp_v_norm, gmlp_w_s, gmlp_b_s, pool_w, pool_scale, w_out, norm_ffn, w_ffn_gate, w_ffn_up, w_ffn_down, norm_ple, w_ple_gate, w_ple_proj, loss_target, m_norm_mix, m_w_in, m_q_norm, m_k_norm, m_forget_bias, m_gmlp_v_norm, m_gmlp_w_s, m_gmlp_b_s, m_pool_w, m_pool_scale, m_w_out, m_norm_ffn, m_w_ffn_gate, m_w_ffn_up, m_w_ffn_down, m_norm_ple, m_w_ple_gate, m_w_ple_proj, v_norm_mix, v_w_in, v_q_norm, v_k_norm, v_forget_bias, v_gmlp_v_norm, v_gmlp_w_s, v_gmlp_b_s, v_pool_w, v_pool_scale, v_w_out, v_norm_ffn, v_w_ffn_gate, v_w_ffn_up, v_w_ffn_down, v_norm_ple, v_w_ple_gate, v_w_ple_proj):
    given = dict(x=x, p=p, norm_mix=norm_mix, w_in=w_in, q_norm=q_norm, k_norm=k_norm, forget_bias=forget_bias, gmlp_v_norm=gmlp_v_norm, gmlp_w_s=gmlp_w_s, gmlp_b_s=gmlp_b_s, pool_w=pool_w, pool_scale=pool_scale, w_out=w_out, norm_ffn=norm_ffn, w_ffn_gate=w_ffn_gate, w_ffn_up=w_ffn_up, w_ffn_down=w_ffn_down, norm_ple=norm_ple, w_ple_gate=w_ple_gate, w_ple_proj=w_ple_proj, loss_target=loss_target, m_norm_mix=m_norm_mix, m_w_in=m_w_in, m_q_norm=m_q_norm, m_k_norm=m_k_norm, m_forget_bias=m_forget_bias, m_gmlp_v_norm=m_gmlp_v_norm, m_gmlp_w_s=m_gmlp_w_s, m_gmlp_b_s=m_gmlp_b_s, m_pool_w=m_pool_w, m_pool_scale=m_pool_scale, m_w_out=m_w_out, m_norm_ffn=m_norm_ffn, m_w_ffn_gate=m_w_ffn_gate, m_w_ffn_up=m_w_ffn_up, m_w_ffn_down=m_w_ffn_down, m_norm_ple=m_norm_ple, m_w_ple_gate=m_w_ple_gate, m_w_ple_proj=m_w_ple_proj, v_norm_mix=v_norm_mix, v_w_in=v_w_in, v_q_norm=v_q_norm, v_k_norm=v_k_norm, v_forget_bias=v_forget_bias, v_gmlp_v_norm=v_gmlp_v_norm, v_gmlp_w_s=v_gmlp_w_s, v_gmlp_b_s=v_gmlp_b_s, v_pool_w=v_pool_w, v_pool_scale=v_pool_scale, v_w_out=v_w_out, v_norm_ffn=v_norm_ffn, v_w_ffn_gate=v_w_ffn_gate, v_w_ffn_up=v_w_ffn_up, v_w_ffn_down=v_w_ffn_down, v_norm_ple=v_norm_ple, v_w_ple_gate=v_w_ple_gate, v_w_ple_proj=v_w_ple_proj)
    weights = {n: given[n] for n in TWIN_WEIGHTS}
    shared = {n: given[n] for n in SHARED_INPUTS}
    per_example = {n: given[n] for n in ['x', 'p']}
    grad_fn = _jax.value_and_grad(_loss, argnums=(0, 1))

    def one_microbatch(ex, loss_target):
        ex = dict(ex)
        diff = ex.pop(TWIN_DIFF_INPUT)
        return grad_fn(weights, diff, {**shared, **ex}, loss_target)

    if N_MICROBATCH == 1:
        loss, (grad_w, grad_x) = one_microbatch(per_example, given["loss_target"])
    else:
        def body(carry, xs):
            loss_sum, grad_sum = carry
            l_k, (gw_k, gx_k) = one_microbatch(xs[0], xs[1])
            with _jax.named_scope("update"):
                return (loss_sum + l_k, _jax.tree.map(_jnp.add, grad_sum, gw_k)), gx_k

        init = (_jnp.zeros((), _jnp.float32), _jax.tree.map(_jnp.zeros_like, weights))
        (loss, grad_w), grad_x = _jax.lax.scan(body, init, (per_example, given["loss_target"]))
    with _jax.named_scope("update"):
        delta_w, new_m, new_v = {}, {}, {}
        for n in TWIN_WEIGHTS:
            delta_w[n], new_m[n], new_v[n] = _adamw(weights[n], grad_w[n], given["m_" + n], given["v_" + n])
    return (loss, grad_x, *[grad_w[n] for n in TWIN_WEIGHTS], *[delta_w[n] for n in TWIN_WEIGHTS],
            *[new_m[n] for n in TWIN_WEIGHTS], *[new_v[n] for n in TWIN_WEIGHTS])
```

```python
import functools

import jax
import jax.numpy as jnp
from jax import lax
from jax.experimental import pallas as pl
from jax.experimental.pallas import tpu as pltpu

F32 = jnp.float32
BF16 = jnp.bfloat16
EPS = 1e-6
HEAD = 128
N_HEADS = 8
N_GROUPS = 4
POOL_WINDOWS = (2, 4, 8, 16)
N_DEV = 8
ATTN_Q_BLOCK = 256

OFF_Q, OFF_K, OFF_V, OFF_F, OFF_GU, OFF_GV, OFF_XP, PROJ_PAD = 0, 1024, 2048, 3072, 3584, 4096, 4608, 5120
F_COLS = 8
PROJ_RAW = 4616

ADAM_LR, ADAM_B1, ADAM_B2, ADAM_EPS, ADAM_WD, ADAM_STEP = 0.001, 0.9, 0.999, 1e-08, 0.01, 10

BIG = ("w_in", "w_out", "w_ffn_gate", "w_ffn_up", "w_ffn_down", "w_ple_gate", "w_ple_proj")
SMALL = ("norm_mix", "q_norm", "k_norm", "forget_bias", "gmlp_v_norm", "gmlp_w_s", "gmlp_b_s", "pool_w",
         "pool_scale", "norm_ffn", "norm_ple")
WEIGHTS = ("norm_mix", "w_in", "q_norm", "k_norm", "forget_bias", "gmlp_v_norm", "gmlp_w_s", "gmlp_b_s", "pool_w",
           "pool_scale", "w_out", "norm_ffn", "w_ffn_gate", "w_ffn_up", "w_ffn_down", "norm_ple", "w_ple_gate",
           "w_ple_proj")

VMEM_LIMIT = 56 * 1024 * 1024

BS = pl.BlockSpec
SDS = jax.ShapeDtypeStruct
ANY = pl.BlockSpec(memory_space=pl.ANY)


def _params(*sem):
    return pltpu.CompilerParams(dimension_semantics=sem, vmem_limit_bytes=VMEM_LIMIT)


def _dot(a, b, mode="nn", precision=None):
    ca, cb = {"nn": (1, 0), "nt": (1, 1), "tn": (0, 0)}[mode]
    return lax.dot_general(a, b, (((ca,), (cb,)), ((), ())), preferred_element_type=F32, precision=precision)


def _rms_scale(x):
    return lax.rsqrt(jnp.mean(x * x, axis=-1, keepdims=True) + EPS)


def _rms_bwd(g, xhat, r):
    return r * (g - xhat * jnp.mean(g * xhat, axis=-1, keepdims=True))


def _gelu(x, with_grad=False):
    k = 0.7978845608028654
    inner = k * (x + 0.044715 * x * x * x)
    t = jnp.tanh(inner)
    y = 0.5 * x * (1.0 + t)
    if not with_grad:
        return y
    dy = 0.5 * (1.0 + t) + 0.5 * x * (1.0 - t * t) * k * (1.0 + 3.0 * 0.044715 * x * x)
    return y, dy


def _sigmoid(x):
    return 1.0 / (1.0 + jnp.exp(-x))


def _tile(n, want):
    t = min(n, want)
    assert n % t == 0, (n, want)
    return t


def _matmul(name, a, b, mode, out_dtype, tm, tn, res=None):
    if mode == "nn":
        (m, k), n = a.shape, b.shape[1]
        a_spec, b_spec = BS((tm, k), lambda i, j: (i, 0)), BS((k, tn), lambda i, j: (0, j))
    elif mode == "nt":
        (m, k), n = a.shape, b.shape[0]
        a_spec, b_spec = BS((tm, k), lambda i, j: (i, 0)), BS((tn, k), lambda i, j: (j, 0))
    else:
        (k, m), n = a.shape, b.shape[1]
        a_spec, b_spec = BS((k, tm), lambda i, j: (0, i)), BS((k, tn), lambda i, j: (0, j))
    assert m % tm == 0 and n % tn == 0
    o_spec = BS((tm, tn), lambda i, j: (i, j))

    def body(a_ref, b_ref, *rest):
        o_ref = rest[-1]
        acc = _dot(a_ref[...], b_ref[...], mode)
        if res is not None:
            acc = acc + rest[0][...]
        o_ref[...] = acc.astype(out_dtype)

    ins, specs = [a, b], [a_spec, b_spec]
    if res is not None:
        ins.append(res)
        specs.append(o_spec)
    return pl.pallas_call(body, out_shape=SDS((m, n), out_dtype), grid=(m // tm, n // tn), in_specs=specs,
                          out_specs=o_spec, name=name, compiler_params=_params("parallel", "parallel"))(*ins)


def _rms_fwd(name, h, gain):
    t, d = h.shape
    tm = _tile(t, 256)

    def body(h_ref, g_ref, o_ref):
        x = h_ref[...]
        o_ref[...] = (x * _rms_scale(x) * g_ref[...]).astype(BF16)

    return pl.pallas_call(body, out_shape=SDS((t, d), BF16), grid=(t // tm,),
                          in_specs=[BS((tm, d), lambda i: (i, 0)), BS((1, d), lambda i: (0, 0))],
                          out_specs=BS((tm, d), lambda i: (i, 0)), name=name, compiler_params=_params("parallel"))(h, gain)


def _rms_bwd_call(name, dxn, h, gain, dres):
    t, d = h.shape
    tm = _tile(t, 256)

    def body(dxn_ref, h_ref, g_ref, dres_ref, dh_ref, dhb_ref, dg_ref):
        x = h_ref[...]
        r = _rms_scale(x)
        xhat = x * r
        dy = dxn_ref[...]
        dh = dres_ref[...] + _rms_bwd(dy * g_ref[...], xhat, r)
        dh_ref[...] = dh
        dhb_ref[...] = dh.astype(BF16)

        @pl.when(pl.program_id(0) == 0)
        def _():
            dg_ref[...] = jnp.zeros_like(dg_ref)

        dg_ref[...] += jnp.sum(dy * xhat, axis=0, keepdims=True)

    row = BS((tm, d), lambda i: (i, 0))
    vec = BS((1, d), lambda i: (0, 0))
    return pl.pallas_call(body, out_shape=(SDS((t, d), F32), SDS((t, d), BF16), SDS((1, d), F32)), grid=(t // tm,),
                          in_specs=[row, row, vec, row], out_specs=(row, row, vec), name=name,
                          compiler_params=_params("arbitrary"))(dxn, h, gain, dres)


def _fgate_fwd(proj, bias):
    t = proj.shape[0]
    nb = t // HEAD

    def body(f_ref, b_ref, c_ref):
        tri = (lax.broadcasted_iota(jnp.int32, (HEAD, HEAD), 0) >= lax.broadcasted_iota(jnp.int32, (HEAD, HEAD), 1)).astype(F32)
        carry = jnp.zeros((1, HEAD), F32)
        for n in range(nb):
            rows = slice(n * HEAD, (n + 1) * HEAD)
            x = f_ref[rows, :] + b_ref[...]
            lf = jnp.minimum(x, 0.0) - jnp.log(1.0 + jnp.exp(-jnp.abs(x)))
            cb = _dot(tri, lf, precision=lax.Precision.HIGHEST) + carry
            c_ref[rows, :] = cb
            carry = cb[HEAD - 1:HEAD, :]

    return pl.pallas_call(body, out_shape=SDS((t, HEAD), F32), grid=(1,),
                          in_specs=[BS((t, HEAD), lambda i: (0, OFF_F // HEAD)), BS((1, HEAD), lambda i: (0, 0))],
                          out_specs=BS((t, HEAD), lambda i: (0, 0)), name="fgate_fwd",
                          compiler_params=_params("arbitrary"))(proj, bias)


def _fgate_bwd(dc, proj, bias):
    t = proj.shape[0]
    nb = t // HEAD
    width = OFF_GU - OFF_F

    def body(dc_ref, f_ref, b_ref, df_ref, db_ref):
        tri = (lax.broadcasted_iota(jnp.int32, (HEAD, HEAD), 0) <= lax.broadcasted_iota(jnp.int32, (HEAD, HEAD), 1)).astype(F32)
        carry = jnp.zeros((1, HEAD), F32)
        db = jnp.zeros((1, HEAD), F32)
        df_ref[:, HEAD:] = jnp.zeros((t, width - HEAD), BF16)
        for n in reversed(range(nb)):
            rows = slice(n * HEAD, (n + 1) * HEAD)
            dlf = _dot(tri, dc_ref[rows, :], precision=lax.Precision.HIGHEST) + carry
            carry = dlf[0:1, :]
            x = f_ref[rows, :] + b_ref[...]
            df = dlf * _sigmoid(-x)
            df_ref[rows, 0:HEAD] = df.astype(BF16)
            db = db + jnp.sum(df, axis=0, keepdims=True)
        db_ref[...] = db

    return pl.pallas_call(body, out_shape=(SDS((t, width), BF16), SDS((1, HEAD), F32)), grid=(1,),
                          in_specs=[BS((t, HEAD), lambda i: (0, 0)), BS((t, HEAD), lambda i: (0, OFF_F // HEAD)),
                                    BS((1, HEAD), lambda i: (0, 0))],
                          out_specs=(BS((t, width), lambda i: (0, 0)), BS((1, HEAD), lambda i: (0, 0))),
                          name="fgate_bwd", compiler_params=_params("arbitrary"))(dc, proj, bias)


def _attn_masked_logits(qs_ref, kn_ref, cq_ref, ck_ref, i, tq):
    lo, hi = i * tq, (i + 1) * tq
    s = _dot(qs_ref[lo:hi, :], kn_ref[0:hi, :], "nt")
    s = s + cq_ref[0, lo:hi, :] - ck_ref[0, :, 0:hi]
    row = lax.broadcasted_iota(jnp.int32, (tq, hi), 0) + lo
    col = lax.broadcasted_iota(jnp.int32, (tq, hi), 1)
    return s, row >= col


def _attn_fwd(proj, cq, ck, qg, kg):
    t = proj.shape[0]
    tq = _tile(t, ATTN_Q_BLOCK)
    nq = t // tq
    scale = HEAD ** -0.5

    def body(q_ref, k_ref, v_ref, cq_ref, ck_ref, qg_ref, kg_ref, o_ref, lse_ref, qs_ref, kn_ref, vb_ref):
        q = q_ref[...]
        k = k_ref[...]
        qs_ref[...] = (q * _rms_scale(q) * qg_ref[...] * scale).astype(BF16)
        kn_ref[...] = (k * _rms_scale(k) * kg_ref[...]).astype(BF16)
        vb_ref[...] = v_ref[...].astype(BF16)
        for i in range(nq):
            lo, hi = i * tq, (i + 1) * tq
            s, keep = _attn_masked_logits(qs_ref, kn_ref, cq_ref, ck_ref, i, tq)
            s = jnp.where(keep, s, -1e30)
            m = jnp.max(s, axis=-1, keepdims=True)
            e = jnp.exp(s - m)
            l = jnp.sum(e, axis=-1, keepdims=True)
            o = _dot(e.astype(BF16), vb_ref[0:hi, :]) / l
            o_ref[lo:hi, :] = o.astype(BF16)
            lse_ref[0, lo:hi, :] = m + jnp.log(l)

    def col(off):
        return BS((t, HEAD), lambda h: (0, off // HEAD + h))

    vec = BS((1, HEAD), lambda h: (0, 0))
    return pl.pallas_call(
        body, out_shape=(SDS((t, N_HEADS * HEAD), BF16), SDS((N_HEADS, t, 1), F32)), grid=(N_HEADS,),
        in_specs=[col(OFF_Q), col(OFF_K), col(OFF_V), BS((1, t, 1), lambda h: (h, 0, 0)), BS((1, 1, t), lambda h: (h, 0, 0)),
                  vec, vec],
        out_specs=(BS((t, HEAD), lambda h: (0, h)), BS((1, t, 1), lambda h: (h, 0, 0))),
        scratch_shapes=[pltpu.VMEM((t, HEAD), BF16)] * 3, name="attn_fwd",
        compiler_params=_params("parallel"))(proj, proj, proj, cq, ck, qg, kg)


def _attn_bwd(proj, cq, ck, lse, dmix, qg, kg):
    t = proj.shape[0]
    tq = _tile(t, ATTN_Q_BLOCK)
    nq = t // tq
    scale = HEAD ** -0.5

    def body(q_ref, k_ref, v_ref, cq_ref, ck_ref, lse_ref, do_ref, qg_ref, kg_ref,
             dq_ref, dk_ref, dv_ref, dcq_ref, dck_ref, dqg_ref, dkg_ref,
             qs_ref, kn_ref, vb_ref, dob_ref, dqs_ref, dkn_ref, dva_ref):
        q = q_ref[...]
        k = k_ref[...]
        rq = _rms_scale(q)
        rk = _rms_scale(k)
        qs_ref[...] = (q * rq * qg_ref[...] * scale).astype(BF16)
        kn_ref[...] = (k * rk * kg_ref[...]).astype(BF16)
        vb_ref[...] = v_ref[...].astype(BF16)
        dob_ref[...] = do_ref[...].astype(BF16)
        dkn_ref[...] = jnp.zeros_like(dkn_ref)
        dva_ref[...] = jnp.zeros_like(dva_ref)
        dck_ref[...] = jnp.zeros_like(dck_ref)
        for i in range(nq):
            lo, hi = i * tq, (i + 1) * tq
            s, keep = _attn_masked_logits(qs_ref, kn_ref, cq_ref, ck_ref, i, tq)
            pr = jnp.where(keep, jnp.exp(s - lse_ref[0, lo:hi, :]), 0.0)
            dp = _dot(dob_ref[lo:hi, :], vb_ref[0:hi, :], "nt")
            delta = jnp.sum(pr * dp, axis=-1, keepdims=True)
            ds = pr * (dp - delta)
            dcq_ref[0, lo:hi, :] = jnp.sum(ds, axis=-1, keepdims=True)
            dck_ref[0, :, 0:hi] += -jnp.sum(ds, axis=0, keepdims=True)
            dsb = ds.astype(BF16)
            dqs_ref[lo:hi, :] = _dot(dsb, kn_ref[0:hi, :])
            dkn_ref[0:hi, :] += _dot(dsb, qs_ref[lo:hi, :], "tn")
            dva_ref[0:hi, :] += _dot(pr.astype(BF16), dob_ref[lo:hi, :], "tn")
        dv_ref[...] = dva_ref[...].astype(BF16)

        @pl.when(pl.program_id(0) == 0)
        def _():
            dqg_ref[...] = jnp.zeros_like(dqg_ref)
            dkg_ref[...] = jnp.zeros_like(dkg_ref)

        qhat = q * rq
        dqn = dqs_ref[...] * scale
        dqg_ref[...] += jnp.sum(dqn * qhat, axis=0, keepdims=True)
        dq_ref[...] = _rms_bwd(dqn * qg_ref[...], qhat, rq).astype(BF16)
        khat = k * rk
        dkn = dkn_ref[...]
        dkg_ref[...] += jnp.sum(dkn * khat, axis=0, keepdims=True)
        dk_ref[...] = _rms_bwd(dkn * kg_ref[...], khat, rk).astype(BF16)

    def col(off):
        return BS((t, HEAD), lambda h: (0, off // HEAD + h))

    vec = BS((1, HEAD), lambda h: (0, 0))
    c_col = BS((1, t, 1), lambda h: (h, 0, 0))
    c_row = BS((1, 1, t), lambda h: (h, 0, 0))
    head_out = BS((t, HEAD), lambda h: (0, h))
    width = N_HEADS * HEAD
    return pl.pallas_call(
        body,
        out_shape=(SDS((t, width), BF16), SDS((t, width), BF16), SDS((t, width), BF16), SDS((N_HEADS, t, 1), F32),
                   SDS((N_HEADS, 1, t), F32), SDS((1, HEAD), F32), SDS((1, HEAD), F32)),
        grid=(N_HEADS,),
        in_specs=[col(OFF_Q), col(OFF_K), col(OFF_V), c_col, c_row, c_col, head_out, vec, vec],
        out_specs=(head_out, head_out, head_out, c_col, c_row, vec, vec),
        scratch_shapes=[pltpu.VMEM((t, HEAD), BF16)] * 4 + [pltpu.VMEM((t, HEAD), F32)] * 3, name="attn_bwd",
        compiler_params=_params("arbitrary"))(proj, proj, proj, cq, ck, lse, dmix, qg, kg)


def _group_cols(off):
    width = N_GROUPS * HEAD
    return lambda t: BS((t, width), lambda i: (0, off // width))


def _tril():
    return lax.broadcasted_iota(jnp.int32, (HEAD, HEAD), 0) >= lax.broadcasted_iota(jnp.int32, (HEAD, HEAD), 1)


def _gmlp_fwd(proj, gain, w_s, b_s):
    t = proj.shape[0]
    width = N_GROUPS * HEAD
    nc = t // HEAD

    def body(gu_ref, gv_ref, gain_ref, ws_ref, bs_ref, o_ref):
        tril = _tril()
        for g in range(N_GROUPS):
            cols = slice(g * HEAD, (g + 1) * HEAD)
            u = _gelu(gu_ref[:, cols])
            vv = _gelu(gv_ref[:, cols])
            vn = (vv * _rms_scale(vv) * gain_ref[:, cols]).astype(BF16)
            w = jnp.where(tril, ws_ref[g], 0.0).astype(BF16)
            for n in range(nc):
                rows = slice(n * HEAD, (n + 1) * HEAD)
                mixed = _dot(w, vn[rows]) + bs_ref[g]
                o_ref[rows, cols] = (u[rows] * mixed).astype(BF16)

    full = lambda shape: BS(shape, lambda i: (0,) * len(shape))
    return pl.pallas_call(body, out_shape=SDS((t, width), BF16), grid=(1,),
                          in_specs=[_group_cols(OFF_GU)(t), _group_cols(OFF_GV)(t), full((1, width)),
                                    full((N_GROUPS, HEAD, HEAD)), full((N_GROUPS, HEAD, 1))],
                          out_specs=full((t, width)), name="gmlp_fwd",
                          compiler_params=_params("arbitrary"))(proj, proj, gain, w_s, b_s)


def _gmlp_bwd(dmix, proj, gain, w_s, b_s):
    t = proj.shape[0]
    width = N_GROUPS * HEAD
    nc = t // HEAD

    def body(dy_ref, gu_ref, gv_ref, gain_ref, ws_ref, bs_ref, dgu_ref, dgv_ref, dgain_ref, dws_ref, dbs_ref, dvn_ref):
        tril = _tril()
        for g in range(N_GROUPS):
            cols = slice(g * HEAD, (g + 1) * HEAD)
            u, du = _gelu(gu_ref[:, cols], with_grad=True)
            vv, dvv = _gelu(gv_ref[:, cols], with_grad=True)
            r = _rms_scale(vv)
            vhat = vv * r
            gain_g = gain_ref[:, cols]
            vn = (vhat * gain_g).astype(BF16)
            w = jnp.where(tril, ws_ref[g], 0.0).astype(BF16)
            dws = jnp.zeros((HEAD, HEAD), F32)
            dbs = jnp.zeros((HEAD, 1), F32)
            for n in range(nc):
                rows = slice(n * HEAD, (n + 1) * HEAD)
                mixed = _dot(w, vn[rows]) + bs_ref[g]
                dy = dy_ref[rows, cols]
                dgu_ref[rows, cols] = (dy * mixed * du[rows]).astype(BF16)
                dm = dy * u[rows]
                dmb = dm.astype(BF16)
                dbs = dbs + jnp.sum(dm, axis=-1, keepdims=True)
                dws = dws + _dot(dmb, vn[rows], "nt")
                dvn_ref[rows, :] = _dot(w, dmb, "tn")
            dvn = dvn_ref[...]
            dgain_ref[:, cols] = jnp.sum(dvn * vhat, axis=0, keepdims=True)
            dgv_ref[:, cols] = (_rms_bwd(dvn * gain_g, vhat, r) * dvv).astype(BF16)
            dws_ref[g] = jnp.where(tril, dws, 0.0)
            dbs_ref[g] = dbs

    full = lambda shape: BS(shape, lambda i: (0,) * len(shape))
    return pl.pallas_call(
        body,
        out_shape=(SDS((t, width), BF16), SDS((t, width), BF16), SDS((1, width), F32), SDS((N_GROUPS, HEAD, HEAD), F32),
                   SDS((N_GROUPS, HEAD, 1), F32)),
        grid=(1,),
        in_specs=[BS((t, width), lambda i: (0, 2)), _group_cols(OFF_GU)(t), _group_cols(OFF_GV)(t), full((1, width)),
                  full((N_GROUPS, HEAD, HEAD)), full((N_GROUPS, HEAD, 1))],
        out_specs=(full((t, width)), full((t, width)), full((1, width)), full((N_GROUPS, HEAD, HEAD)),
                   full((N_GROUPS, HEAD, 1))),
        scratch_shapes=[pltpu.VMEM((t, HEAD), F32)], name="gmlp_bwd",
        compiler_params=_params("arbitrary"))(dmix, proj, proj, gain, w_s, b_s)


def _pool_window_mean_minus_x(x, window, t_idx):
    s, span = x, 1
    while span < window:
        s = s + jnp.where(t_idx >= span, pltpu.roll(s, span, 0), 0.0)
        span *= 2
    cnt = jnp.minimum(t_idx + 1, window).astype(F32)
    return s / cnt - x, cnt


def _pool_fwd(proj, w_pool, scale):
    t = proj.shape[0]
    width = N_GROUPS * HEAD

    def body(x_ref, w_ref, sc_ref, o_ref):
        t_idx = lax.broadcasted_iota(jnp.int32, (t, HEAD), 0)
        for g in range(N_GROUPS):
            cols = slice(g * HEAD, (g + 1) * HEAD)
            d, _ = _pool_window_mean_minus_x(x_ref[:, cols], POOL_WINDOWS[g], t_idx)
            y = _dot(d.astype(BF16), w_ref[g].astype(BF16)) * sc_ref[:, cols]
            o_ref[:, cols] = y.astype(BF16)

    full = lambda shape: BS(shape, lambda i: (0,) * len(shape))
    return pl.pallas_call(body, out_shape=SDS((t, width), BF16), grid=(1,),
                          in_specs=[_group_cols(OFF_XP)(t), full((N_GROUPS, HEAD, HEAD)), full((1, width))],
                          out_specs=full((t, width)), name="pool_fwd",
                          compiler_params=_params("arbitrary"))(proj, w_pool, scale)


def _pool_bwd(dmix, proj, w_pool, scale):
    t = proj.shape[0]
    width = N_GROUPS * HEAD

    def body(dy_ref, x_ref, w_ref, sc_ref, dx_ref, dw_ref, dsc_ref):
        t_idx = lax.broadcasted_iota(jnp.int32, (t, HEAD), 0)
        for g in range(N_GROUPS):
            cols = slice(g * HEAD, (g + 1) * HEAD)
            window = POOL_WINDOWS[g]
            d, cnt = _pool_window_mean_minus_x(x_ref[:, cols], window, t_idx)
            db = d.astype(BF16)
            wb = w_ref[g].astype(BF16)
            dy = dy_ref[:, cols]
            dsc_ref[:, cols] = jnp.sum(dy * _dot(db, wb), axis=0, keepdims=True)
            dyw = (dy * sc_ref[:, cols]).astype(BF16)
            dw_ref[g] = _dot(db, dyw, "tn")
            dd = _dot(dyw, wb, "nt")
            rsum, span = dd / cnt, 1
            while span < window:
                rsum = rsum + jnp.where(t_idx < t - span, pltpu.roll(rsum, t - span, 0), 0.0)
                span *= 2
            dx_ref[:, cols] = (rsum - dd).astype(BF16)

    full = lambda shape: BS(shape, lambda i: (0,) * len(shape))
    return pl.pallas_call(
        body, out_shape=(SDS((t, width), BF16), SDS((N_GROUPS, HEAD, HEAD), F32), SDS((1, width), F32)), grid=(1,),
        in_specs=[BS((t, width), lambda i: (0, 3)), _group_cols(OFF_XP)(t), full((N_GROUPS, HEAD, HEAD)), full((1, width))],
        out_specs=(full((t, width)), full((N_GROUPS, HEAD, HEAD)), full((1, width))), name="pool_bwd",
        compiler_params=_params("arbitrary"))(dmix, proj, w_pool, scale)


def _ffn_fwd(xn, wg, wu):
    t, d = xn.shape
    fs = wg.shape[2]
    tm = _tile(t, 512)

    def body(x_ref, wg_ref, wu_ref, a_ref, b_ref, hh_ref):
        x = x_ref[...]
        a = _dot(x, wg_ref[0])
        b = _dot(x, wu_ref[0])
        a_ref[0] = a
        b_ref[0] = b
        hh_ref[0] = (a * _sigmoid(a) * b).astype(BF16)

    w_spec = BS((1, d, fs), lambda j, i: (j, 0, 0))
    o_spec = BS((1, tm, fs), lambda j, i: (j, i, 0))
    return pl.pallas_call(body, out_shape=(SDS((N_DEV, t, fs), F32), SDS((N_DEV, t, fs), F32), SDS((N_DEV, t, fs), BF16)),
                          grid=(N_DEV, t // tm), in_specs=[BS((tm, d), lambda j, i: (i, 0)), w_spec, w_spec],
                          out_specs=(o_spec, o_spec, o_spec), name="ffn_fwd",
                          compiler_params=_params("parallel", "parallel"))(xn, wg, wu)


def _ffn_down(hh, wd, res):
    _, t, fs = hh.shape
    d = wd.shape[2]
    tm, tn = _tile(t, 1024), _tile(d, 1024)

    def body(a_ref, b_ref, r_ref, o_ref, acc_ref):
        k = pl.program_id(2)

        @pl.when(k == 0)
        def _():
            acc_ref[...] = r_ref[...]

        acc_ref[...] += _dot(a_ref[0], b_ref[0])

        @pl.when(k == N_DEV - 1)
        def _():
            o_ref[...] = acc_ref[...]

    o_spec = BS((tm, tn), lambda i, j, k: (i, j))
    return pl.pallas_call(body, out_shape=SDS((t, d), F32), grid=(t // tm, d // tn, N_DEV),
                          in_specs=[BS((1, tm, fs), lambda i, j, k: (k, i, 0)), BS((1, fs, tn), lambda i, j, k: (k, 0, j)), o_spec],
                          out_specs=o_spec, scratch_shapes=[pltpu.VMEM((tm, tn), F32)], name="ffn_down",
                          compiler_params=_params("parallel", "parallel", "arbitrary"))(hh, wd, res)


def _ffn_bwd_hidden(dh_bf, wd, a, b):
    t, d = dh_bf.shape
    fs = wd.shape[1]
    tm = _tile(t, 512)

    def body(dh_ref, wd_ref, a_ref, b_ref, da_ref, db_ref):
        dhh = _dot(dh_ref[...], wd_ref[0], "nt")
        av = a_ref[0]
        sig = _sigmoid(av)
        da_ref[0] = (dhh * b_ref[0] * sig * (1.0 + av * (1.0 - sig))).astype(BF16)
        db_ref[0] = (dhh * av * sig).astype(BF16)

    o_spec = BS((1, tm, fs), lambda j, i: (j, i, 0))
    return pl.pallas_call(body, out_shape=(SDS((N_DEV, t, fs), BF16), SDS((N_DEV, t, fs), BF16)), grid=(N_DEV, t // tm),
                          in_specs=[BS((tm, d), lambda j, i: (i, 0)), BS((1, fs, d), lambda j, i: (j, 0, 0)), o_spec, o_spec],
                          out_specs=(o_spec, o_spec), name="ffn_bwd_hidden",
                          compiler_params=_params("parallel", "parallel"))(dh_bf, wd, a, b)


def _ffn_dwd(hh, dh_bf):
    _, t, fs = hh.shape
    d = dh_bf.shape[1]
    tn = _tile(d, 1024)

    def body(a_ref, b_ref, o_ref):
        o_ref[0] = _dot(a_ref[0], b_ref[...], "tn").astype(BF16)

    return pl.pallas_call(body, out_shape=SDS((N_DEV, fs, d), BF16), grid=(N_DEV, d // tn),
                          in_specs=[BS((1, t, fs), lambda j, n: (j, 0, 0)), BS((t, tn), lambda j, n: (0, n))],
                          out_specs=BS((1, fs, tn), lambda j, n: (j, 0, n)), name="ffn_dwd",
                          compiler_params=_params("parallel", "parallel"))(hh, dh_bf)


def _ffn_dwgu(xn, da, db):
    t, d = xn.shape
    fs = da.shape[2]
    tm = _tile(d, 1024)

    def body(x_ref, da_ref, db_ref, dg_ref, du_ref):
        x = x_ref[...]
        dg_ref[0] = _dot(x, da_ref[0], "tn").astype(BF16)
        du_ref[0] = _dot(x, db_ref[0], "tn").astype(BF16)

    g_spec = BS((1, t, fs), lambda j, i: (j, 0, 0))
    o_spec = BS((1, tm, fs), lambda j, i: (j, i, 0))
    return pl.pallas_call(body, out_shape=(SDS((N_DEV, d, fs), BF16), SDS((N_DEV, d, fs), BF16)), grid=(N_DEV, d // tm),
                          in_specs=[BS((t, tm), lambda j, i: (0, i)), g_spec, g_spec], out_specs=(o_spec, o_spec),
                          name="ffn_dwgu", compiler_params=_params("parallel", "parallel"))(xn, da, db)


def _ffn_dxn(da, db, wg, wu):
    _, t, fs = da.shape
    d = wg.shape[1]
    tm, tn = _tile(t, 1024), _tile(d, 1024)

    def body(da_ref, db_ref, wg_ref, wu_ref, o_ref, acc_ref):
        k = pl.program_id(2)

        @pl.when(k == 0)
        def _():
            acc_ref[...] = jnp.zeros_like(acc_ref)

        acc_ref[...] += _dot(da_ref[0], wg_ref[0], "nt") + _dot(db_ref[0], wu_ref[0], "nt")

        @pl.when(k == N_DEV - 1)
        def _():
            o_ref[...] = acc_ref[...]

    g_spec = BS((1, tm, fs), lambda i, j, k: (k, i, 0))
    w_spec = BS((1, tn, fs), lambda i, j, k: (k, j, 0))
    return pl.pallas_call(body, out_shape=SDS((t, d), F32), grid=(t // tm, d // tn, N_DEV),
                          in_specs=[g_spec, g_spec, w_spec, w_spec], out_specs=BS((tm, tn), lambda i, j, k: (i, j)),
                          scratch_shapes=[pltpu.VMEM((tm, tn), F32)], name="ffn_dxn",
                          compiler_params=_params("parallel", "parallel", "arbitrary"))(da, db, wg, wu)


def _ple_fwd(xn, wpg, p_bf, wpp, h):
    t, d = xn.shape
    dp = p_bf.shape[1]
    tn = wpp.shape[2]
    tm = _tile(t, 1024)

    def body(x_ref, wg_ref, p_ref, wp_ref, h_ref, o_ref, z_ref, pp_ref):
        z = _dot(x_ref[...], wg_ref[...])
        pp = _dot(p_ref[...], wp_ref[0])
        z_ref[...] = z
        pp_ref[...] = pp
        o_ref[...] = h_ref[...] + pp * _sigmoid(z)

    o_spec = BS((tm, tn), lambda i, j: (i, j))
    out = SDS((t, d), F32)
    return pl.pallas_call(body, out_shape=(out, out, out), grid=(t // tm, N_DEV),
                          in_specs=[BS((tm, d), lambda i, j: (i, 0)), BS((d, tn), lambda i, j: (0, j)),
                                    BS((tm, dp), lambda i, j: (i, 0)), BS((1, dp, tn), lambda i, j: (j, 0, 0)), o_spec],
                          out_specs=(o_spec, o_spec, o_spec), name="ple_fwd",
                          compiler_params=_params("parallel", "parallel"))(xn, wpg, p_bf, wpp, h)


def _ple_bwd_gate(dh, z, pp):
    t, d = dh.shape
    tm = _tile(t, 256)

    def body(dh_ref, z_ref, pp_ref, dpp_ref, dz_ref):
        g = _sigmoid(z_ref[...])
        dh_v = dh_ref[...]
        dpp_ref[...] = (dh_v * g).astype(BF16)
        dz_ref[...] = (dh_v * pp_ref[...] * g * (1.0 - g)).astype(BF16)

    row = BS((tm, d), lambda i: (i, 0))
    return pl.pallas_call(body, out_shape=(SDS((t, d), BF16), SDS((t, d), BF16)), grid=(t // tm,), in_specs=[row, row, row],
                          out_specs=(row, row), name="ple_bwd_gate", compiler_params=_params("parallel"))(dh, z, pp)


def _ple_dwpp(p_bf, dpp):
    t, dp = p_bf.shape
    tn = dpp.shape[1] // N_DEV

    def body(p_ref, g_ref, o_ref):
        o_ref[0] = _dot(p_ref[...], g_ref[...], "tn").astype(BF16)

    return pl.pallas_call(body, out_shape=SDS((N_DEV, dp, tn), BF16), grid=(N_DEV,),
                          in_specs=[BS((t, dp), lambda j: (0, 0)), BS((t, tn), lambda j: (0, j))],
                          out_specs=BS((1, dp, tn), lambda j: (j, 0, 0)), name="ple_dwpp",
                          compiler_params=_params("parallel"))(p_bf, dpp)


def _loss_call(y, target):
    t, d = y.shape
    tm = _tile(t, 256)

    def body(y_ref, t_ref, dy_ref, loss_ref):
        diff = y_ref[...] - t_ref[...]
        dy_ref[...] = diff * (1.0 / d)

        @pl.when(pl.program_id(0) == 0)
        def _():
            loss_ref[...] = jnp.zeros_like(loss_ref)

        loss_ref[...] += 0.5 * jnp.sum(jnp.mean(diff * diff, axis=-1, keepdims=True), axis=0, keepdims=True)

    row = BS((tm, d), lambda i: (i, 0))
    return pl.pallas_call(body, out_shape=(SDS((t, d), F32), SDS((1, 1), F32)), grid=(t // tm,), in_specs=[row, row],
                          out_specs=(row, BS((1, 1), lambda i: (0, 0))), name="loss",
                          compiler_params=_params("arbitrary"))(y, target)


def _mesh_pos():
    return lax.axis_index("x"), lax.axis_index("y"), lax.axis_index("c")


def _dev_index(px, py, pc):
    return 4 * px + 2 * py + pc


def _allgather(name, shards):
    n = len(shards)
    shapes = [(a.shape[1:] if layer is not None else a.shape) for a, layer in shards]

    def body(*refs):
        ins, outs = refs[:n], refs[n:2 * n]
        send_sems, recv_sems, local_sems = refs[2 * n:]
        x, y, c = _mesh_pos()
        me, sibling = (x, y, c), (x, y, 1 - c)
        chips = [(1 - x, y), (x, 1 - y), (1 - x, 1 - y)]

        def src_of(a):
            layer = shards[a][1]
            return ins[a] if layer is None else ins[a].at[layer]

        def slot(a, pos):
            return outs[a].at[_dev_index(*pos)]

        def copy(a, k, block, to, src=None):
            return pltpu.make_async_remote_copy(
                src_ref=slot(a, block) if src is None else src, dst_ref=slot(a, block), send_sem=send_sems.at[a, k],
                recv_sem=recv_sems.at[a, k], device_id=to, device_id_type=pl.DeviceIdType.MESH)

        mine = [pltpu.make_async_copy(src_of(a), slot(a, me), local_sems.at[a]) for a in range(n)]
        first, passed = [], []
        for a in range(n):
            mine[a].start()
            first.append(copy(a, 0, me, sibling, src=src_of(a)))
            first += [copy(a, 1 + j, me, (*chip, c), src=src_of(a)) for j, chip in enumerate(chips)]
        for cp in first:
            cp.start()
        for j, chip in enumerate(chips):
            for a in range(n):
                copy(a, 1 + j, (*chip, c), me).wait_recv()
                fwd = copy(a, 4 + j, (*chip, c), sibling)
                fwd.start()
                passed.append(fwd)
        for a in range(n):
            copy(a, 0, sibling, me).wait_recv()
            for j, chip in enumerate(chips):
                copy(a, 4 + j, (*chip, 1 - c), me).wait_recv()
        for cp in first + passed:
            cp.wait_send()
        for cp in mine:
            cp.wait()

    out_shape = tuple(SDS((N_DEV, *shp), a.dtype) for shp, (a, _) in zip(shapes, shards))
    outs = pl.pallas_call(
        body, out_shape=out_shape, in_specs=[ANY] * n, out_specs=tuple([ANY] * n),
        scratch_shapes=[pltpu.SemaphoreType.DMA((n, 7)), pltpu.SemaphoreType.DMA((n, 7)), pltpu.SemaphoreType.DMA((n,))],
        name=name, compiler_params=pltpu.CompilerParams(has_side_effects=True))(*[a for a, _ in shards])
    return list(outs)


def _exchange_partials(name, grads):
    n = len(grads)
    relations = [(fx, fy, fc) for fx in (0, 1) for fy in (0, 1) for fc in (0, 1)][1:]

    def body(*refs):
        ins, outs = refs[:n], refs[n:2 * n]
        send_sems, recv_sems, local_sems = refs[2 * n:]
        x, y, c = _mesh_pos()
        me = _dev_index(x, y, c)
        peers = [(1 - x if fx else x, 1 - y if fy else y, 1 - c if fc else c) for fx, fy, fc in relations]

        def copy(a, k):
            peer = peers[k]
            return pltpu.make_async_remote_copy(
                src_ref=ins[a].at[_dev_index(*peer)], dst_ref=outs[a].at[me], send_sem=send_sems.at[a, k],
                recv_sem=recv_sems.at[a, k], device_id=peer, device_id_type=pl.DeviceIdType.MESH)

        def arrival(a, k):
            slot = outs[a].at[_dev_index(*peers[k])]
            return pltpu.make_async_remote_copy(
                src_ref=slot, dst_ref=slot, send_sem=send_sems.at[a, k], recv_sem=recv_sems.at[a, k],
                device_id=peers[k], device_id_type=pl.DeviceIdType.MESH)

        mine = [pltpu.make_async_copy(ins[a].at[me], outs[a].at[me], local_sems.at[a]) for a in range(n)]
        sends = [copy(a, k) for k in range(7) for a in range(n)]
        for cp in mine + sends:
            cp.start()
        for a in range(n):
            for k in range(7):
                arrival(a, k).wait_recv()
        for cp in sends:
            cp.wait_send()
        for cp in mine:
            cp.wait()

    out_shape = tuple(SDS(g.shape, g.dtype) for g in grads)
    outs = pl.pallas_call(
        body, out_shape=out_shape, in_specs=[ANY] * n, out_specs=tuple([ANY] * n),
        scratch_shapes=[pltpu.SemaphoreType.DMA((n, 7)), pltpu.SemaphoreType.DMA((n, 7)), pltpu.SemaphoreType.DMA((n,))],
        name=name, compiler_params=pltpu.CompilerParams(has_side_effects=True))(*grads)
    return list(outs)


def _adamw_math(w, g, m, v):
    m = ADAM_B1 * m + (1.0 - ADAM_B1) * g
    v = ADAM_B2 * v + (1.0 - ADAM_B2) * (g * g)
    m_hat = m / (1.0 - ADAM_B1 ** ADAM_STEP)
    v_hat = v / (1.0 - ADAM_B2 ** ADAM_STEP)
    delta = -ADAM_LR * (m_hat / (jnp.sqrt(v_hat) + ADAM_EPS) + ADAM_WD * w)
    return delta, m, v


def _adamw_layer(name, parts, w, m, v, layer, prev):
    depth, r, c = w.shape
    tr = r
    for cand in (256, 128, 64, 32, 16, 8):
        if r % cand == 0:
            tr = cand
            break

    def body(parts_ref, w_ref, m_ref, v_ref, *rest):
        g_ref, d_ref, nm_ref, nv_ref = rest[-4:]
        g = parts_ref[0].astype(F32)
        for k in range(1, N_DEV):
            g = g + parts_ref[k].astype(F32)
        delta, nm, nv = _adamw_math(w_ref[0], g, m_ref[0], v_ref[0])
        g_ref[0] = g
        d_ref[0] = delta
        nm_ref[0] = nm
        nv_ref[0] = nv

    lay = BS((1, tr, c), lambda i: (layer, i, 0))
    stacked = SDS((depth, r, c), F32)
    ins = [parts, w, m, v]
    specs = [BS((N_DEV, tr, c), lambda i: (0, i, 0)), lay, lay, lay]
    aliases = {}
    if prev is not None:
        ins += list(prev)
        specs += [ANY] * 4
        aliases = {4 + q: q for q in range(4)}
    return pl.pallas_call(body, out_shape=(stacked,) * 4, grid=(r // tr,), in_specs=specs, out_specs=(lay,) * 4,
                          input_output_aliases=aliases, name=name, compiler_params=_params("parallel"))(*ins)


def _adamw_small(parts, w, m, v):
    r, c = w.shape
    tr = _tile(r, 256)

    def body(parts_ref, w_ref, m_ref, v_ref, g_ref, d_ref, nm_ref, nv_ref):
        g = parts_ref[0]
        for k in range(1, N_DEV):
            g = g + parts_ref[k]
        delta, nm, nv = _adamw_math(w_ref[...], g, m_ref[...], v_ref[...])
        g_ref[...] = g
        d_ref[...] = delta
        nm_ref[...] = nm
        nv_ref[...] = nv

    row = BS((tr, c), lambda i: (i, 0))
    out = SDS((r, c), F32)
    return pl.pallas_call(body, out_shape=(out,) * 4, grid=(r // tr,),
                          in_specs=[BS((N_DEV, tr, c), lambda i: (0, i, 0)), row, row, row], out_specs=(row,) * 4,
                          name="adamw_small", compiler_params=_params("parallel"))(parts, w, m, v)


def _pad_w_in(gathered):
    _, d, _ = gathered.shape
    w = jnp.transpose(gathered, (1, 0, 2)).reshape(d, PROJ_RAW)
    real_f = OFF_F + F_COLS
    return jnp.concatenate([w[:, :real_f], jnp.zeros((d, OFF_GU - real_f), w.dtype), w[:, real_f:]], axis=1)


def _unpad_dw_in(dw):
    d = dw.shape[0]
    real_f = OFF_F + F_COLS
    w = jnp.concatenate([dw[:, :real_f], dw[:, OFF_GU:]], axis=1)
    return jnp.transpose(w.reshape(d, N_DEV, PROJ_RAW // N_DEV), (1, 0, 2))


def _pack_small(tree):
    flat = jnp.concatenate([tree[n].reshape(-1) for n in SMALL])
    rows = -(-flat.shape[0] // (256 * HEAD)) * 256
    return jnp.pad(flat, (0, rows * HEAD - flat.shape[0])).reshape(rows, HEAD)


def _unpack_small(packed, like):
    flat = packed.reshape(-1)
    out, off = {}, 0
    for n in SMALL:
        size = like[n].size
        out[n] = flat[off:off + size].reshape(like[n].shape)
        off += size
    return out


def _layer_fwd(h0, p_bf, sw, gw):
    t, d = h0.shape
    xn1 = _rms_fwd("rms_fwd", h0, sw["norm_mix"])
    proj = _matmul("proj_fwd", xn1, gw["w_in"], "nn", F32, t, 512)
    c = _fgate_fwd(proj, sw["forget_bias"])
    cq = c[:, :N_HEADS].T.reshape(N_HEADS, t, 1)
    ck = cq.reshape(N_HEADS, 1, t)
    y_attn, lse = _attn_fwd(proj, cq, ck, sw["q_norm"], sw["k_norm"])
    y_gmlp = _gmlp_fwd(proj, sw["gmlp_v_norm"], sw["gmlp_w_s"], sw["gmlp_b_s"])
    y_pool = _pool_fwd(proj, sw["pool_w"], sw["pool_scale"])
    mix = jnp.concatenate([y_attn, y_gmlp, y_pool], axis=1)
    h1 = _matmul("out_fwd", mix, gw["w_out"], "nn", F32, t, 512, res=h0)
    xn2 = _rms_fwd("rms_fwd", h1, sw["norm_ffn"])
    a, b, hh = _ffn_fwd(xn2, gw["w_ffn_gate"], gw["w_ffn_up"])
    h2 = _ffn_down(hh, gw["w_ffn_down"], h1)
    xn3 = _rms_fwd("rms_fwd", h2, sw["norm_ple"])
    h3, z, pp = _ple_fwd(xn3, gw["w_ple_gate"], p_bf, gw["w_ple_proj"], h2)
    saved = dict(h0=h0, xn1=xn1, proj=proj, cq=cq, ck=ck, lse=lse, mix=mix, h1=h1, xn2=xn2, a=a, b=b, hh=hh, h2=h2,
                 xn3=xn3, z=z, pp=pp)
    return h3, saved


def _layer_bwd(dh3, p_bf, sw, gw, s):
    t, d = dh3.shape
    big, small = {}, {}
    dpp, dz = _ple_bwd_gate(dh3, s["z"], s["pp"])
    big["w_ple_proj"] = _ple_dwpp(p_bf, dpp)
    big["w_ple_gate"] = _matmul("dw_tn", s["xn3"], dz, "tn", BF16, d, 512).reshape(N_DEV, d // N_DEV, d)
    dxn3 = _matmul("dx_nt", dz, gw["w_ple_gate"], "nt", F32, t, 512)
    dh2, dh2_bf, small["norm_ple"] = _rms_bwd_call("rms_bwd", dxn3, s["h2"], sw["norm_ple"], dh3)
    da, db = _ffn_bwd_hidden(dh2_bf, gw["w_ffn_down"], s["a"], s["b"])
    big["w_ffn_down"] = _ffn_dwd(s["hh"], dh2_bf)
    big["w_ffn_gate"], big["w_ffn_up"] = _ffn_dwgu(s["xn2"], da, db)
    dxn2 = _ffn_dxn(da, db, gw["w_ffn_gate"], gw["w_ffn_up"])
    dh1, dh1_bf, small["norm_ffn"] = _rms_bwd_call("rms_bwd", dxn2, s["h1"], sw["norm_ffn"], dh2)
    dmix = _matmul("dx_nt", dh1_bf, gw["w_out"], "nt", F32, t, 512)
    big["w_out"] = _matmul("dw_tn", s["mix"], dh1_bf, "tn", BF16, d, 512).reshape(N_DEV, d // N_DEV, d)
    proj = s["proj"]
    dxp, small["pool_w"], small["pool_scale"] = _pool_bwd(dmix, proj, sw["pool_w"], sw["pool_scale"])
    dgu, dgv, small["gmlp_v_norm"], small["gmlp_w_s"], small["gmlp_b_s"] = _gmlp_bwd(
        dmix, proj, sw["gmlp_v_norm"], sw["gmlp_w_s"], sw["gmlp_b_s"])
    dq, dk, dv, dcq, dck, small["q_norm"], small["k_norm"] = _attn_bwd(
        proj, s["cq"], s["ck"], s["lse"], dmix, sw["q_norm"], sw["k_norm"])
    dc = (dcq.reshape(N_HEADS, t) + dck.reshape(N_HEADS, t)).T
    dc = jnp.pad(dc, ((0, 0), (0, HEAD - N_HEADS)))
    df, small["forget_bias"] = _fgate_bwd(dc, proj, sw["forget_bias"])
    dproj = jnp.concatenate([dq, dk, dv, df, dgu, dgv, dxp], axis=1)
    big["w_in"] = _unpad_dw_in(_matmul("dw_in_tn", s["xn1"], dproj, "tn", BF16, d, 512))
    dxn1 = _matmul("dx_in_nt", dproj, gw["w_in"], "nt", F32, _tile(t, 512), 512)
    dh0, _, small["norm_mix"] = _rms_bwd_call("rms_bwd", dxn1, s["h0"], sw["norm_mix"], dh1)
    return dh0, big, small


def _small_kernel_shapes(sm, i):
    row = lambda a: a[i].reshape(1, -1)
    return dict(
        norm_mix=row(sm["norm_mix"]), norm_ffn=row(sm["norm_ffn"]), norm_ple=row(sm["norm_ple"]),
        q_norm=row(sm["q_norm"]), k_norm=row(sm["k_norm"]),
        forget_bias=jnp.pad(row(sm["forget_bias"]), ((0, 0), (0, HEAD - F_COLS))),
        gmlp_v_norm=row(sm["gmlp_v_norm"]), gmlp_w_s=sm["gmlp_w_s"][i], gmlp_b_s=sm["gmlp_b_s"][i].reshape(N_GROUPS, HEAD, 1),
        pool_w=sm["pool_w"][i], pool_scale=row(sm["pool_scale"]))


def _small_grad_shapes(g, like):
    out = {}
    for n in SMALL:
        v = g[n]
        if n == "forget_bias":
            v = v[:, :F_COLS]
        out[n] = v.reshape(like[n].shape[1:])
    return out


def kernel(x, p, norm_mix, w_in, q_norm, k_norm, forget_bias, gmlp_v_norm, gmlp_w_s, gmlp_b_s, pool_w, pool_scale, w_out, norm_ffn, w_ffn_gate, w_ffn_up, w_ffn_down, norm_ple, w_ple_gate, w_ple_proj, loss_target, m_norm_mix, m_w_in, m_q_norm, m_k_norm, m_forget_bias, m_gmlp_v_norm, m_gmlp_w_s, m_gmlp_b_s, m_pool_w, m_pool_scale, m_w_out, m_norm_ffn, m_w_ffn_gate, m_w_ffn_up, m_w_ffn_down, m_norm_ple, m_w_ple_gate, m_w_ple_proj, v_norm_mix, v_w_in, v_q_norm, v_k_norm, v_forget_bias, v_gmlp_v_norm, v_gmlp_w_s, v_gmlp_b_s, v_pool_w, v_pool_scale, v_w_out, v_norm_ffn, v_w_ffn_gate, v_w_ffn_up, v_w_ffn_down, v_norm_ple, v_w_ple_gate, v_w_ple_proj):
    w = dict(norm_mix=norm_mix, w_in=w_in, q_norm=q_norm, k_norm=k_norm, forget_bias=forget_bias, gmlp_v_norm=gmlp_v_norm,
             gmlp_w_s=gmlp_w_s, gmlp_b_s=gmlp_b_s, pool_w=pool_w, pool_scale=pool_scale, w_out=w_out, norm_ffn=norm_ffn,
             w_ffn_gate=w_ffn_gate, w_ffn_up=w_ffn_up, w_ffn_down=w_ffn_down, norm_ple=norm_ple, w_ple_gate=w_ple_gate,
             w_ple_proj=w_ple_proj)
    m = dict(norm_mix=m_norm_mix, w_in=m_w_in, q_norm=m_q_norm, k_norm=m_k_norm, forget_bias=m_forget_bias,
             gmlp_v_norm=m_gmlp_v_norm, gmlp_w_s=m_gmlp_w_s, gmlp_b_s=m_gmlp_b_s, pool_w=m_pool_w, pool_scale=m_pool_scale,
             w_out=m_w_out, norm_ffn=m_norm_ffn, w_ffn_gate=m_w_ffn_gate, w_ffn_up=m_w_ffn_up, w_ffn_down=m_w_ffn_down,
             norm_ple=m_norm_ple, w_ple_gate=m_w_ple_gate, w_ple_proj=m_w_ple_proj)
    v = dict(norm_mix=v_norm_mix, w_in=v_w_in, q_norm=v_q_norm, k_norm=v_k_norm, forget_bias=v_forget_bias,
             gmlp_v_norm=v_gmlp_v_norm, gmlp_w_s=v_gmlp_w_s, gmlp_b_s=v_gmlp_b_s, pool_w=v_pool_w, pool_scale=v_pool_scale,
             w_out=v_w_out, norm_ffn=v_norm_ffn, w_ffn_gate=v_w_ffn_gate, w_ffn_up=v_w_ffn_up, w_ffn_down=v_w_ffn_down,
             norm_ple=v_norm_ple, w_ple_gate=v_w_ple_gate, w_ple_proj=v_w_ple_proj)
    depth = w_in.shape[0]
    t, d = x.shape[1], x.shape[2]
    h = x[0]
    p_bf = p[:, 0].astype(BF16)
    w_bf = {n: w[n].astype(BF16) for n in BIG}

    gathered, saved = [], []
    for i in range(depth):
        got = _allgather(f"allgather_l{i}", [(w_bf[n], i) for n in BIG])
        gw = dict(zip(BIG, got))
        gw["w_in"] = _pad_w_in(gw["w_in"])
        gw["w_out"] = gw["w_out"].reshape(d, d)
        gw["w_ple_gate"] = gw["w_ple_gate"].reshape(d, d)
        sw = _small_kernel_shapes(w, i)
        h, s = _layer_fwd(h, p_bf[i], sw, gw)
        gathered.append((sw, gw))
        saved.append(s)

    dh, loss_part = _loss_call(h, loss_target[0])
    loss = lax.psum(loss_part[0, 0], ("x", "y", "c"))

    small_grads = [None] * depth
    stacked = {n: None for n in BIG}
    for i in reversed(range(depth)):
        sw, gw = gathered[i]
        dh, big, small = _layer_bwd(dh, p_bf[i], sw, gw, saved[i])
        small_grads[i] = _small_grad_shapes(small, w)
        parts = dict(zip(BIG, _exchange_partials(f"exchange_l{i}", [big[n] for n in BIG])))
        for n in BIG:
            stacked[n] = _adamw_layer(f"adamw_{n}", parts[n], w[n], m[n], v[n], i, stacked[n])

    g_small = {n: jnp.stack([small_grads[i][n] for i in range(depth)]) for n in SMALL}
    parts = _allgather("allgather_small", [(_pack_small(g_small), None)])[0]
    packed = _adamw_small(parts, _pack_small(w), _pack_small(m), _pack_small(v))
    small_out = [_unpack_small(q, w) for q in packed]

    results = []
    for q in range(4):
        results.append({**{n: stacked[n][q] for n in BIG}, **small_out[q]})
    outs = [loss, dh[None]]
    for q in range(4):
        outs += [results[q][n] for n in WEIGHTS]
    return tuple(outs)
```

```python
import functools

import jax
import jax.numpy as jnp
from jax import lax
from jax.experimental import pallas as pl
from jax.experimental.pallas import tpu as pltpu

F32 = jnp.float32
BF16 = jnp.bfloat16
EPS = 1e-6
HEAD = 128
N_HEADS = 8
N_GROUPS = 4
POOL_WINDOWS = (2, 4, 8, 16)
N_DEV = 8
ATTN_Q_BLOCK = 256

OFF_Q, OFF_K, OFF_V, OFF_F, OFF_GU, OFF_GV, OFF_XP, PROJ_PAD = 0, 1024, 2048, 3072, 3584, 4096, 4608, 5120
F_COLS = 8
PROJ_RAW = 4616

ADAM_LR, ADAM_B1, ADAM_B2, ADAM_EPS, ADAM_WD, ADAM_STEP = 0.001, 0.9, 0.999, 1e-08, 0.01, 10

BIG = ("w_in", "w_out", "w_ffn_gate", "w_ffn_up", "w_ffn_down", "w_ple_gate", "w_ple_proj")
SMALL = ("norm_mix", "q_norm", "k_norm", "forget_bias", "gmlp_v_norm", "gmlp_w_s", "gmlp_b_s", "pool_w",
         "pool_scale", "norm_ffn", "norm_ple")
WEIGHTS = ("norm_mix", "w_in", "q_norm", "k_norm", "forget_bias", "gmlp_v_norm", "gmlp_w_s", "gmlp_b_s", "pool_w",
           "pool_scale", "w_out", "norm_ffn", "w_ffn_gate", "w_ffn_up", "w_ffn_down", "norm_ple", "w_ple_gate",
           "w_ple_proj")

VMEM_LIMIT = 56 * 1024 * 1024

BS = pl.BlockSpec
SDS = jax.ShapeDtypeStruct
ANY = pl.BlockSpec(memory_space=pl.ANY)


def _params(*sem):
    return pltpu.CompilerParams(dimension_semantics=sem, vmem_limit_bytes=VMEM_LIMIT)


def _dot(a, b, mode="nn", precision=None):
    ca, cb = {"nn": (1, 0), "nt": (1, 1), "tn": (0, 0)}[mode]
    return lax.dot_general(a, b, (((ca,), (cb,)), ((), ())), preferred_element_type=F32, precision=precision)


def _rms_scale(x):
    return lax.rsqrt(jnp.mean(x * x, axis=-1, keepdims=True) + EPS)


def _rms_bwd(g, xhat, r):
    return r * (g - xhat * jnp.mean(g * xhat, axis=-1, keepdims=True))


def _gelu(x, with_grad=False):
    k = 0.7978845608028654
    inner = k * (x + 0.044715 * x * x * x)
    t = jnp.tanh(inner)
    y = 0.5 * x * (1.0 + t)
    if not with_grad:
        return y
    dy = 0.5 * (1.0 + t) + 0.5 * x * (1.0 - t * t) * k * (1.0 + 3.0 * 0.044715 * x * x)
    return y, dy


def _sigmoid(x):
    return 1.0 / (1.0 + jnp.exp(-x))


def _tile(n, want):
    t = min(n, want)
    assert n % t == 0, (n, want)
    return t


def _matmul(name, a, b, mode, out_dtype, tm, tn, res=None):
    if mode == "nn":
        (m, k), n = a.shape, b.shape[1]
        a_spec, b_spec = BS((tm, k), lambda i, j: (i, 0)), BS((k, tn), lambda i, j: (0, j))
    elif mode == "nt":
        (m, k), n = a.shape, b.shape[0]
        a_spec, b_spec = BS((tm, k), lambda i, j: (i, 0)), BS((tn, k), lambda i, j: (j, 0))
    else:
        (k, m), n = a.shape, b.shape[1]
        a_spec, b_spec = BS((k, tm), lambda i, j: (0, i)), BS((k, tn), lambda i, j: (0, j))
    assert m % tm == 0 and n % tn == 0
    o_spec = BS((tm, tn), lambda i, j: (i, j))

    def body(a_ref, b_ref, *rest):
        o_ref = rest[-1]
        acc = _dot(a_ref[...], b_ref[...], mode)
        if res is not None:
            acc = acc + rest[0][...]
        o_ref[...] = acc.astype(out_dtype)

    ins, specs = [a, b], [a_spec, b_spec]
    if res is not None:
        ins.append(res)
        specs.append(o_spec)
    return pl.pallas_call(body, out_shape=SDS((m, n), out_dtype), grid=(m // tm, n // tn), in_specs=specs,
                          out_specs=o_spec, name=name, compiler_params=_params("parallel", "parallel"))(*ins)


def _rms_fwd(name, h, gain):
    t, d = h.shape
    tm = _tile(t, 256)

    def body(h_ref, g_ref, o_ref):
        x = h_ref[...]
        o_ref[...] = (x * _rms_scale(x) * g_ref[...]).astype(BF16)

    return pl.pallas_call(body, out_shape=SDS((t, d), BF16), grid=(t // tm,),
                          in_specs=[BS((tm, d), lambda i: (i, 0)), BS((1, d), lambda i: (0, 0))],
                          out_specs=BS((tm, d), lambda i: (i, 0)), name=name, compiler_params=_params("parallel"))(h, gain)


def _rms_bwd_call(name, dxn, h, gain, dres):
    t, d = h.shape
    tm = _tile(t, 256)

    def body(dxn_ref, h_ref, g_ref, dres_ref, dh_ref, dhb_ref, dg_ref):
        x = h_ref[...]
        r = _rms_scale(x)
        xhat = x * r
        dy = dxn_ref[...]
        dh = dres_ref[...] + _rms_bwd(dy * g_ref[...], xhat, r)
        dh_ref[...] = dh
        dhb_ref[...] = dh.astype(BF16)

        @pl.when(pl.program_id(0) == 0)
        def _():
            dg_ref[...] = jnp.zeros_like(dg_ref)

        dg_ref[...] += jnp.sum(dy * xhat, axis=0, keepdims=True)

    row = BS((tm, d), lambda i: (i, 0))
    vec = BS((1, d), lambda i: (0, 0))
    return pl.pallas_call(body, out_shape=(SDS((t, d), F32), SDS((t, d), BF16), SDS((1, d), F32)), grid=(t // tm,),
                          in_specs=[row, row, vec, row], out_specs=(row, row, vec), name=name,
                          compiler_params=_params("arbitrary"))(dxn, h, gain, dres)


def _fgate_fwd(proj, bias):
    t = proj.shape[0]
    nb = t // HEAD

    def body(f_ref, b_ref, c_ref):
        tri = (lax.broadcasted_iota(jnp.int32, (HEAD, HEAD), 0) >= lax.broadcasted_iota(jnp.int32, (HEAD, HEAD), 1)).astype(F32)
        carry = jnp.zeros((1, HEAD), F32)
        for n in range(nb):
            rows = slice(n * HEAD, (n + 1) * HEAD)
            x = f_ref[rows, :] + b_ref[...]
            lf = jnp.minimum(x, 0.0) - jnp.log(1.0 + jnp.exp(-jnp.abs(x)))
            cb = _dot(tri, lf, precision=lax.Precision.HIGHEST) + carry
            c_ref[rows, :] = cb
            carry = cb[HEAD - 1:HEAD, :]

    return pl.pallas_call(body, out_shape=SDS((t, HEAD), F32), grid=(1,),
                          in_specs=[BS((t, HEAD), lambda i: (0, OFF_F // HEAD)), BS((1, HEAD), lambda i: (0, 0))],
                          out_specs=BS((t, HEAD), lambda i: (0, 0)), name="fgate_fwd",
                          compiler_params=_params("arbitrary"))(proj, bias)


def _fgate_bwd(dc, proj, bias):
    t = proj.shape[0]
    nb = t // HEAD
    width = OFF_GU - OFF_F

    def body(dc_ref, f_ref, b_ref, df_ref, db_ref):
        tri = (lax.broadcasted_iota(jnp.int32, (HEAD, HEAD), 0) <= lax.broadcasted_iota(jnp.int32, (HEAD, HEAD), 1)).astype(F32)
        carry = jnp.zeros((1, HEAD), F32)
        db = jnp.zeros((1, HEAD), F32)
        df_ref[:, HEAD:] = jnp.zeros((t, width - HEAD), BF16)
        for n in reversed(range(nb)):
            rows = slice(n * HEAD, (n + 1) * HEAD)
            dlf = _dot(tri, dc_ref[rows, :], precision=lax.Precision.HIGHEST) + carry
            carry = dlf[0:1, :]
            x = f_ref[rows, :] + b_ref[...]
            df = dlf * _sigmoid(-x)
            df_ref[rows, 0:HEAD] = df.astype(BF16)
            db = db + jnp.sum(df, axis=0, keepdims=True)
        db_ref[...] = db

    return pl.pallas_call(body, out_shape=(SDS((t, width), BF16), SDS((1, HEAD), F32)), grid=(1,),
                          in_specs=[BS((t, HEAD), lambda i: (0, 0)), BS((t, HEAD), lambda i: (0, OFF_F // HEAD)),
                                    BS((1, HEAD), lambda i: (0, 0))],
                          out_specs=(BS((t, width), lambda i: (0, 0)), BS((1, HEAD), lambda i: (0, 0))),
                          name="fgate_bwd", compiler_params=_params("arbitrary"))(dc, proj, bias)


def _attn_masked_logits(qs_ref, kn_ref, cq_ref, ck_ref, i, tq):
    lo, hi = i * tq, (i + 1) * tq
    s = _dot(qs_ref[lo:hi, :], kn_ref[0:hi, :], "nt")
    s = s + cq_ref[0, lo:hi, :] - ck_ref[0, :, 0:hi]
    row = lax.broadcasted_iota(jnp.int32, (tq, hi), 0) + lo
    col = lax.broadcasted_iota(jnp.int32, (tq, hi), 1)
    return s, row >= col


def _attn_fwd(proj, cq, ck, qg, kg):
    t = proj.shape[0]
    tq = _tile(t, ATTN_Q_BLOCK)
    nq = t // tq
    scale = HEAD ** -0.5

    def body(q_ref, k_ref, v_ref, cq_ref, ck_ref, qg_ref, kg_ref, o_ref, lse_ref, qs_ref, kn_ref, vb_ref):
        q = q_ref[...]
        k = k_ref[...]
        qs_ref[...] = (q * _rms_scale(q) * qg_ref[...] * scale).astype(BF16)
        kn_ref[...] = (k * _rms_scale(k) * kg_ref[...]).astype(BF16)
        vb_ref[...] = v_ref[...].astype(BF16)
        for i in range(nq):
            lo, hi = i * tq, (i + 1) * tq
            s, keep = _attn_masked_logits(qs_ref, kn_ref, cq_ref, ck_ref, i, tq)
            s = jnp.where(keep, s, -1e30)
            m = jnp.max(s, axis=-1, keepdims=True)
            e = jnp.exp(s - m)
            l = jnp.sum(e, axis=-1, keepdims=True)
            o = _dot(e.astype(BF16), vb_ref[0:hi, :]) / l
            o_ref[lo:hi, :] = o.astype(BF16)
            lse_ref[0, lo:hi, :] = m + jnp.log(l)

    def col(off):
        return BS((t, HEAD), lambda h: (0, off // HEAD + h))

    vec = BS((1, HEAD), lambda h: (0, 0))
    return pl.pallas_call(
        body, out_shape=(SDS((t, N_HEADS * HEAD), BF16), SDS((N_HEADS, t, 1), F32)), grid=(N_HEADS,),
        in_specs=[col(OFF_Q), col(OFF_K), col(OFF_V), BS((1, t, 1), lambda h: (h, 0, 0)), BS((1, 1, t), lambda h: (h, 0, 0)),
                  vec, vec],
        out_specs=(BS((t, HEAD), lambda h: (0, h)), BS((1, t, 1), lambda h: (h, 0, 0))),
        scratch_shapes=[pltpu.VMEM((t, HEAD), BF16)] * 3, name="attn_fwd",
        compiler_params=_params("parallel"))(proj, proj, proj, cq, ck, qg, kg)


def _attn_bwd(proj, cq, ck, lse, dmix, qg, kg):
    t = proj.shape[0]
    tq = _tile(t, ATTN_Q_BLOCK)
    nq = t // tq
    scale = HEAD ** -0.5

    def body(q_ref, k_ref, v_ref, cq_ref, ck_ref, lse_ref, do_ref, qg_ref, kg_ref,
             dq_ref, dk_ref, dv_ref, dcq_ref, dck_ref, dqg_ref, dkg_ref,
             qs_ref, kn_ref, vb_ref, dob_ref, dqs_ref, dkn_ref, dva_ref):
        q = q_ref[...]
        k = k_ref[...]
        rq = _rms_scale(q)
        rk = _rms_scale(k)
        qs_ref[...] = (q * rq * qg_ref[...] * scale).astype(BF16)
        kn_ref[...] = (k * rk * kg_ref[...]).astype(BF16)
        vb_ref[...] = v_ref[...].astype(BF16)
        dob_ref[...] = do_ref[...].astype(BF16)
        dkn_ref[...] = jnp.zeros_like(dkn_ref)
        dva_ref[...] = jnp.zeros_like(dva_ref)
        dck_ref[...] = jnp.zeros_like(dck_ref)
        for i in range(nq):
            lo, hi = i * tq, (i + 1) * tq
            s, keep = _attn_masked_logits(qs_ref, kn_ref, cq_ref, ck_ref, i, tq)
            pr = jnp.where(keep, jnp.exp(s - lse_ref[0, lo:hi, :]), 0.0)
            dp = _dot(dob_ref[lo:hi, :], vb_ref[0:hi, :], "nt")
            delta = jnp.sum(pr * dp, axis=-1, keepdims=True)
            ds = pr * (dp - delta)
            dcq_ref[0, lo:hi, :] = jnp.sum(ds, axis=-1, keepdims=True)
            dck_ref[0, :, 0:hi] += -jnp.sum(ds, axis=0, keepdims=True)
            dsb = ds.astype(BF16)
            dqs_ref[lo:hi, :] = _dot(dsb, kn_ref[0:hi, :])
            dkn_ref[0:hi, :] += _dot(dsb, qs_ref[lo:hi, :], "tn")
            dva_ref[0:hi, :] += _dot(pr.astype(BF16), dob_ref[lo:hi, :], "tn")
        dv_ref[...] = dva_ref[...].astype(BF16)

        @pl.when(pl.program_id(0) == 0)
        def _():
            dqg_ref[...] = jnp.zeros_like(dqg_ref)
            dkg_ref[...] = jnp.zeros_like(dkg_ref)

        qhat = q * rq
        dqn = dqs_ref[...] * scale
        dqg_ref[...] += jnp.sum(dqn * qhat, axis=0, keepdims=True)
        dq_ref[...] = _rms_bwd(dqn * qg_ref[...], qhat, rq).astype(BF16)
        khat = k * rk
        dkn = dkn_ref[...]
        dkg_ref[...] += jnp.sum(dkn * khat, axis=0, keepdims=True)
        dk_ref[...] = _rms_bwd(dkn * kg_ref[...], khat, rk).astype(BF16)

    def col(off):
        return BS((t, HEAD), lambda h: (0, off // HEAD + h))

    vec = BS((1, HEAD), lambda h: (0, 0))
    c_col = BS((1, t, 1), lambda h: (h, 0, 0))
    c_row = BS((1, 1, t), lambda h: (h, 0, 0))
    head_out = BS((t, HEAD), lambda h: (0, h))
    width = N_HEADS * HEAD
    return pl.pallas_call(
        body,
        out_shape=(SDS((t, width), BF16), SDS((t, width), BF16), SDS((t, width), BF16), SDS((N_HEADS, t, 1), F32),
                   SDS((N_HEADS, 1, t), F32), SDS((1, HEAD), F32), SDS((1, HEAD), F32)),
        grid=(N_HEADS,),
        in_specs=[col(OFF_Q), col(OFF_K), col(OFF_V), c_col, c_row, c_col, head_out, vec, vec],
        out_specs=(head_out, head_out, head_out, c_col, c_row, vec, vec),
        scratch_shapes=[pltpu.VMEM((t, HEAD), BF16)] * 4 + [pltpu.VMEM((t, HEAD), F32)] * 3, name="attn_bwd",
        compiler_params=_params("arbitrary"))(proj, proj, proj, cq, ck, lse, dmix, qg, kg)


def _group_cols(off):
    width = N_GROUPS * HEAD
    return lambda t: BS((t, width), lambda i: (0, off // width))


def _tril():
    return lax.broadcasted_iota(jnp.int32, (HEAD, HEAD), 0) >= lax.broadcasted_iota(jnp.int32, (HEAD, HEAD), 1)


def _gmlp_fwd(proj, gain, w_s, b_s):
    t = proj.shape[0]
    width = N_GROUPS * HEAD
    nc = t // HEAD

    def body(gu_ref, gv_ref, gain_ref, ws_ref, bs_ref, o_ref):
        tril = _tril()
        for g in range(N_GROUPS):
            cols = slice(g * HEAD, (g + 1) * HEAD)
            u = _gelu(gu_ref[:, cols])
            vv = _gelu(gv_ref[:, cols])
            vn = (vv * _rms_scale(vv) * gain_ref[:, cols]).astype(BF16)
            w = jnp.where(tril, ws_ref[g], 0.0).astype(BF16)
            for n in range(nc):
                rows = slice(n * HEAD, (n + 1) * HEAD)
                mixed = _dot(w, vn[rows]) + bs_ref[g]
                o_ref[rows, cols] = (u[rows] * mixed).astype(BF16)

    full = lambda shape: BS(shape, lambda i: (0,) * len(shape))
    return pl.pallas_call(body, out_shape=SDS((t, width), BF16), grid=(1,),
                          in_specs=[_group_cols(OFF_GU)(t), _group_cols(OFF_GV)(t), full((1, width)),
                                    full((N_GROUPS, HEAD, HEAD)), full((N_GROUPS, HEAD, 1))],
                          out_specs=full((t, width)), name="gmlp_fwd",
                          compiler_params=_params("arbitrary"))(proj, proj, gain, w_s, b_s)


def _gmlp_bwd(dmix, proj, gain, w_s, b_s):
    t = proj.shape[0]
    width = N_GROUPS * HEAD
    nc = t // HEAD

    def body(dy_ref, gu_ref, gv_ref, gain_ref, ws_ref, bs_ref, dgu_ref, dgv_ref, dgain_ref, dws_ref, dbs_ref, dvn_ref):
        tril = _tril()
        for g in range(N_GROUPS):
            cols = slice(g * HEAD, (g + 1) * HEAD)
            u, du = _gelu(gu_ref[:, cols], with_grad=True)
            vv, dvv = _gelu(gv_ref[:, cols], with_grad=True)
            r = _rms_scale(vv)
            vhat = vv * r
            gain_g = gain_ref[:, cols]
            vn = (vhat * gain_g).astype(BF16)
            w = jnp.where(tril, ws_ref[g], 0.0).astype(BF16)
            dws = jnp.zeros((HEAD, HEAD), F32)
            dbs = jnp.zeros((HEAD, 1), F32)
            for n in range(nc):
                rows = slice(n * HEAD, (n + 1) * HEAD)
                mixed = _dot(w, vn[rows]) + bs_ref[g]
                dy = dy_ref[rows, cols]
                dgu_ref[rows, cols] = (dy * mixed * du[rows]).astype(BF16)
                dm = dy * u[rows]
                dmb = dm.astype(BF16)
                dbs = dbs + jnp.sum(dm, axis=-1, keepdims=True)
                dws = dws + _dot(dmb, vn[rows], "nt")
                dvn_ref[rows, :] = _dot(w, dmb, "tn")
            dvn = dvn_ref[...]
            dgain_ref[:, cols] = jnp.sum(dvn * vhat, axis=0, keepdims=True)
            dgv_ref[:, cols] = (_rms_bwd(dvn * gain_g, vhat, r) * dvv).astype(BF16)
            dws_ref[g] = jnp.where(tril, dws, 0.0)
            dbs_ref[g] = dbs

    full = lambda shape: BS(shape, lambda i: (0,) * len(shape))
    return pl.pallas_call(
        body,
        out_shape=(SDS((t, width), BF16), SDS((t, width), BF16), SDS((1, width), F32), SDS((N_GROUPS, HEAD, HEAD), F32),
                   SDS((N_GROUPS, HEAD, 1), F32)),
        grid=(1,),
        in_specs=[BS((t, width), lambda i: (0, 2)), _group_cols(OFF_GU)(t), _group_cols(OFF_GV)(t), full((1, width)),
                  full((N_GROUPS, HEAD, HEAD)), full((N_GROUPS, HEAD, 1))],
        out_specs=(full((t, width)), full((t, width)), full((1, width)), full((N_GROUPS, HEAD, HEAD)),
                   full((N_GROUPS, HEAD, 1))),
        scratch_shapes=[pltpu.VMEM((t, HEAD), F32)], name="gmlp_bwd",
        compiler_params=_params("arbitrary"))(dmix, proj, proj, gain, w_s, b_s)


def _pool_window_mean_minus_x(x, window, t_idx):
    s, span = x, 1
    while span < window:
        s = s + jnp.where(t_idx >= span, pltpu.roll(s, span, 0), 0.0)
        span *= 2
    cnt = jnp.minimum(t_idx + 1, window).astype(F32)
    return s / cnt - x, cnt


def _pool_fwd(proj, w_pool, scale):
    t = proj.shape[0]
    width = N_GROUPS * HEAD

    def body(x_ref, w_ref, sc_ref, o_ref):
        t_idx = lax.broadcasted_iota(jnp.int32, (t, HEAD), 0)
        for g in range(N_GROUPS):
            cols = slice(g * HEAD, (g + 1) * HEAD)
            d, _ = _pool_window_mean_minus_x(x_ref[:, cols], POOL_WINDOWS[g], t_idx)
            y = _dot(d.astype(BF16), w_ref[g].astype(BF16)) * sc_ref[:, cols]
            o_ref[:, cols] = y.astype(BF16)

    full = lambda shape: BS(shape, lambda i: (0,) * len(shape))
    return pl.pallas_call(body, out_shape=SDS((t, width), BF16), grid=(1,),
                          in_specs=[_group_cols(OFF_XP)(t), full((N_GROUPS, HEAD, HEAD)), full((1, width))],
                          out_specs=full((t, width)), name="pool_fwd",
                          compiler_params=_params("arbitrary"))(proj, w_pool, scale)


def _pool_bwd(dmix, proj, w_pool, scale):
    t = proj.shape[0]
    width = N_GROUPS * HEAD

    def body(dy_ref, x_ref, w_ref, sc_ref, dx_ref, dw_ref, dsc_ref):
        t_idx = lax.broadcasted_iota(jnp.int32, (t, HEAD), 0)
        for g in range(N_GROUPS):
            cols = slice(g * HEAD, (g + 1) * HEAD)
            window = POOL_WINDOWS[g]
            d, cnt = _pool_window_mean_minus_x(x_ref[:, cols], window, t_idx)
            db = d.astype(BF16)
            wb = w_ref[g].astype(BF16)
            dy = dy_ref[:, cols]
            dsc_ref[:, cols] = jnp.sum(dy * _dot(db, wb), axis=0, keepdims=True)
            dyw = (dy * sc_ref[:, cols]).astype(BF16)
            dw_ref[g] = _dot(db, dyw, "tn")
            dd = _dot(dyw, wb, "nt")
            rsum, span = dd / cnt, 1
            while span < window:
                rsum = rsum + jnp.where(t_idx < t - span, pltpu.roll(rsum, t - span, 0), 0.0)
                span *= 2
            dx_ref[:, cols] = (rsum - dd).astype(BF16)

    full = lambda shape: BS(shape, lambda i: (0,) * len(shape))
    return pl.pallas_call(
        body, out_shape=(SDS((t, width), BF16), SDS((N_GROUPS, HEAD, HEAD), F32), SDS((1, width), F32)), grid=(1,),
        in_specs=[BS((t, width), lambda i: (0, 3)), _group_cols(OFF_XP)(t), full((N_GROUPS, HEAD, HEAD)), full((1, width))],
        out_specs=(full((t, width)), full((N_GROUPS, HEAD, HEAD)), full((1, width))), name="pool_bwd",
        compiler_params=_params("arbitrary"))(dmix, proj, w_pool, scale)


def _ffn_fwd(xn, wg, wu):
    t, d = xn.shape
    fs = wg.shape[2]
    tm = _tile(t, 512)

    def body(x_ref, wg_ref, wu_ref, a_ref, b_ref, hh_ref):
        x = x_ref[...]
        a = _dot(x, wg_ref[0])
        b = _dot(x, wu_ref[0])
        a_ref[0] = a
        b_ref[0] = b
        hh_ref[0] = (a * _sigmoid(a) * b).astype(BF16)

    w_spec = BS((1, d, fs), lambda j, i: (j, 0, 0))
    o_spec = BS((1, tm, fs), lambda j, i: (j, i, 0))
    return pl.pallas_call(body, out_shape=(SDS((N_DEV, t, fs), F32), SDS((N_DEV, t, fs), F32), SDS((N_DEV, t, fs), BF16)),
                          grid=(N_DEV, t // tm), in_specs=[BS((tm, d), lambda j, i: (i, 0)), w_spec, w_spec],
                          out_specs=(o_spec, o_spec, o_spec), name="ffn_fwd",
                          compiler_params=_params("parallel", "parallel"))(xn, wg, wu)


def _ffn_down(hh, wd, res):
    _, t, fs = hh.shape
    d = wd.shape[2]
    tm, tn = _tile(t, 1024), _tile(d, 1024)

    def body(a_ref, b_ref, r_ref, o_ref, acc_ref):
        k = pl.program_id(2)

        @pl.when(k == 0)
        def _():
            acc_ref[...] = r_ref[...]

        acc_ref[...] += _dot(a_ref[0], b_ref[0])

        @pl.when(k == N_DEV - 1)
        def _():
            o_ref[...] = acc_ref[...]

    o_spec = BS((tm, tn), lambda i, j, k: (i, j))
    return pl.pallas_call(body, out_shape=SDS((t, d), F32), grid=(t // tm, d // tn, N_DEV),
                          in_specs=[BS((1, tm, fs), lambda i, j, k: (k, i, 0)), BS((1, fs, tn), lambda i, j, k: (k, 0, j)), o_spec],
                          out_specs=o_spec, scratch_shapes=[pltpu.VMEM((tm, tn), F32)], name="ffn_down",
                          compiler_params=_params("parallel", "parallel", "arbitrary"))(hh, wd, res)


def _ffn_bwd_hidden(dh_bf, wd, a, b):
    t, d = dh_bf.shape
    fs = wd.shape[1]
    tm = _tile(t, 512)

    def body(dh_ref, wd_ref, a_ref, b_ref, da_ref, db_ref):
        dhh = _dot(dh_ref[...], wd_ref[0], "nt")
        av = a_ref[0]
        sig = _sigmoid(av)
        da_ref[0] = (dhh * b_ref[0] * sig * (1.0 + av * (1.0 - sig))).astype(BF16)
        db_ref[0] = (dhh * av * sig).astype(BF16)

    o_spec = BS((1, tm, fs), lambda j, i: (j, i, 0))
    return pl.pallas_call(body, out_shape=(SDS((N_DEV, t, fs), BF16), SDS((N_DEV, t, fs), BF16)), grid=(N_DEV, t // tm),
                          in_specs=[BS((tm, d), lambda j, i: (i, 0)), BS((1, fs, d), lambda j, i: (j, 0, 0)), o_spec, o_spec],
                          out_specs=(o_spec, o_spec), name="ffn_bwd_hidden",
                          compiler_params=_params("parallel", "parallel"))(dh_bf, wd, a, b)


def _ffn_dwd(hh, dh_bf):
    _, t, fs = hh.shape
    d = dh_bf.shape[1]
    tn = _tile(d, 1024)

    def body(a_ref, b_ref, o_ref):
        o_ref[0] = _dot(a_ref[0], b_ref[...], "tn").astype(BF16)

    return pl.pallas_call(body, out_shape=SDS((N_DEV, fs, d), BF16), grid=(N_DEV, d // tn),
                          in_specs=[BS((1, t, fs), lambda j, n: (j, 0, 0)), BS((t, tn), lambda j, n: (0, n))],
                          out_specs=BS((1, fs, tn), lambda j, n: (j, 0, n)), name="ffn_dwd",
                          compiler_params=_params("parallel", "parallel"))(hh, dh_bf)


def _ffn_dwgu(xn, da, db):
    t, d = xn.shape
    fs = da.shape[2]
    tm = _tile(d, 1024)

    def body(x_ref, da_ref, db_ref, dg_ref, du_ref):
        x = x_ref[...]
        dg_ref[0] = _dot(x, da_ref[0], "tn").astype(BF16)
        du_ref[0] = _dot(x, db_ref[0], "tn").astype(BF16)

    g_spec = BS((1, t, fs), lambda j, i: (j, 0, 0))
    o_spec = BS((1, tm, fs), lambda j, i: (j, i, 0))
    return pl.pallas_call(body, out_shape=(SDS((N_DEV, d, fs), BF16), SDS((N_DEV, d, fs), BF16)), grid=(N_DEV, d // tm),
                          in_specs=[BS((t, tm), lambda j, i: (0, i)), g_spec, g_spec], out_specs=(o_spec, o_spec),
                          name="ffn_dwgu", compiler_params=_params("parallel", "parallel"))(xn, da, db)


def _ffn_dxn(da, db, wg, wu):
    _, t, fs = da.shape
    d = wg.shape[1]
    tm, tn = _tile(t, 1024), _tile(d, 1024)

    def body(da_ref, db_ref, wg_ref, wu_ref, o_ref, acc_ref):
        k = pl.program_id(2)

        @pl.when(k == 0)
        def _():
            acc_ref[...] = jnp.zeros_like(acc_ref)

        acc_ref[...] += _dot(da_ref[0], wg_ref[0], "nt") + _dot(db_ref[0], wu_ref[0], "nt")

        @pl.when(k == N_DEV - 1)
        def _():
            o_ref[...] = acc_ref[...]

    g_spec = BS((1, tm, fs), lambda i, j, k: (k, i, 0))
    w_spec = BS((1, tn, fs), lambda i, j, k: (k, j, 0))
    return pl.pallas_call(body, out_shape=SDS((t, d), F32), grid=(t // tm, d // tn, N_DEV),
                          in_specs=[g_spec, g_spec, w_spec, w_spec], out_specs=BS((tm, tn), lambda i, j, k: (i, j)),
                          scratch_shapes=[pltpu.VMEM((tm, tn), F32)], name="ffn_dxn",
                          compiler_params=_params("parallel", "parallel", "arbitrary"))(da, db, wg, wu)


def _ple_fwd(xn, wpg, p_bf, wpp, h):
    t, d = xn.shape
    dp = p_bf.shape[1]
    tn = wpp.shape[2]
    tm = _tile(t, 1024)

    def body(x_ref, wg_ref, p_ref, wp_ref, h_ref, o_ref, z_ref, pp_ref):
        z = _dot(x_ref[...], wg_ref[...])
        pp = _dot(p_ref[...], wp_ref[0])
        z_ref[...] = z
        pp_ref[...] = pp
        o_ref[...] = h_ref[...] + pp * _sigmoid(z)

    o_spec = BS((tm, tn), lambda i, j: (i, j))
    out = SDS((t, d), F32)
    return pl.pallas_call(body, out_shape=(out, out, out), grid=(t // tm, N_DEV),
                          in_specs=[BS((tm, d), lambda i, j: (i, 0)), BS((d, tn), lambda i, j: (0, j)),
                                    BS((tm, dp), lambda i, j: (i, 0)), BS((1, dp, tn), lambda i, j: (j, 0, 0)), o_spec],
                          out_specs=(o_spec, o_spec, o_spec), name="ple_fwd",
                          compiler_params=_params("parallel", "parallel"))(xn, wpg, p_bf, wpp, h)


def _ple_bwd_gate(dh, z, pp):
    t, d = dh.shape
    tm = _tile(t, 256)

    def body(dh_ref, z_ref, pp_ref, dpp_ref, dz_ref):
        g = _sigmoid(z_ref[...])
        dh_v = dh_ref[...]
        dpp_ref[...] = (dh_v * g).astype(BF16)
        dz_ref[...] = (dh_v * pp_ref[...] * g * (1.0 - g)).astype(BF16)

    row = BS((tm, d), lambda i: (i, 0))
    return pl.pallas_call(body, out_shape=(SDS((t, d), BF16), SDS((t, d), BF16)), grid=(t // tm,), in_specs=[row, row, row],
                          out_specs=(row, row), name="ple_bwd_gate", compiler_params=_params("parallel"))(dh, z, pp)


def _ple_dwpp(p_bf, dpp):
    t, dp = p_bf.shape
    tn = dpp.shape[1] // N_DEV

    def body(p_ref, g_ref, o_ref):
        o_ref[0] = _dot(p_ref[...], g_ref[...], "tn").astype(BF16)

    return pl.pallas_call(body, out_shape=SDS((N_DEV, dp, tn), BF16), grid=(N_DEV,),
                          in_specs=[BS((t, dp), lambda j: (0, 0)), BS((t, tn), lambda j: (0, j))],
                          out_specs=BS((1, dp, tn), lambda j: (j, 0, 0)), name="ple_dwpp",
                          compiler_params=_params("parallel"))(p_bf, dpp)


def _loss_call(y, target):
    t, d = y.shape
    tm = _tile(t, 256)

    def body(y_ref, t_ref, dy_ref, loss_ref):
        diff = y_ref[...] - t_ref[...]
        dy_ref[...] = diff * (1.0 / d)

        @pl.when(pl.program_id(0) == 0)
        def _():
            loss_ref[...] = jnp.zeros_like(loss_ref)

        loss_ref[...] += 0.5 * jnp.sum(jnp.mean(diff * diff, axis=-1, keepdims=True), axis=0, keepdims=True)

    row = BS((tm, d), lambda i: (i, 0))
    return pl.pallas_call(body, out_shape=(SDS((t, d), F32), SDS((1, 1), F32)), grid=(t // tm,), in_specs=[row, row],
                          out_specs=(row, BS((1, 1), lambda i: (0, 0))), name="loss",
                          compiler_params=_params("arbitrary"))(y, target)


def _mesh_pos():
    return lax.axis_index("x"), lax.axis_index("y"), lax.axis_index("c")


def _dev_index(px, py, pc):
    return 4 * px + 2 * py + pc


def _allgather(name, shards):
    n = len(shards)
    shapes = [(a.shape[1:] if layer is not None else a.shape) for a, layer in shards]

    def body(*refs):
        ins, outs = refs[:n], refs[n:2 * n]
        send_sems, recv_sems, local_sems = refs[2 * n:]
        x, y, c = _mesh_pos()
        me, sibling = (x, y, c), (x, y, 1 - c)
        chips = [(1 - x, y), (x, 1 - y), (1 - x, 1 - y)]

        def src_of(a):
            layer = shards[a][1]
            return ins[a] if layer is None else ins[a].at[layer]

        def slot(a, pos):
            return outs[a].at[_dev_index(*pos)]

        def copy(a, k, block, to, src=None):
            return pltpu.make_async_remote_copy(
                src_ref=slot(a, block) if src is None else src, dst_ref=slot(a, block), send_sem=send_sems.at[a, k],
                recv_sem=recv_sems.at[a, k], device_id=to, device_id_type=pl.DeviceIdType.MESH)

        mine = [pltpu.make_async_copy(src_of(a), slot(a, me), local_sems.at[a]) for a in range(n)]
        first, passed = [], []
        for a in range(n):
            mine[a].start()
            first.append(copy(a, 0, me, sibling, src=src_of(a)))
            first += [copy(a, 1 + j, me, (*chip, c), src=src_of(a)) for j, chip in enumerate(chips)]
        for cp in first:
            cp.start()
        for j, chip in enumerate(chips):
            for a in range(n):
                copy(a, 1 + j, (*chip, c), me).wait_recv()
                fwd = copy(a, 4 + j, (*chip, c), sibling)
                fwd.start()
                passed.append(fwd)
        for a in range(n):
            copy(a, 0, sibling, me).wait_recv()
            for j, chip in enumerate(chips):
                copy(a, 4 + j, (*chip, 1 - c), me).wait_recv()
        for cp in first + passed:
            cp.wait_send()
        for cp in mine:
            cp.wait()

    out_shape = tuple(SDS((N_DEV, *shp), a.dtype) for shp, (a, _) in zip(shapes, shards))
    outs = pl.pallas_call(
        body, out_shape=out_shape, in_specs=[ANY] * n, out_specs=tuple([ANY] * n),
        scratch_shapes=[pltpu.SemaphoreType.DMA((n, 7)), pltpu.SemaphoreType.DMA((n, 7)), pltpu.SemaphoreType.DMA((n,))],
        name=name, compiler_params=pltpu.CompilerParams(has_side_effects=True))(*[a for a, _ in shards])
    return list(outs)


HBM = pl.BlockSpec(memory_space=pltpu.HBM)
SEM = pl.BlockSpec(memory_space=pltpu.SEMAPHORE)
DATAFLOW = pltpu.SideEffectType.DATAFLOW_SIDE_EFFECTING


def _in_hbm(arrs):
    return [pltpu.with_memory_space_constraint(a, pltpu.HBM) for a in arrs]


def _other_chips(x, y):
    return [(1 - x, y), (x, 1 - y), (1 - x, 1 - y)]


def _split_start(name, srcs, lands, n_sems, copies):
    n = len(srcs)

    def body(*refs):
        for cp in copies(refs[:n], refs[n:2 * n], refs[2 * n], refs[2 * n + 1]):
            cp.start()
        token = refs[-1]
        token[...] = jnp.zeros_like(token)

    thru = [pltpu.HBM(a.shape, a.dtype) for a in list(srcs) + list(lands)]
    outs = pl.pallas_call(
        body, name=name,
        out_shape=(pltpu.SemaphoreType.DMA((n * n_sems,)), pltpu.SemaphoreType.DMA((n * n_sems,)), *thru, SDS((8, HEAD), F32)),
        in_specs=[HBM] * (2 * n), out_specs=(SEM, SEM, *([HBM] * (2 * n)), pl.BlockSpec(memory_space=pltpu.VMEM)),
        input_output_aliases={q: 2 + q for q in range(2 * n)},
        compiler_params=pltpu.CompilerParams(has_side_effects=DATAFLOW))(*_in_hbm(list(srcs) + list(lands)))
    return outs[0], outs[1], list(outs[2:2 + n]), list(outs[2 + n:2 + 2 * n]), outs[-1]


def _split_wait(name, srcs, lands, send_sems, recv_sems, after, copies):
    n = len(srcs)

    def body(*refs):
        for cp in copies(refs[:n], refs[n:2 * n], refs[2 * n], refs[2 * n + 1]):
            cp.wait_send()
            cp.wait_recv()

    thru = [pltpu.HBM(a.shape, a.dtype) for a in list(srcs) + list(lands)]
    outs = pl.pallas_call(
        body, name=name, out_shape=tuple(thru), in_specs=[HBM] * (2 * n) + [SEM, SEM, ANY], out_specs=tuple([HBM] * (2 * n)),
        input_output_aliases={q: q for q in range(2 * n)},
        compiler_params=pltpu.CompilerParams(has_side_effects=DATAFLOW))(*list(srcs), *list(lands), send_sems, recv_sems, after)
    return list(outs[:n]), list(outs[n:])


def _gather_ici_copies(layer, waiting):
    def copies(src_refs, land_refs, send_sems, recv_sems):
        x, y, c = _mesh_pos()
        out = []
        for a in range(len(src_refs)):
            for j, chip in enumerate(_other_chips(x, y)):
                slot = _dev_index(*chip, c) if waiting else _dev_index(x, y, c)
                out.append(pltpu.make_async_remote_copy(
                    src_ref=src_refs[a].at[layer], dst_ref=land_refs[a].at[slot], send_sem=send_sems.at[3 * a + j],
                    recv_sem=recv_sems.at[3 * a + j], device_id=(*chip, c), device_id_type=pl.DeviceIdType.MESH))
        return out
    return copies


def _gather_d2d(name, srcs, layer, lands):
    n = len(srcs)

    def body(*refs):
        src_refs, land_refs = refs[:n], refs[n:2 * n]
        send_sems, recv_sems, local_sems = refs[3 * n:]
        x, y, c = _mesh_pos()
        sibling = (x, y, 1 - c)
        blocks = [(x, y)] + _other_chips(x, y)

        def copy(a, k, waiting):
            pc = 1 - c if waiting else c
            slot = land_refs[a].at[_dev_index(*blocks[k], pc)]
            src = src_refs[a].at[layer] if (k == 0 and not waiting) else slot
            return pltpu.make_async_remote_copy(src_ref=src, dst_ref=slot, send_sem=send_sems.at[a, k],
                                                recv_sem=recv_sems.at[a, k], device_id=sibling,
                                                device_id_type=pl.DeviceIdType.MESH)

        mine = [pltpu.make_async_copy(src_refs[a].at[layer], land_refs[a].at[_dev_index(x, y, c)], local_sems.at[a])
                for a in range(n)]
        sends = [copy(a, k, False) for a in range(n) for k in range(4)]
        for cp in mine + sends:
            cp.start()
        for a in range(n):
            for k in range(4):
                copy(a, k, True).wait_recv()
        for cp in sends:
            cp.wait_send()
        for cp in mine:
            cp.wait()

    outs = pl.pallas_call(
        body, name=name, out_shape=tuple(SDS(l.shape, l.dtype) for l in lands), in_specs=[ANY] * (2 * n),
        out_specs=tuple([ANY] * n), input_output_aliases={n + q: q for q in range(n)},
        scratch_shapes=[pltpu.SemaphoreType.DMA((n, 4)), pltpu.SemaphoreType.DMA((n, 4)), pltpu.SemaphoreType.DMA((n,))],
        compiler_params=pltpu.CompilerParams(has_side_effects=True))(*srcs, *lands)
    return list(outs)


def _exchange_copies(waiting):
    relations = [(fx, fy, fc) for fx in (0, 1) for fy in (0, 1) for fc in (0, 1)][1:]

    def copies(src_refs, land_refs, send_sems, recv_sems):
        x, y, c = _mesh_pos()
        me = _dev_index(x, y, c)
        out = []
        for k, (fx, fy, fc) in enumerate(relations):
            peer = (1 - x if fx else x, 1 - y if fy else y, 1 - c if fc else c)
            for a in range(len(src_refs)):
                slot = _dev_index(*peer) if waiting else me
                out.append(pltpu.make_async_remote_copy(
                    src_ref=src_refs[a].at[_dev_index(*peer)], dst_ref=land_refs[a].at[slot], send_sem=send_sems.at[7 * a + k],
                    recv_sem=recv_sems.at[7 * a + k], device_id=peer, device_id_type=pl.DeviceIdType.MESH))
        return out
    return copies


def _own_slot(name, grads, lands):
    n = len(grads)

    def body(*refs):
        g_refs, land_refs, sems = refs[:n], refs[n:2 * n], refs[3 * n]
        x, y, c = _mesh_pos()
        me = _dev_index(x, y, c)
        cps = [pltpu.make_async_copy(g_refs[a].at[me], land_refs[a].at[me], sems.at[a]) for a in range(n)]
        for cp in cps:
            cp.start()
        for cp in cps:
            cp.wait()

    outs = pl.pallas_call(
        body, name=name, out_shape=tuple(SDS(l.shape, l.dtype) for l in lands), in_specs=[ANY] * (2 * n),
        out_specs=tuple([ANY] * n), input_output_aliases={n + q: q for q in range(n)},
        scratch_shapes=[pltpu.SemaphoreType.DMA((n,))],
        compiler_params=pltpu.CompilerParams(has_side_effects=True))(*grads, *lands)
    return list(outs)


def _adamw_math(w, g, m, v):
    m = ADAM_B1 * m + (1.0 - ADAM_B1) * g
    v = ADAM_B2 * v + (1.0 - ADAM_B2) * (g * g)
    m_hat = m / (1.0 - ADAM_B1 ** ADAM_STEP)
    v_hat = v / (1.0 - ADAM_B2 ** ADAM_STEP)
    delta = -ADAM_LR * (m_hat / (jnp.sqrt(v_hat) + ADAM_EPS) + ADAM_WD * w)
    return delta, m, v


def _adamw_layer(name, parts, w, m, v, layer, prev):
    depth, r, c = w.shape
    tr = r
    for cand in (256, 128, 64, 32, 16, 8):
        if r % cand == 0:
            tr = cand
            break

    def body(parts_ref, w_ref, m_ref, v_ref, *rest):
        g_ref, d_ref, nm_ref, nv_ref = rest[-4:]
        g = parts_ref[0].astype(F32)
        for k in range(1, N_DEV):
            g = g + parts_ref[k].astype(F32)
        delta, nm, nv = _adamw_math(w_ref[0], g, m_ref[0], v_ref[0])
        g_ref[0] = g
        d_ref[0] = delta
        nm_ref[0] = nm
        nv_ref[0] = nv

    lay = BS((1, tr, c), lambda i: (layer, i, 0))
    stacked = SDS((depth, r, c), F32)
    ins = [parts, w, m, v]
    specs = [BS((N_DEV, tr, c), lambda i: (0, i, 0)), lay, lay, lay]
    aliases = {}
    if prev is not None:
        ins += list(prev)
        specs += [ANY] * 4
        aliases = {4 + q: q for q in range(4)}
    return pl.pallas_call(body, out_shape=(stacked,) * 4, grid=(r // tr,), in_specs=specs, out_specs=(lay,) * 4,
                          input_output_aliases=aliases, name=name, compiler_params=_params("parallel"))(*ins)


def _adamw_small(parts, w, m, v):
    r, c = w.shape
    tr = _tile(r, 256)

    def body(parts_ref, w_ref, m_ref, v_ref, g_ref, d_ref, nm_ref, nv_ref):
        g = parts_ref[0]
        for k in range(1, N_DEV):
            g = g + parts_ref[k]
        delta, nm, nv = _adamw_math(w_ref[...], g, m_ref[...], v_ref[...])
        g_ref[...] = g
        d_ref[...] = delta
        nm_ref[...] = nm
        nv_ref[...] = nv

    row = BS((tr, c), lambda i: (i, 0))
    out = SDS((r, c), F32)
    return pl.pallas_call(body, out_shape=(out,) * 4, grid=(r // tr,),
                          in_specs=[BS((N_DEV, tr, c), lambda i: (0, i, 0)), row, row, row], out_specs=(row,) * 4,
                          name="adamw_small", compiler_params=_params("parallel"))(parts, w, m, v)


def _pad_w_in(gathered):
    _, d, _ = gathered.shape
    w = jnp.transpose(gathered, (1, 0, 2)).reshape(d, PROJ_RAW)
    real_f = OFF_F + F_COLS
    return jnp.concatenate([w[:, :real_f], jnp.zeros((d, OFF_GU - real_f), w.dtype), w[:, real_f:]], axis=1)


def _unpad_dw_in(dw):
    d = dw.shape[0]
    real_f = OFF_F + F_COLS
    w = jnp.concatenate([dw[:, :real_f], dw[:, OFF_GU:]], axis=1)
    return jnp.transpose(w.reshape(d, N_DEV, PROJ_RAW // N_DEV), (1, 0, 2))


def _pack_small(tree):
    flat = jnp.concatenate([tree[n].reshape(-1) for n in SMALL])
    rows = -(-flat.shape[0] // (256 * HEAD)) * 256
    return jnp.pad(flat, (0, rows * HEAD - flat.shape[0])).reshape(rows, HEAD)


def _unpack_small(packed, like):
    flat = packed.reshape(-1)
    out, off = {}, 0
    for n in SMALL:
        size = like[n].size
        out[n] = flat[off:off + size].reshape(like[n].shape)
        off += size
    return out


def _layer_fwd(h0, p_bf, sw, gw):
    t, d = h0.shape
    xn1 = _rms_fwd("rms_fwd", h0, sw["norm_mix"])
    proj = _matmul("proj_fwd", xn1, gw["w_in"], "nn", F32, t, 512)
    c = _fgate_fwd(proj, sw["forget_bias"])
    cq = c[:, :N_HEADS].T.reshape(N_HEADS, t, 1)
    ck = cq.reshape(N_HEADS, 1, t)
    y_attn, lse = _attn_fwd(proj, cq, ck, sw["q_norm"], sw["k_norm"])
    y_gmlp = _gmlp_fwd(proj, sw["gmlp_v_norm"], sw["gmlp_w_s"], sw["gmlp_b_s"])
    y_pool = _pool_fwd(proj, sw["pool_w"], sw["pool_scale"])
    mix = jnp.concatenate([y_attn, y_gmlp, y_pool], axis=1)
    h1 = _matmul("out_fwd", mix, gw["w_out"], "nn", F32, t, 512, res=h0)
    xn2 = _rms_fwd("rms_fwd", h1, sw["norm_ffn"])
    a, b, hh = _ffn_fwd(xn2, gw["w_ffn_gate"], gw["w_ffn_up"])
    h2 = _ffn_down(hh, gw["w_ffn_down"], h1)
    xn3 = _rms_fwd("rms_fwd", h2, sw["norm_ple"])
    h3, z, pp = _ple_fwd(xn3, gw["w_ple_gate"], p_bf, gw["w_ple_proj"], h2)
    saved = dict(h0=h0, xn1=xn1, proj=proj, cq=cq, ck=ck, lse=lse, mix=mix, h1=h1, xn2=xn2, a=a, b=b, hh=hh, h2=h2,
                 xn3=xn3, z=z, pp=pp)
    return h3, saved


def _layer_bwd(dh3, p_bf, sw, gw, s):
    t, d = dh3.shape
    big, small = {}, {}
    dpp, dz = _ple_bwd_gate(dh3, s["z"], s["pp"])
    big["w_ple_proj"] = _ple_dwpp(p_bf, dpp)
    big["w_ple_gate"] = _matmul("dw_tn", s["xn3"], dz, "tn", BF16, d, 512).reshape(N_DEV, d // N_DEV, d)
    dxn3 = _matmul("dx_nt", dz, gw["w_ple_gate"], "nt", F32, t, 512)
    dh2, dh2_bf, small["norm_ple"] = _rms_bwd_call("rms_bwd", dxn3, s["h2"], sw["norm_ple"], dh3)
    da, db = _ffn_bwd_hidden(dh2_bf, gw["w_ffn_down"], s["a"], s["b"])
    big["w_ffn_down"] = _ffn_dwd(s["hh"], dh2_bf)
    big["w_ffn_gate"], big["w_ffn_up"] = _ffn_dwgu(s["xn2"], da, db)
    dxn2 = _ffn_dxn(da, db, gw["w_ffn_gate"], gw["w_ffn_up"])
    dh1, dh1_bf, small["norm_ffn"] = _rms_bwd_call("rms_bwd", dxn2, s["h1"], sw["norm_ffn"], dh2)
    dmix = _matmul("dx_nt", dh1_bf, gw["w_out"], "nt", F32, t, 512)
    big["w_out"] = _matmul("dw_tn", s["mix"], dh1_bf, "tn", BF16, d, 512).reshape(N_DEV, d // N_DEV, d)
    proj = s["proj"]
    dxp, small["pool_w"], small["pool_scale"] = _pool_bwd(dmix, proj, sw["pool_w"], sw["pool_scale"])
    dgu, dgv, small["gmlp_v_norm"], small["gmlp_w_s"], small["gmlp_b_s"] = _gmlp_bwd(
        dmix, proj, sw["gmlp_v_norm"], sw["gmlp_w_s"], sw["gmlp_b_s"])
    dq, dk, dv, dcq, dck, small["q_norm"], small["k_norm"] = _attn_bwd(
        proj, s["cq"], s["ck"], s["lse"], dmix, sw["q_norm"], sw["k_norm"])
    dc = (dcq.reshape(N_HEADS, t) + dck.reshape(N_HEADS, t)).T
    dc = jnp.pad(dc, ((0, 0), (0, HEAD - N_HEADS)))
    df, small["forget_bias"] = _fgate_bwd(dc, proj, sw["forget_bias"])
    dproj = jnp.concatenate([dq, dk, dv, df, dgu, dgv, dxp], axis=1)
    big["w_in"] = _unpad_dw_in(_matmul("dw_in_tn", s["xn1"], dproj, "tn", BF16, d, 512))
    dxn1 = _matmul("dx_in_nt", dproj, gw["w_in"], "nt", F32, _tile(t, 512), 512)
    dh0, _, small["norm_mix"] = _rms_bwd_call("rms_bwd", dxn1, s["h0"], sw["norm_mix"], dh1)
    return dh0, big, small


def _small_kernel_shapes(sm, i):
    row = lambda a: a[i].reshape(1, -1)
    return dict(
        norm_mix=row(sm["norm_mix"]), norm_ffn=row(sm["norm_ffn"]), norm_ple=row(sm["norm_ple"]),
        q_norm=row(sm["q_norm"]), k_norm=row(sm["k_norm"]),
        forget_bias=jnp.pad(row(sm["forget_bias"]), ((0, 0), (0, HEAD - F_COLS))),
        gmlp_v_norm=row(sm["gmlp_v_norm"]), gmlp_w_s=sm["gmlp_w_s"][i], gmlp_b_s=sm["gmlp_b_s"][i].reshape(N_GROUPS, HEAD, 1),
        pool_w=sm["pool_w"][i], pool_scale=row(sm["pool_scale"]))


def _small_grad_shapes(g, like):
    out = {}
    for n in SMALL:
        v = g[n]
        if n == "forget_bias":
            v = v[:, :F_COLS]
        out[n] = v.reshape(like[n].shape[1:])
    return out


def kernel(x, p, norm_mix, w_in, q_norm, k_norm, forget_bias, gmlp_v_norm, gmlp_w_s, gmlp_b_s, pool_w, pool_scale, w_out, norm_ffn, w_ffn_gate, w_ffn_up, w_ffn_down, norm_ple, w_ple_gate, w_ple_proj, loss_target, m_norm_mix, m_w_in, m_q_norm, m_k_norm, m_forget_bias, m_gmlp_v_norm, m_gmlp_w_s, m_gmlp_b_s, m_pool_w, m_pool_scale, m_w_out, m_norm_ffn, m_w_ffn_gate, m_w_ffn_up, m_w_ffn_down, m_norm_ple, m_w_ple_gate, m_w_ple_proj, v_norm_mix, v_w_in, v_q_norm, v_k_norm, v_forget_bias, v_gmlp_v_norm, v_gmlp_w_s, v_gmlp_b_s, v_pool_w, v_pool_scale, v_w_out, v_norm_ffn, v_w_ffn_gate, v_w_ffn_up, v_w_ffn_down, v_norm_ple, v_w_ple_gate, v_w_ple_proj):
    w = dict(norm_mix=norm_mix, w_in=w_in, q_norm=q_norm, k_norm=k_norm, forget_bias=forget_bias, gmlp_v_norm=gmlp_v_norm,
             gmlp_w_s=gmlp_w_s, gmlp_b_s=gmlp_b_s, pool_w=pool_w, pool_scale=pool_scale, w_out=w_out, norm_ffn=norm_ffn,
             w_ffn_gate=w_ffn_gate, w_ffn_up=w_ffn_up, w_ffn_down=w_ffn_down, norm_ple=norm_ple, w_ple_gate=w_ple_gate,
             w_ple_proj=w_ple_proj)
    m = dict(norm_mix=m_norm_mix, w_in=m_w_in, q_norm=m_q_norm, k_norm=m_k_norm, forget_bias=m_forget_bias,
             gmlp_v_norm=m_gmlp_v_norm, gmlp_w_s=m_gmlp_w_s, gmlp_b_s=m_gmlp_b_s, pool_w=m_pool_w, pool_scale=m_pool_scale,
             w_out=m_w_out, norm_ffn=m_norm_ffn, w_ffn_gate=m_w_ffn_gate, w_ffn_up=m_w_ffn_up, w_ffn_down=m_w_ffn_down,
             norm_ple=m_norm_ple, w_ple_gate=m_w_ple_gate, w_ple_proj=m_w_ple_proj)
    v = dict(norm_mix=v_norm_mix, w_in=v_w_in, q_norm=v_q_norm, k_norm=v_k_norm, forget_bias=v_forget_bias,
             gmlp_v_norm=v_gmlp_v_norm, gmlp_w_s=v_gmlp_w_s, gmlp_b_s=v_gmlp_b_s, pool_w=v_pool_w, pool_scale=v_pool_scale,
             w_out=v_w_out, norm_ffn=v_norm_ffn, w_ffn_gate=v_w_ffn_gate, w_ffn_up=v_w_ffn_up, w_ffn_down=v_w_ffn_down,
             norm_ple=v_norm_ple, w_ple_gate=v_w_ple_gate, w_ple_proj=v_w_ple_proj)
    depth = w_in.shape[0]
    t, d = x.shape[1], x.shape[2]
    h = x[0]
    p_bf = p[:, 0].astype(BF16)
    w_bf = {n: w[n].astype(BF16) for n in BIG}

    srcs = [w_bf[n] for n in BIG]

    def landing_zones():
        return [lax.empty((N_DEV, *a.shape[1:]), a.dtype) for a in srcs]

    def gather_start(i, srcs):
        return _split_start(f"gather_start_l{i}", srcs, landing_zones(), 3, _gather_ici_copies(i, False))

    gathered, saved = [], []
    send_sems, recv_sems, srcs, lands, _ = gather_start(0, srcs)
    for i in range(depth):
        srcs, lands = _split_wait(f"gather_wait_l{i}", srcs, lands, send_sems, recv_sems, h, _gather_ici_copies(i, True))
        sw = _small_kernel_shapes(w, i)
        if i + 1 < depth:
            send_sems, recv_sems, srcs, next_lands, token = gather_start(i + 1, srcs)
            sw["norm_mix"] = sw["norm_mix"] + token[0, 0]
        gw = dict(zip(BIG, _gather_d2d(f"gather_d2d_l{i}", srcs, i, lands)))
        gw["w_in"] = _pad_w_in(gw["w_in"])
        gw["w_out"] = gw["w_out"].reshape(d, d)
        gw["w_ple_gate"] = gw["w_ple_gate"].reshape(d, d)
        h, s = _layer_fwd(h, p_bf[i], sw, gw)
        gathered.append((sw, gw))
        saved.append(s)
        if i + 1 < depth:
            lands = next_lands

    dh, loss_part = _loss_call(h, loss_target[0])
    loss = lax.psum(loss_part[0, 0], ("x", "y", "c"))

    small_grads = [None] * depth
    stacked = {n: None for n in BIG}
    pending = None

    parts = [None] * depth

    def finish(i, grads, lands, send_sems, recv_sems, after):
        grads, lands = _split_wait(f"exchange_wait_l{i}", grads, lands, send_sems, recv_sems, after, _exchange_copies(True))
        parts[i] = dict(zip(BIG, _own_slot(f"exchange_own_l{i}", grads, lands)))

    def adamw(i):
        for n in BIG:
            stacked[n] = _adamw_layer(f"adamw_{n}", parts[i][n], w[n], m[n], v[n], i, stacked[n])

    for i in reversed(range(depth)):
        sw, gw = gathered[i]
        if pending is not None:
            sw = dict(sw, norm_ple=sw["norm_ple"] + pending[-1][0, 0])
        dh, big, small = _layer_bwd(dh, p_bf[i], sw, gw, saved[i])
        small_grads[i] = _small_grad_shapes(small, w)
        if pending is not None:
            finish(*pending[:-1], dh)
        grads = [big[n] for n in BIG]
        send_sems, recv_sems, grads, lands, token = _split_start(
            f"exchange_start_l{i}", grads, [lax.empty(g.shape, g.dtype) for g in grads], 7, _exchange_copies(False))
        pending = (i, grads, lands, send_sems, recv_sems, token)
    for i in reversed(range(1, depth)):
        adamw(i)
    finish(*pending[:-1], stacked[BIG[-1]][0] if depth > 1 else dh)
    adamw(0)

    g_small = {n: jnp.stack([small_grads[i][n] for i in range(depth)]) for n in SMALL}
    parts = _allgather("allgather_small", [(_pack_small(g_small), None)])[0]
    packed = _adamw_small(parts, _pack_small(w), _pack_small(m), _pack_small(v))
    small_out = [_unpack_small(q, w) for q in packed]

    results = []
    for q in range(4):
        results.append({**{n: stacked[n][q] for n in BIG}, **small_out[q]})
    outs = [loss, dh[None]]
    for q in range(4):
        outs += [results[q][n] for n in WEIGHTS]
    return tuple(outs)
```

```python
import functools

import jax
import jax.numpy as jnp
from jax import lax
from jax.experimental import pallas as pl
from jax.experimental.pallas import tpu as pltpu

F32 = jnp.float32
BF16 = jnp.bfloat16
EPS = 1e-6
HEAD = 128
N_HEADS = 8
N_GROUPS = 4
POOL_WINDOWS = (2, 4, 8, 16)
N_DEV = 8
ATTN_Q_BLOCK = 256

OFF_Q, OFF_K, OFF_V, OFF_F, OFF_GU, OFF_GV, OFF_XP, PROJ_PAD = 0, 1024, 2048, 3072, 3584, 4096, 4608, 5120
F_COLS = 8
PROJ_RAW = 4616

ADAM_LR, ADAM_B1, ADAM_B2, ADAM_EPS, ADAM_WD, ADAM_STEP = 0.001, 0.9, 0.999, 1e-08, 0.01, 10

BIG = ("w_in", "w_out", "w_ffn_gate", "w_ffn_up", "w_ffn_down", "w_ple_gate", "w_ple_proj")
SMALL = ("norm_mix", "q_norm", "k_norm", "forget_bias", "gmlp_v_norm", "gmlp_w_s", "gmlp_b_s", "pool_w",
         "pool_scale", "norm_ffn", "norm_ple")
WEIGHTS = ("norm_mix", "w_in", "q_norm", "k_norm", "forget_bias", "gmlp_v_norm", "gmlp_w_s", "gmlp_b_s", "pool_w",
           "pool_scale", "w_out", "norm_ffn", "w_ffn_gate", "w_ffn_up", "w_ffn_down", "norm_ple", "w_ple_gate",
           "w_ple_proj")

VMEM_LIMIT = 56 * 1024 * 1024

BS = pl.BlockSpec
SDS = jax.ShapeDtypeStruct
ANY = pl.BlockSpec(memory_space=pl.ANY)


def _params(*sem):
    return pltpu.CompilerParams(dimension_semantics=sem, vmem_limit_bytes=VMEM_LIMIT)


def _dot(a, b, mode="nn", precision=None):
    ca, cb = {"nn": (1, 0), "nt": (1, 1), "tn": (0, 0)}[mode]
    return lax.dot_general(a, b, (((ca,), (cb,)), ((), ())), preferred_element_type=F32, precision=precision)


def _rms_scale(x):
    return lax.rsqrt(jnp.mean(x * x, axis=-1, keepdims=True) + EPS)


def _rms_bwd(g, xhat, r):
    return r * (g - xhat * jnp.mean(g * xhat, axis=-1, keepdims=True))


def _gelu(x, with_grad=False):
    k = 0.7978845608028654
    inner = k * (x + 0.044715 * x * x * x)
    t = jnp.tanh(inner)
    y = 0.5 * x * (1.0 + t)
    if not with_grad:
        return y
    dy = 0.5 * (1.0 + t) + 0.5 * x * (1.0 - t * t) * k * (1.0 + 3.0 * 0.044715 * x * x)
    return y, dy


def _sigmoid(x):
    return 1.0 / (1.0 + jnp.exp(-x))


def _tile(n, want):
    t = min(n, want)
    assert n % t == 0, (n, want)
    return t


def _matmul(name, a, b, mode, out_dtype, tm, tn, res=None):
    if mode == "nn":
        (m, k), n = a.shape, b.shape[1]
        a_spec, b_spec = BS((tm, k), lambda i, j: (i, 0)), BS((k, tn), lambda i, j: (0, j))
    elif mode == "nt":
        (m, k), n = a.shape, b.shape[0]
        a_spec, b_spec = BS((tm, k), lambda i, j: (i, 0)), BS((tn, k), lambda i, j: (j, 0))
    else:
        (k, m), n = a.shape, b.shape[1]
        a_spec, b_spec = BS((k, tm), lambda i, j: (0, i)), BS((k, tn), lambda i, j: (0, j))
    assert m % tm == 0 and n % tn == 0
    o_spec = BS((tm, tn), lambda i, j: (i, j))

    def body(a_ref, b_ref, *rest):
        o_ref = rest[-1]
        acc = _dot(a_ref[...], b_ref[...], mode)
        if res is not None:
            acc = acc + rest[0][...]
        o_ref[...] = acc.astype(out_dtype)

    ins, specs = [a, b], [a_spec, b_spec]
    if res is not None:
        ins.append(res)
        specs.append(o_spec)
    return pl.pallas_call(body, out_shape=SDS((m, n), out_dtype), grid=(m // tm, n // tn), in_specs=specs,
                          out_specs=o_spec, name=name, compiler_params=_params("parallel", "parallel"))(*ins)


def _rms_fwd(name, h, gain):
    t, d = h.shape
    tm = _tile(t, 256)

    def body(h_ref, g_ref, o_ref):
        x = h_ref[...]
        o_ref[...] = (x * _rms_scale(x) * g_ref[...]).astype(BF16)

    return pl.pallas_call(body, out_shape=SDS((t, d), BF16), grid=(t // tm,),
                          in_specs=[BS((tm, d), lambda i: (i, 0)), BS((1, d), lambda i: (0, 0))],
                          out_specs=BS((tm, d), lambda i: (i, 0)), name=name, compiler_params=_params("parallel"))(h, gain)


def _rms_bwd_call(name, dxn, h, gain, dres):
    t, d = h.shape
    tm = _tile(t, 256)

    def body(dxn_ref, h_ref, g_ref, dres_ref, dh_ref, dhb_ref, dg_ref):
        x = h_ref[...]
        r = _rms_scale(x)
        xhat = x * r
        dy = dxn_ref[...]
        dh = dres_ref[...] + _rms_bwd(dy * g_ref[...], xhat, r)
        dh_ref[...] = dh
        dhb_ref[...] = dh.astype(BF16)

        @pl.when(pl.program_id(0) == 0)
        def _():
            dg_ref[...] = jnp.zeros_like(dg_ref)

        dg_ref[...] += jnp.sum(dy * xhat, axis=0, keepdims=True)

    row = BS((tm, d), lambda i: (i, 0))
    vec = BS((1, d), lambda i: (0, 0))
    return pl.pallas_call(body, out_shape=(SDS((t, d), F32), SDS((t, d), BF16), SDS((1, d), F32)), grid=(t // tm,),
                          in_specs=[row, row, vec, row], out_specs=(row, row, vec), name=name,
                          compiler_params=_params("arbitrary"))(dxn, h, gain, dres)


def _fgate_fwd(proj, bias):
    t = proj.shape[0]
    nb = t // HEAD

    def body(f_ref, b_ref, c_ref):
        tri = (lax.broadcasted_iota(jnp.int32, (HEAD, HEAD), 0) >= lax.broadcasted_iota(jnp.int32, (HEAD, HEAD), 1)).astype(F32)
        carry = jnp.zeros((1, HEAD), F32)
        for n in range(nb):
            rows = slice(n * HEAD, (n + 1) * HEAD)
            x = f_ref[rows, :] + b_ref[...]
            lf = jnp.minimum(x, 0.0) - jnp.log(1.0 + jnp.exp(-jnp.abs(x)))
            cb = _dot(tri, lf, precision=lax.Precision.HIGHEST) + carry
            c_ref[rows, :] = cb
            carry = cb[HEAD - 1:HEAD, :]

    return pl.pallas_call(body, out_shape=SDS((t, HEAD), F32), grid=(1,),
                          in_specs=[BS((t, HEAD), lambda i: (0, OFF_F // HEAD)), BS((1, HEAD), lambda i: (0, 0))],
                          out_specs=BS((t, HEAD), lambda i: (0, 0)), name="fgate_fwd",
                          compiler_params=_params("arbitrary"))(proj, bias)


def _fgate_bwd(dc, proj, bias):
    t = proj.shape[0]
    nb = t // HEAD
    width = OFF_GU - OFF_F

    def body(dc_ref, f_ref, b_ref, df_ref, db_ref):
        tri = (lax.broadcasted_iota(jnp.int32, (HEAD, HEAD), 0) <= lax.broadcasted_iota(jnp.int32, (HEAD, HEAD), 1)).astype(F32)
        carry = jnp.zeros((1, HEAD), F32)
        db = jnp.zeros((1, HEAD), F32)
        df_ref[:, HEAD:] = jnp.zeros((t, width - HEAD), BF16)
        for n in reversed(range(nb)):
            rows = slice(n * HEAD, (n + 1) * HEAD)
            dlf = _dot(tri, dc_ref[rows, :], precision=lax.Precision.HIGHEST) + carry
            carry = dlf[0:1, :]
            x = f_ref[rows, :] + b_ref[...]
            df = dlf * _sigmoid(-x)
            df_ref[rows, 0:HEAD] = df.astype(BF16)
            db = db + jnp.sum(df, axis=0, keepdims=True)
        db_ref[...] = db

    return pl.pallas_call(body, out_shape=(SDS((t, width), BF16), SDS((1, HEAD), F32)), grid=(1,),
                          in_specs=[BS((t, HEAD), lambda i: (0, 0)), BS((t, HEAD), lambda i: (0, OFF_F // HEAD)),
                                    BS((1, HEAD), lambda i: (0, 0))],
                          out_specs=(BS((t, width), lambda i: (0, 0)), BS((1, HEAD), lambda i: (0, 0))),
                          name="fgate_bwd", compiler_params=_params("arbitrary"))(dc, proj, bias)


def _attn_masked_logits(qs_ref, kn_ref, cq_ref, ck_ref, i, tq):
    lo, hi = i * tq, (i + 1) * tq
    s = _dot(qs_ref[lo:hi, :], kn_ref[0:hi, :], "nt")
    s = s + cq_ref[0, lo:hi, :] - ck_ref[0, :, 0:hi]
    row = lax.broadcasted_iota(jnp.int32, (tq, hi), 0) + lo
    col = lax.broadcasted_iota(jnp.int32, (tq, hi), 1)
    return s, row >= col


def _attn_fwd(proj, cq, ck, qg, kg):
    t = proj.shape[0]
    tq = _tile(t, ATTN_Q_BLOCK)
    nq = t // tq
    scale = HEAD ** -0.5

    def body(q_ref, k_ref, v_ref, cq_ref, ck_ref, qg_ref, kg_ref, o_ref, lse_ref, qs_ref, kn_ref, vb_ref):
        q = q_ref[...]
        k = k_ref[...]
        qs_ref[...] = (q * _rms_scale(q) * qg_ref[...] * scale).astype(BF16)
        kn_ref[...] = (k * _rms_scale(k) * kg_ref[...]).astype(BF16)
        vb_ref[...] = v_ref[...].astype(BF16)
        for i in range(nq):
            lo, hi = i * tq, (i + 1) * tq
            s, keep = _attn_masked_logits(qs_ref, kn_ref, cq_ref, ck_ref, i, tq)
            s = jnp.where(keep, s, -1e30)
            m = jnp.max(s, axis=-1, keepdims=True)
            e = jnp.exp(s - m)
            l = jnp.sum(e, axis=-1, keepdims=True)
            o = _dot(e.astype(BF16), vb_ref[0:hi, :]) / l
            o_ref[lo:hi, :] = o.astype(BF16)
            lse_ref[0, lo:hi, :] = m + jnp.log(l)

    def col(off):
        return BS((t, HEAD), lambda h: (0, off // HEAD + h))

    vec = BS((1, HEAD), lambda h: (0, 0))
    return pl.pallas_call(
        body, out_shape=(SDS((t, N_HEADS * HEAD), BF16), SDS((N_HEADS, t, 1), F32)), grid=(N_HEADS,),
        in_specs=[col(OFF_Q), col(OFF_K), col(OFF_V), BS((1, t, 1), lambda h: (h, 0, 0)), BS((1, 1, t), lambda h: (h, 0, 0)),
                  vec, vec],
        out_specs=(BS((t, HEAD), lambda h: (0, h)), BS((1, t, 1), lambda h: (h, 0, 0))),
        scratch_shapes=[pltpu.VMEM((t, HEAD), BF16)] * 3, name="attn_fwd",
        compiler_params=_params("parallel"))(proj, proj, proj, cq, ck, qg, kg)


def _attn_bwd(proj, cq, ck, lse, dmix, qg, kg):
    t = proj.shape[0]
    tq = _tile(t, ATTN_Q_BLOCK)
    nq = t // tq
    scale = HEAD ** -0.5

    def body(q_ref, k_ref, v_ref, cq_ref, ck_ref, lse_ref, do_ref, qg_ref, kg_ref,
             dq_ref, dk_ref, dv_ref, dcq_ref, dck_ref, dqg_ref, dkg_ref,
             qs_ref, kn_ref, vb_ref, dob_ref, dqs_ref, dkn_ref, dva_ref):
        q = q_ref[...]
        k = k_ref[...]
        rq = _rms_scale(q)
        rk = _rms_scale(k)
        qs_ref[...] = (q * rq * qg_ref[...] * scale).astype(BF16)
        kn_ref[...] = (k * rk * kg_ref[...]).astype(BF16)
        vb_ref[...] = v_ref[...].astype(BF16)
        dob_ref[...] = do_ref[...].astype(BF16)
        dkn_ref[...] = jnp.zeros_like(dkn_ref)
        dva_ref[...] = jnp.zeros_like(dva_ref)
        dck_ref[...] = jnp.zeros_like(dck_ref)
        for i in range(nq):
            lo, hi = i * tq, (i + 1) * tq
            s, keep = _attn_masked_logits(qs_ref, kn_ref, cq_ref, ck_ref, i, tq)
            pr = jnp.where(keep, jnp.exp(s - lse_ref[0, lo:hi, :]), 0.0)
            dp = _dot(dob_ref[lo:hi, :], vb_ref[0:hi, :], "nt")
            delta = jnp.sum(pr * dp, axis=-1, keepdims=True)
            ds = pr * (dp - delta)
            dcq_ref[0, lo:hi, :] = jnp.sum(ds, axis=-1, keepdims=True)
            dck_ref[0, :, 0:hi] += -jnp.sum(ds, axis=0, keepdims=True)
            dsb = ds.astype(BF16)
            dqs_ref[lo:hi, :] = _dot(dsb, kn_ref[0:hi, :])
            dkn_ref[0:hi, :] += _dot(dsb, qs_ref[lo:hi, :], "tn")
            dva_ref[0:hi, :] += _dot(pr.astype(BF16), dob_ref[lo:hi, :], "tn")
        dv_ref[...] = dva_ref[...].astype(BF16)

        @pl.when(pl.program_id(0) == 0)
        def _():
            dqg_ref[...] = jnp.zeros_like(dqg_ref)
            dkg_ref[...] = jnp.zeros_like(dkg_ref)

        qhat = q * rq
        dqn = dqs_ref[...] * scale
        dqg_ref[...] += jnp.sum(dqn * qhat, axis=0, keepdims=True)
        dq_ref[...] = _rms_bwd(dqn * qg_ref[...], qhat, rq).astype(BF16)
        khat = k * rk
        dkn = dkn_ref[...]
        dkg_ref[...] += jnp.sum(dkn * khat, axis=0, keepdims=True)
        dk_ref[...] = _rms_bwd(dkn * kg_ref[...], khat, rk).astype(BF16)

    def col(off):
        return BS((t, HEAD), lambda h: (0, off // HEAD + h))

    vec = BS((1, HEAD), lambda h: (0, 0))
    c_col = BS((1, t, 1), lambda h: (h, 0, 0))
    c_row = BS((1, 1, t), lambda h: (h, 0, 0))
    head_out = BS((t, HEAD), lambda h: (0, h))
    width = N_HEADS * HEAD
    return pl.pallas_call(
        body,
        out_shape=(SDS((t, width), BF16), SDS((t, width), BF16), SDS((t, width), BF16), SDS((N_HEADS, t, 1), F32),
                   SDS((N_HEADS, 1, t), F32), SDS((1, HEAD), F32), SDS((1, HEAD), F32)),
        grid=(N_HEADS,),
        in_specs=[col(OFF_Q), col(OFF_K), col(OFF_V), c_col, c_row, c_col, head_out, vec, vec],
        out_specs=(head_out, head_out, head_out, c_col, c_row, vec, vec),
        scratch_shapes=[pltpu.VMEM((t, HEAD), BF16)] * 4 + [pltpu.VMEM((t, HEAD), F32)] * 3, name="attn_bwd",
        compiler_params=_params("arbitrary"))(proj, proj, proj, cq, ck, lse, dmix, qg, kg)


def _group_cols(off):
    width = N_GROUPS * HEAD
    return lambda t: BS((t, width), lambda i: (0, off // width))


def _tril():
    return lax.broadcasted_iota(jnp.int32, (HEAD, HEAD), 0) >= lax.broadcasted_iota(jnp.int32, (HEAD, HEAD), 1)


def _gmlp_fwd(proj, gain, w_s, b_s):
    t = proj.shape[0]
    width = N_GROUPS * HEAD
    nc = t // HEAD

    def body(gu_ref, gv_ref, gain_ref, ws_ref, bs_ref, o_ref):
        tril = _tril()
        for g in range(N_GROUPS):
            cols = slice(g * HEAD, (g + 1) * HEAD)
            u = _gelu(gu_ref[:, cols])
            vv = _gelu(gv_ref[:, cols])
            vn = (vv * _rms_scale(vv) * gain_ref[:, cols]).astype(BF16)
            w = jnp.where(tril, ws_ref[g], 0.0).astype(BF16)
            for n in range(nc):
                rows = slice(n * HEAD, (n + 1) * HEAD)
                mixed = _dot(w, vn[rows]) + bs_ref[g]
                o_ref[rows, cols] = (u[rows] * mixed).astype(BF16)

    full = lambda shape: BS(shape, lambda i: (0,) * len(shape))
    return pl.pallas_call(body, out_shape=SDS((t, width), BF16), grid=(1,),
                          in_specs=[_group_cols(OFF_GU)(t), _group_cols(OFF_GV)(t), full((1, width)),
                                    full((N_GROUPS, HEAD, HEAD)), full((N_GROUPS, HEAD, 1))],
                          out_specs=full((t, width)), name="gmlp_fwd",
                          compiler_params=_params("arbitrary"))(proj, proj, gain, w_s, b_s)


def _gmlp_bwd(dmix, proj, gain, w_s, b_s):
    t = proj.shape[0]
    width = N_GROUPS * HEAD
    nc = t // HEAD

    def body(dy_ref, gu_ref, gv_ref, gain_ref, ws_ref, bs_ref, dgu_ref, dgv_ref, dgain_ref, dws_ref, dbs_ref, dvn_ref):
        tril = _tril()
        for g in range(N_GROUPS):
            cols = slice(g * HEAD, (g + 1) * HEAD)
            u, du = _gelu(gu_ref[:, cols], with_grad=True)
            vv, dvv = _gelu(gv_ref[:, cols], with_grad=True)
            r = _rms_scale(vv)
            vhat = vv * r
            gain_g = gain_ref[:, cols]
            vn = (vhat * gain_g).astype(BF16)
            w = jnp.where(tril, ws_ref[g], 0.0).astype(BF16)
            dws = jnp.zeros((HEAD, HEAD), F32)
            dbs = jnp.zeros((HEAD, 1), F32)
            for n in range(nc):
                rows = slice(n * HEAD, (n + 1) * HEAD)
                mixed = _dot(w, vn[rows]) + bs_ref[g]
                dy = dy_ref[rows, cols]
                dgu_ref[rows, cols] = (dy * mixed * du[rows]).astype(BF16)
                dm = dy * u[rows]
                dmb = dm.astype(BF16)
                dbs = dbs + jnp.sum(dm, axis=-1, keepdims=True)
                dws = dws + _dot(dmb, vn[rows], "nt")
                dvn_ref[rows, :] = _dot(w, dmb, "tn")
            dvn = dvn_ref[...]
            dgain_ref[:, cols] = jnp.sum(dvn * vhat, axis=0, keepdims=True)
            dgv_ref[:, cols] = (_rms_bwd(dvn * gain_g, vhat, r) * dvv).astype(BF16)
            dws_ref[g] = jnp.where(tril, dws, 0.0)
            dbs_ref[g] = dbs

    full = lambda shape: BS(shape, lambda i: (0,) * len(shape))
    return pl.pallas_call(
        body,
        out_shape=(SDS((t, width), BF16), SDS((t, width), BF16), SDS((1, width), F32), SDS((N_GROUPS, HEAD, HEAD), F32),
                   SDS((N_GROUPS, HEAD, 1), F32)),
        grid=(1,),
        in_specs=[BS((t, width), lambda i: (0, 2)), _group_cols(OFF_GU)(t), _group_cols(OFF_GV)(t), full((1, width)),
                  full((N_GROUPS, HEAD, HEAD)), full((N_GROUPS, HEAD, 1))],
        out_specs=(full((t, width)), full((t, width)), full((1, width)), full((N_GROUPS, HEAD, HEAD)),
                   full((N_GROUPS, HEAD, 1))),
        scratch_shapes=[pltpu.VMEM((t, HEAD), F32)], name="gmlp_bwd",
        compiler_params=_params("arbitrary"))(dmix, proj, proj, gain, w_s, b_s)


def _pool_window_mean_minus_x(x, window, t_idx):
    s, span = x, 1
    while span < window:
        s = s + jnp.where(t_idx >= span, pltpu.roll(s, span, 0), 0.0)
        span *= 2
    cnt = jnp.minimum(t_idx + 1, window).astype(F32)
    return s / cnt - x, cnt


def _pool_fwd(proj, w_pool, scale):
    t = proj.shape[0]
    width = N_GROUPS * HEAD

    def body(x_ref, w_ref, sc_ref, o_ref):
        t_idx = lax.broadcasted_iota(jnp.int32, (t, HEAD), 0)
        for g in range(N_GROUPS):
            cols = slice(g * HEAD, (g + 1) * HEAD)
            d, _ = _pool_window_mean_minus_x(x_ref[:, cols], POOL_WINDOWS[g], t_idx)
            y = _dot(d.astype(BF16), w_ref[g].astype(BF16)) * sc_ref[:, cols]
            o_ref[:, cols] = y.astype(BF16)

    full = lambda shape: BS(shape, lambda i: (0,) * len(shape))
    return pl.pallas_call(body, out_shape=SDS((t, width), BF16), grid=(1,),
                          in_specs=[_group_cols(OFF_XP)(t), full((N_GROUPS, HEAD, HEAD)), full((1, width))],
                          out_specs=full((t, width)), name="pool_fwd",
                          compiler_params=_params("arbitrary"))(proj, w_pool, scale)


def _pool_bwd(dmix, proj, w_pool, scale):
    t = proj.shape[0]
    width = N_GROUPS * HEAD

    def body(dy_ref, x_ref, w_ref, sc_ref, dx_ref, dw_ref, dsc_ref):
        t_idx = lax.broadcasted_iota(jnp.int32, (t, HEAD), 0)
        for g in range(N_GROUPS):
            cols = slice(g * HEAD, (g + 1) * HEAD)
            window = POOL_WINDOWS[g]
            d, cnt = _pool_window_mean_minus_x(x_ref[:, cols], window, t_idx)
            db = d.astype(BF16)
            wb = w_ref[g].astype(BF16)
            dy = dy_ref[:, cols]
            dsc_ref[:, cols] = jnp.sum(dy * _dot(db, wb), axis=0, keepdims=True)
            dyw = (dy * sc_ref[:, cols]).astype(BF16)
            dw_ref[g] = _dot(db, dyw, "tn")
            dd = _dot(dyw, wb, "nt")
            rsum, span = dd / cnt, 1
            while span < window:
                rsum = rsum + jnp.where(t_idx < t - span, pltpu.roll(rsum, t - span, 0), 0.0)
                span *= 2
            dx_ref[:, cols] = (rsum - dd).astype(BF16)

    full = lambda shape: BS(shape, lambda i: (0,) * len(shape))
    return pl.pallas_call(
        body, out_shape=(SDS((t, width), BF16), SDS((N_GROUPS, HEAD, HEAD), F32), SDS((1, width), F32)), grid=(1,),
        in_specs=[BS((t, width), lambda i: (0, 3)), _group_cols(OFF_XP)(t), full((N_GROUPS, HEAD, HEAD)), full((1, width))],
        out_specs=(full((t, width)), full((N_GROUPS, HEAD, HEAD)), full((1, width))), name="pool_bwd",
        compiler_params=_params("arbitrary"))(dmix, proj, w_pool, scale)


def _ffn_fwd(xn, wg, wu):
    t, d = xn.shape
    fs = wg.shape[2]
    tm = _tile(t, 512)

    def body(x_ref, wg_ref, wu_ref, a_ref, b_ref, hh_ref):
        x = x_ref[...]
        a = _dot(x, wg_ref[0])
        b = _dot(x, wu_ref[0])
        a_ref[0] = a
        b_ref[0] = b
        hh_ref[0] = (a * _sigmoid(a) * b).astype(BF16)

    w_spec = BS((1, d, fs), lambda j, i: (j, 0, 0))
    o_spec = BS((1, tm, fs), lambda j, i: (j, i, 0))
    return pl.pallas_call(body, out_shape=(SDS((N_DEV, t, fs), F32), SDS((N_DEV, t, fs), F32), SDS((N_DEV, t, fs), BF16)),
                          grid=(N_DEV, t // tm), in_specs=[BS((tm, d), lambda j, i: (i, 0)), w_spec, w_spec],
                          out_specs=(o_spec, o_spec, o_spec), name="ffn_fwd",
                          compiler_params=_params("parallel", "parallel"))(xn, wg, wu)


def _ffn_down(hh, wd, res):
    _, t, fs = hh.shape
    d = wd.shape[2]
    tm, tn = _tile(t, 1024), _tile(d, 1024)

    def body(a_ref, b_ref, r_ref, o_ref, acc_ref):
        k = pl.program_id(2)

        @pl.when(k == 0)
        def _():
            acc_ref[...] = r_ref[...]

        acc_ref[...] += _dot(a_ref[0], b_ref[0])

        @pl.when(k == N_DEV - 1)
        def _():
            o_ref[...] = acc_ref[...]

    o_spec = BS((tm, tn), lambda i, j, k: (i, j))
    return pl.pallas_call(body, out_shape=SDS((t, d), F32), grid=(t // tm, d // tn, N_DEV),
                          in_specs=[BS((1, tm, fs), lambda i, j, k: (k, i, 0)), BS((1, fs, tn), lambda i, j, k: (k, 0, j)), o_spec],
                          out_specs=o_spec, scratch_shapes=[pltpu.VMEM((tm, tn), F32)], name="ffn_down",
                          compiler_params=_params("parallel", "parallel", "arbitrary"))(hh, wd, res)


def _ffn_bwd_hidden(dh_bf, wd, a, b):
    t, d = dh_bf.shape
    fs = wd.shape[1]
    tm = _tile(t, 512)

    def body(dh_ref, wd_ref, a_ref, b_ref, da_ref, db_ref):
        dhh = _dot(dh_ref[...], wd_ref[0], "nt")
        av = a_ref[0]
        sig = _sigmoid(av)
        da_ref[0] = (dhh * b_ref[0] * sig * (1.0 + av * (1.0 - sig))).astype(BF16)
        db_ref[0] = (dhh * av * sig).astype(BF16)

    o_spec = BS((1, tm, fs), lambda j, i: (j, i, 0))
    return pl.pallas_call(body, out_shape=(SDS((N_DEV, t, fs), BF16), SDS((N_DEV, t, fs), BF16)), grid=(N_DEV, t // tm),
                          in_specs=[BS((tm, d), lambda j, i: (i, 0)), BS((1, fs, d), lambda j, i: (j, 0, 0)), o_spec, o_spec],
                          out_specs=(o_spec, o_spec), name="ffn_bwd_hidden",
                          compiler_params=_params("parallel", "parallel"))(dh_bf, wd, a, b)


def _ffn_dwd(hh, dh_bf):
    _, t, fs = hh.shape
    d = dh_bf.shape[1]
    tn = _tile(d, 1024)

    def body(a_ref, b_ref, o_ref):
        o_ref[0] = _dot(a_ref[0], b_ref[...], "tn").astype(BF16)

    return pl.pallas_call(body, out_shape=SDS((N_DEV, fs, d), BF16), grid=(N_DEV, d // tn),
                          in_specs=[BS((1, t, fs), lambda j, n: (j, 0, 0)), BS((t, tn), lambda j, n: (0, n))],
                          out_specs=BS((1, fs, tn), lambda j, n: (j, 0, n)), name="ffn_dwd",
                          compiler_params=_params("parallel", "parallel"))(hh, dh_bf)


def _ffn_dwgu(xn, da, db):
    t, d = xn.shape
    fs = da.shape[2]
    tm = _tile(d, 1024)

    def body(x_ref, da_ref, db_ref, dg_ref, du_ref):
        x = x_ref[...]
        dg_ref[0] = _dot(x, da_ref[0], "tn").astype(BF16)
        du_ref[0] = _dot(x, db_ref[0], "tn").astype(BF16)

    g_spec = BS((1, t, fs), lambda j, i: (j, 0, 0))
    o_spec = BS((1, tm, fs), lambda j, i: (j, i, 0))
    return pl.pallas_call(body, out_shape=(SDS((N_DEV, d, fs), BF16), SDS((N_DEV, d, fs), BF16)), grid=(N_DEV, d // tm),
                          in_specs=[BS((t, tm), lambda j, i: (0, i)), g_spec, g_spec], out_specs=(o_spec, o_spec),
                          name="ffn_dwgu", compiler_params=_params("parallel", "parallel"))(xn, da, db)


def _ffn_dxn(da, db, wg, wu):
    _, t, fs = da.shape
    d = wg.shape[1]
    tm, tn = _tile(t, 1024), _tile(d, 1024)

    def body(da_ref, db_ref, wg_ref, wu_ref, o_ref, acc_ref):
        k = pl.program_id(2)

        @pl.when(k == 0)
        def _():
            acc_ref[...] = jnp.zeros_like(acc_ref)

        acc_ref[...] += _dot(da_ref[0], wg_ref[0], "nt") + _dot(db_ref[0], wu_ref[0], "nt")

        @pl.when(k == N_DEV - 1)
        def _():
            o_ref[...] = acc_ref[...]

    g_spec = BS((1, tm, fs), lambda i, j, k: (k, i, 0))
    w_spec = BS((1, tn, fs), lambda i, j, k: (k, j, 0))
    return pl.pallas_call(body, out_shape=SDS((t, d), F32), grid=(t // tm, d // tn, N_DEV),
                          in_specs=[g_spec, g_spec, w_spec, w_spec], out_specs=BS((tm, tn), lambda i, j, k: (i, j)),
                          scratch_shapes=[pltpu.VMEM((tm, tn), F32)], name="ffn_dxn",
                          compiler_params=_params("parallel", "parallel", "arbitrary"))(da, db, wg, wu)


def _ple_fwd(xn, wpg, p_bf, wpp, h):
    t, d = xn.shape
    dp = p_bf.shape[1]
    tn = wpp.shape[2]
    tm = _tile(t, 1024)

    def body(x_ref, wg_ref, p_ref, wp_ref, h_ref, o_ref, z_ref, pp_ref):
        z = _dot(x_ref[...], wg_ref[...])
        pp = _dot(p_ref[...], wp_ref[0])
        z_ref[...] = z
        pp_ref[...] = pp
        o_ref[...] = h_ref[...] + pp * _sigmoid(z)

    o_spec = BS((tm, tn), lambda i, j: (i, j))
    out = SDS((t, d), F32)
    return pl.pallas_call(body, out_shape=(out, out, out), grid=(t // tm, N_DEV),
                          in_specs=[BS((tm, d), lambda i, j: (i, 0)), BS((d, tn), lambda i, j: (0, j)),
                                    BS((tm, dp), lambda i, j: (i, 0)), BS((1, dp, tn), lambda i, j: (j, 0, 0)), o_spec],
                          out_specs=(o_spec, o_spec, o_spec), name="ple_fwd",
                          compiler_params=_params("parallel", "parallel"))(xn, wpg, p_bf, wpp, h)


def _ple_bwd_gate(dh, z, pp):
    t, d = dh.shape
    tm = _tile(t, 256)

    def body(dh_ref, z_ref, pp_ref, dpp_ref, dz_ref):
        g = _sigmoid(z_ref[...])
        dh_v = dh_ref[...]
        dpp_ref[...] = (dh_v * g).astype(BF16)
        dz_ref[...] = (dh_v * pp_ref[...] * g * (1.0 - g)).astype(BF16)

    row = BS((tm, d), lambda i: (i, 0))
    return pl.pallas_call(body, out_shape=(SDS((t, d), BF16), SDS((t, d), BF16)), grid=(t // tm,), in_specs=[row, row, row],
                          out_specs=(row, row), name="ple_bwd_gate", compiler_params=_params("parallel"))(dh, z, pp)


def _ple_dwpp(p_bf, dpp):
    t, dp = p_bf.shape
    tn = dpp.shape[1] // N_DEV

    def body(p_ref, g_ref, o_ref):
        o_ref[0] = _dot(p_ref[...], g_ref[...], "tn").astype(BF16)

    return pl.pallas_call(body, out_shape=SDS((N_DEV, dp, tn), BF16), grid=(N_DEV,),
                          in_specs=[BS((t, dp), lambda j: (0, 0)), BS((t, tn), lambda j: (0, j))],
                          out_specs=BS((1, dp, tn), lambda j: (j, 0, 0)), name="ple_dwpp",
                          compiler_params=_params("parallel"))(p_bf, dpp)


def _loss_call(y, target):
    t, d = y.shape
    tm = _tile(t, 256)

    def body(y_ref, t_ref, dy_ref, loss_ref):
        diff = y_ref[...] - t_ref[...]
        dy_ref[...] = diff * (1.0 / d)

        @pl.when(pl.program_id(0) == 0)
        def _():
            loss_ref[...] = jnp.zeros_like(loss_ref)

        loss_ref[...] += 0.5 * jnp.sum(jnp.mean(diff * diff, axis=-1, keepdims=True), axis=0, keepdims=True)

    row = BS((tm, d), lambda i: (i, 0))
    return pl.pallas_call(body, out_shape=(SDS((t, d), F32), SDS((1, 1), F32)), grid=(t // tm,), in_specs=[row, row],
                          out_specs=(row, BS((1, 1), lambda i: (0, 0))), name="loss",
                          compiler_params=_params("arbitrary"))(y, target)


def _mesh_pos():
    return lax.axis_index("x"), lax.axis_index("y"), lax.axis_index("c")


def _dev_index(px, py, pc):
    return 4 * px + 2 * py + pc


DMA_CHUNK_BYTES = 256 * 1024


def _row_chunks(ref):
    r, c = ref.shape
    rows = max(16, DMA_CHUNK_BYTES // (c * jnp.dtype(ref.dtype).itemsize) // 16 * 16)
    return [(s, min(rows, r - s)) for s in range(0, r, rows)]


HBM = pl.BlockSpec(memory_space=pltpu.HBM)
SEM = pl.BlockSpec(memory_space=pltpu.SEMAPHORE)
DATAFLOW = pltpu.SideEffectType.DATAFLOW_SIDE_EFFECTING


def _in_hbm(arrs):
    return [pltpu.with_memory_space_constraint(a, pltpu.HBM) for a in arrs]


def _other_chips(x, y):
    return [(1 - x, y), (x, 1 - y), (1 - x, 1 - y)]


def _split_start(name, srcs, lands, n_sems, copies):
    n = len(srcs)

    def body(*refs):
        for cp in copies(refs[:n], refs[n:2 * n], refs[2 * n], refs[2 * n + 1]):
            cp.start()
        token = refs[-1]
        token[...] = jnp.zeros_like(token)

    thru = [pltpu.HBM(a.shape, a.dtype) for a in list(srcs) + list(lands)]
    outs = pl.pallas_call(
        body, name=name,
        out_shape=(pltpu.SemaphoreType.DMA((n * n_sems,)), pltpu.SemaphoreType.DMA((n * n_sems,)), *thru, SDS((8, HEAD), F32)),
        in_specs=[HBM] * (2 * n), out_specs=(SEM, SEM, *([HBM] * (2 * n)), pl.BlockSpec(memory_space=pltpu.VMEM)),
        input_output_aliases={q: 2 + q for q in range(2 * n)},
        compiler_params=pltpu.CompilerParams(has_side_effects=DATAFLOW))(*_in_hbm(list(srcs) + list(lands)))
    return outs[0], outs[1], list(outs[2:2 + n]), list(outs[2 + n:2 + 2 * n]), outs[-1]


def _split_wait(name, srcs, lands, send_sems, recv_sems, after, copies):
    n = len(srcs)

    def body(*refs):
        for cp in copies(refs[:n], refs[n:2 * n], refs[2 * n], refs[2 * n + 1]):
            cp.wait_send()
            cp.wait_recv()

    thru = [pltpu.HBM(a.shape, a.dtype) for a in list(srcs) + list(lands)]
    outs = pl.pallas_call(
        body, name=name, out_shape=tuple(thru), in_specs=[HBM] * (2 * n) + [SEM, SEM, ANY], out_specs=tuple([HBM] * (2 * n)),
        input_output_aliases={q: q for q in range(2 * n)},
        compiler_params=pltpu.CompilerParams(has_side_effects=DATAFLOW))(*list(srcs), *list(lands), send_sems, recv_sems, after)
    return list(outs[:n]), list(outs[n:])


def _gather_ici_copies(layer, waiting):
    def copies(src_refs, land_refs, send_sems, recv_sems):
        x, y, c = _mesh_pos()
        out = []
        for a in range(len(src_refs)):
            for j, chip in enumerate(_other_chips(x, y)):
                slot = _dev_index(*chip, c) if waiting else _dev_index(x, y, c)
                out.append(pltpu.make_async_remote_copy(
                    src_ref=src_refs[a].at[layer], dst_ref=land_refs[a].at[slot], send_sem=send_sems.at[3 * a + j],
                    recv_sem=recv_sems.at[3 * a + j], device_id=(*chip, c), device_id_type=pl.DeviceIdType.MESH))
        return out
    return copies


def _place_own(me, src, layer, land):
    _, r, c = land.shape
    tr = next(cand for cand in (512, 256, 176, 128, 64, 16) if r % cand == 0)

    def body(me_ref, s_ref, l_ref, o_ref):
        o_ref[...] = s_ref[...]

    grid_spec = pltpu.PrefetchScalarGridSpec(
        num_scalar_prefetch=1, grid=(r // tr,), in_specs=[BS((1, tr, c), lambda i, me: (layer, i, 0)), ANY],
        out_specs=BS((1, tr, c), lambda i, me: (me[0], i, 0)))
    return pl.pallas_call(body, grid_spec=grid_spec, out_shape=SDS(land.shape, land.dtype), input_output_aliases={2: 0},
                          name="place_own", compiler_params=_params("parallel"))(me, src, land)


def _gather_d2d(name, srcs, layer, lands):
    n = len(srcs)

    def body(*refs):
        src_refs, land_refs = refs[:n], refs[n:2 * n]
        send_sems, recv_sems = refs[3 * n:]
        x, y, c = _mesh_pos()
        sibling = (x, y, 1 - c)
        blocks = [(x, y)] + _other_chips(x, y)

        def copy(a, k, waiting, rows=None):
            pc = 1 - c if waiting else c
            slot = land_refs[a].at[_dev_index(*blocks[k], pc)]
            src = src_refs[a].at[layer] if (k == 0 and not waiting) else slot
            if rows is not None:
                src, slot = src.at[pl.ds(*rows)], slot.at[pl.ds(*rows)]
            return pltpu.make_async_remote_copy(src_ref=src, dst_ref=slot, send_sem=send_sems.at[a, k],
                                                recv_sem=recv_sems.at[a, k], device_id=sibling,
                                                device_id_type=pl.DeviceIdType.MESH)

        for a in range(n):
            for rows in _row_chunks(src_refs[a].at[layer]):
                for k in range(4):
                    copy(a, k, False, rows).start()
        for a in range(n):
            for k in range(4):
                copy(a, k, True).wait_recv()
        for a in range(n):
            for k in range(4):
                copy(a, k, False).wait_send()

    outs = pl.pallas_call(
        body, name=name, out_shape=tuple(SDS(l.shape, l.dtype) for l in lands), in_specs=[ANY] * (2 * n),
        out_specs=tuple([ANY] * n), input_output_aliases={n + q: q for q in range(n)},
        scratch_shapes=[pltpu.SemaphoreType.DMA((n, 4)), pltpu.SemaphoreType.DMA((n, 4))],
        compiler_params=pltpu.CompilerParams(has_side_effects=True))(*srcs, *lands)
    return list(outs)


def _partial_for(src_ref, land_ref, dev):
    return src_ref if len(src_ref.shape) < len(land_ref.shape) else src_ref.at[dev]


def _exchange_copies(waiting):
    relations = [(fx, fy, fc) for fx in (0, 1) for fy in (0, 1) for fc in (0, 1)][1:]

    def copies(src_refs, land_refs, send_sems, recv_sems):
        x, y, c = _mesh_pos()
        me = _dev_index(x, y, c)
        out = []
        for k, (fx, fy, fc) in enumerate(relations):
            peer = (1 - x if fx else x, 1 - y if fy else y, 1 - c if fc else c)
            for a in range(len(src_refs)):
                slot = _dev_index(*peer) if waiting else me
                out.append(pltpu.make_async_remote_copy(
                    src_ref=_partial_for(src_refs[a], land_refs[a], _dev_index(*peer)), dst_ref=land_refs[a].at[slot],
                    send_sem=send_sems.at[7 * a + k], recv_sem=recv_sems.at[7 * a + k], device_id=peer,
                    device_id_type=pl.DeviceIdType.MESH))
        return out
    return copies


def _adamw_math(w, g, m, v):
    m = ADAM_B1 * m + (1.0 - ADAM_B1) * g
    v = ADAM_B2 * v + (1.0 - ADAM_B2) * (g * g)
    m_hat = m / (1.0 - ADAM_B1 ** ADAM_STEP)
    v_hat = v / (1.0 - ADAM_B2 ** ADAM_STEP)
    delta = -ADAM_LR * (m_hat / (jnp.sqrt(v_hat) + ADAM_EPS) + ADAM_WD * w)
    return delta, m, v


def _sum_partials(me, own, parts_ref):
    g = None
    for k in range(N_DEV):
        term = jnp.where(me == k, own, parts_ref[k].astype(F32))
        g = term if g is None else g + term
    return g


def _adamw_layer(name, me, parts, own, w, m, v, layer, prev):
    depth, r, c = w.shape
    tr = next(cand for cand in (256, 128, 64, 32, 16, 8) if r % cand == 0)

    def body(me_ref, parts_ref, own_ref, w_ref, m_ref, v_ref, *rest):
        g_ref, d_ref, nm_ref, nv_ref = rest[-4:]
        g = _sum_partials(me_ref[0], own_ref[0].astype(F32), parts_ref)
        delta, nm, nv = _adamw_math(w_ref[0], g, m_ref[0], v_ref[0])
        g_ref[0] = g
        d_ref[0] = delta
        nm_ref[0] = nm
        nv_ref[0] = nv

    lay = BS((1, tr, c), lambda i, me: (layer, i, 0))
    stacked = SDS((depth, r, c), F32)
    ins = [me, parts, own, w, m, v]
    specs = [BS((N_DEV, tr, c), lambda i, me: (0, i, 0)), BS((1, tr, c), lambda i, me: (me[0], i, 0)), lay, lay, lay]
    aliases = {}
    if prev is not None:
        ins += list(prev)
        specs += [ANY] * 4
        aliases = {6 + q: q for q in range(4)}
    grid_spec = pltpu.PrefetchScalarGridSpec(num_scalar_prefetch=1, grid=(r // tr,), in_specs=specs, out_specs=(lay,) * 4)
    return pl.pallas_call(body, out_shape=(stacked,) * 4, grid_spec=grid_spec, input_output_aliases=aliases, name=name,
                          compiler_params=_params("parallel"))(*ins)


def _adamw_small(me, parts, own, w, m, v):
    r, c = w.shape
    tr = _tile(r, 256)

    def body(me_ref, parts_ref, own_ref, w_ref, m_ref, v_ref, g_ref, d_ref, nm_ref, nv_ref):
        g = _sum_partials(me_ref[0], own_ref[...], parts_ref)
        delta, nm, nv = _adamw_math(w_ref[...], g, m_ref[...], v_ref[...])
        g_ref[...] = g
        d_ref[...] = delta
        nm_ref[...] = nm
        nv_ref[...] = nv

    row = BS((tr, c), lambda i, me: (i, 0))
    out = SDS((r, c), F32)
    grid_spec = pltpu.PrefetchScalarGridSpec(
        num_scalar_prefetch=1, grid=(r // tr,), in_specs=[BS((N_DEV, tr, c), lambda i, me: (0, i, 0)), row, row, row, row],
        out_specs=(row,) * 4)
    return pl.pallas_call(body, out_shape=(out,) * 4, grid_spec=grid_spec, name="adamw_small",
                          compiler_params=_params("parallel"))(me, parts, own, w, m, v)


def _pad_w_in(gathered):
    _, d, _ = gathered.shape
    w = jnp.transpose(gathered, (1, 0, 2)).reshape(d, PROJ_RAW)
    real_f = OFF_F + F_COLS
    return jnp.concatenate([w[:, :real_f], jnp.zeros((d, OFF_GU - real_f), w.dtype), w[:, real_f:]], axis=1)


def _unpad_dw_in(dw):
    d = dw.shape[0]
    real_f = OFF_F + F_COLS
    w = jnp.concatenate([dw[:, :real_f], dw[:, OFF_GU:]], axis=1)
    return jnp.transpose(w.reshape(d, N_DEV, PROJ_RAW // N_DEV), (1, 0, 2))


def _pack_small(tree):
    flat = jnp.concatenate([tree[n].reshape(-1) for n in SMALL])
    rows = -(-flat.shape[0] // (256 * HEAD)) * 256
    return jnp.pad(flat, (0, rows * HEAD - flat.shape[0])).reshape(rows, HEAD)


def _unpack_small(packed, like):
    flat = packed.reshape(-1)
    out, off = {}, 0
    for n in SMALL:
        size = like[n].size
        out[n] = flat[off:off + size].reshape(like[n].shape)
        off += size
    return out


def _layer_fwd(h0, p_bf, sw, gw):
    t, d = h0.shape
    xn1 = _rms_fwd("rms_fwd", h0, sw["norm_mix"])
    proj = _matmul("proj_fwd", xn1, gw["w_in"], "nn", F32, t, 512)
    c = _fgate_fwd(proj, sw["forget_bias"])
    cq = c[:, :N_HEADS].T.reshape(N_HEADS, t, 1)
    ck = cq.reshape(N_HEADS, 1, t)
    y_attn, lse = _attn_fwd(proj, cq, ck, sw["q_norm"], sw["k_norm"])
    y_gmlp = _gmlp_fwd(proj, sw["gmlp_v_norm"], sw["gmlp_w_s"], sw["gmlp_b_s"])
    y_pool = _pool_fwd(proj, sw["pool_w"], sw["pool_scale"])
    mix = jnp.concatenate([y_attn, y_gmlp, y_pool], axis=1)
    h1 = _matmul("out_fwd", mix, gw["w_out"], "nn", F32, t, 512, res=h0)
    xn2 = _rms_fwd("rms_fwd", h1, sw["norm_ffn"])
    a, b, hh = _ffn_fwd(xn2, gw["w_ffn_gate"], gw["w_ffn_up"])
    h2 = _ffn_down(hh, gw["w_ffn_down"], h1)
    xn3 = _rms_fwd("rms_fwd", h2, sw["norm_ple"])
    h3, z, pp = _ple_fwd(xn3, gw["w_ple_gate"], p_bf, gw["w_ple_proj"], h2)
    saved = dict(h0=h0, xn1=xn1, proj=proj, cq=cq, ck=ck, lse=lse, mix=mix, h1=h1, xn2=xn2, a=a, b=b, hh=hh, h2=h2,
                 xn3=xn3, z=z, pp=pp)
    return h3, saved


FFN_SIDE = ("w_ple_proj", "w_ple_gate", "w_ffn_down", "w_ffn_gate", "w_ffn_up")
MIX_SIDE = ("w_out", "w_in")


def _layer_bwd_ffn(dh3, p_bf, sw, gw, s):
    t, d = dh3.shape
    big, small = {}, {}
    dpp, dz = _ple_bwd_gate(dh3, s["z"], s["pp"])
    big["w_ple_proj"] = _ple_dwpp(p_bf, dpp)
    big["w_ple_gate"] = _matmul("dw_tn", s["xn3"], dz, "tn", BF16, d, 512).reshape(N_DEV, d // N_DEV, d)
    dxn3 = _matmul("dx_nt", dz, gw["w_ple_gate"], "nt", F32, t, 512)
    dh2, dh2_bf, small["norm_ple"] = _rms_bwd_call("rms_bwd", dxn3, s["h2"], sw["norm_ple"], dh3)
    da, db = _ffn_bwd_hidden(dh2_bf, gw["w_ffn_down"], s["a"], s["b"])
    big["w_ffn_down"] = _ffn_dwd(s["hh"], dh2_bf)
    big["w_ffn_gate"], big["w_ffn_up"] = _ffn_dwgu(s["xn2"], da, db)
    dxn2 = _ffn_dxn(da, db, gw["w_ffn_gate"], gw["w_ffn_up"])
    dh1, dh1_bf, small["norm_ffn"] = _rms_bwd_call("rms_bwd", dxn2, s["h1"], sw["norm_ffn"], dh2)
    return (dh1, dh1_bf), big, small


def _layer_bwd_mix(dh1, dh1_bf, sw, gw, s):
    t, d = dh1.shape
    big, small = {}, {}
    dmix = _matmul("dx_nt", dh1_bf, gw["w_out"], "nt", F32, t, 512)
    big["w_out"] = _matmul("dw_tn", s["mix"], dh1_bf, "tn", BF16, d, 512).reshape(N_DEV, d // N_DEV, d)
    proj = s["proj"]
    dxp, small["pool_w"], small["pool_scale"] = _pool_bwd(dmix, proj, sw["pool_w"], sw["pool_scale"])
    dgu, dgv, small["gmlp_v_norm"], small["gmlp_w_s"], small["gmlp_b_s"] = _gmlp_bwd(
        dmix, proj, sw["gmlp_v_norm"], sw["gmlp_w_s"], sw["gmlp_b_s"])
    dq, dk, dv, dcq, dck, small["q_norm"], small["k_norm"] = _attn_bwd(
        proj, s["cq"], s["ck"], s["lse"], dmix, sw["q_norm"], sw["k_norm"])
    dc = (dcq.reshape(N_HEADS, t) + dck.reshape(N_HEADS, t)).T
    dc = jnp.pad(dc, ((0, 0), (0, HEAD - N_HEADS)))
    df, small["forget_bias"] = _fgate_bwd(dc, proj, sw["forget_bias"])
    dproj = jnp.concatenate([dq, dk, dv, df, dgu, dgv, dxp], axis=1)
    big["w_in"] = _unpad_dw_in(_matmul("dw_in_tn", s["xn1"], dproj, "tn", BF16, d, 512))
    dxn1 = _matmul("dx_in_nt", dproj, gw["w_in"], "nt", F32, _tile(t, 512), 512)
    dh0, _, small["norm_mix"] = _rms_bwd_call("rms_bwd", dxn1, s["h0"], sw["norm_mix"], dh1)
    return dh0, big, small


def _small_kernel_shapes(sm, i):
    row = lambda a: a[i].reshape(1, -1)
    return dict(
        norm_mix=row(sm["norm_mix"]), norm_ffn=row(sm["norm_ffn"]), norm_ple=row(sm["norm_ple"]),
        q_norm=row(sm["q_norm"]), k_norm=row(sm["k_norm"]),
        forget_bias=jnp.pad(row(sm["forget_bias"]), ((0, 0), (0, HEAD - F_COLS))),
        gmlp_v_norm=row(sm["gmlp_v_norm"]), gmlp_w_s=sm["gmlp_w_s"][i], gmlp_b_s=sm["gmlp_b_s"][i].reshape(N_GROUPS, HEAD, 1),
        pool_w=sm["pool_w"][i], pool_scale=row(sm["pool_scale"]))


def _small_grad_shapes(g, like):
    out = {}
    for n in SMALL:
        v = g[n]
        if n == "forget_bias":
            v = v[:, :F_COLS]
        out[n] = v.reshape(like[n].shape[1:])
    return out


def kernel(x, p, norm_mix, w_in, q_norm, k_norm, forget_bias, gmlp_v_norm, gmlp_w_s, gmlp_b_s, pool_w, pool_scale, w_out, norm_ffn, w_ffn_gate, w_ffn_up, w_ffn_down, norm_ple, w_ple_gate, w_ple_proj, loss_target, m_norm_mix, m_w_in, m_q_norm, m_k_norm, m_forget_bias, m_gmlp_v_norm, m_gmlp_w_s, m_gmlp_b_s, m_pool_w, m_pool_scale, m_w_out, m_norm_ffn, m_w_ffn_gate, m_w_ffn_up, m_w_ffn_down, m_norm_ple, m_w_ple_gate, m_w_ple_proj, v_norm_mix, v_w_in, v_q_norm, v_k_norm, v_forget_bias, v_gmlp_v_norm, v_gmlp_w_s, v_gmlp_b_s, v_pool_w, v_pool_scale, v_w_out, v_norm_ffn, v_w_ffn_gate, v_w_ffn_up, v_w_ffn_down, v_norm_ple, v_w_ple_gate, v_w_ple_proj):
    w = dict(norm_mix=norm_mix, w_in=w_in, q_norm=q_norm, k_norm=k_norm, forget_bias=forget_bias, gmlp_v_norm=gmlp_v_norm,
             gmlp_w_s=gmlp_w_s, gmlp_b_s=gmlp_b_s, pool_w=pool_w, pool_scale=pool_scale, w_out=w_out, norm_ffn=norm_ffn,
             w_ffn_gate=w_ffn_gate, w_ffn_up=w_ffn_up, w_ffn_down=w_ffn_down, norm_ple=norm_ple, w_ple_gate=w_ple_gate,
             w_ple_proj=w_ple_proj)
    m = dict(norm_mix=m_norm_mix, w_in=m_w_in, q_norm=m_q_norm, k_norm=m_k_norm, forget_bias=m_forget_bias,
             gmlp_v_norm=m_gmlp_v_norm, gmlp_w_s=m_gmlp_w_s, gmlp_b_s=m_gmlp_b_s, pool_w=m_pool_w, pool_scale=m_pool_scale,
             w_out=m_w_out, norm_ffn=m_norm_ffn, w_ffn_gate=m_w_ffn_gate, w_ffn_up=m_w_ffn_up, w_ffn_down=m_w_ffn_down,
             norm_ple=m_norm_ple, w_ple_gate=m_w_ple_gate, w_ple_proj=m_w_ple_proj)
    v = dict(norm_mix=v_norm_mix, w_in=v_w_in, q_norm=v_q_norm, k_norm=v_k_norm, forget_bias=v_forget_bias,
             gmlp_v_norm=v_gmlp_v_norm, gmlp_w_s=v_gmlp_w_s, gmlp_b_s=v_gmlp_b_s, pool_w=v_pool_w, pool_scale=v_pool_scale,
             w_out=v_w_out, norm_ffn=v_norm_ffn, w_ffn_gate=v_w_ffn_gate, w_ffn_up=v_w_ffn_up, w_ffn_down=v_w_ffn_down,
             norm_ple=v_norm_ple, w_ple_gate=v_w_ple_gate, w_ple_proj=v_w_ple_proj)
    depth = w_in.shape[0]
    t, d = x.shape[1], x.shape[2]
    h = x[0]
    me = _dev_index(*_mesh_pos()).astype(jnp.int32).reshape(1)
    p_bf = p[:, 0].astype(BF16)
    w_bf = {n: w[n].astype(BF16) for n in BIG}

    srcs = [w_bf[n] for n in BIG]

    def landing_zones():
        return [lax.empty((N_DEV, *a.shape[1:]), a.dtype) for a in srcs]

    def gather_start(i, srcs):
        return _split_start(f"gather_start_l{i}", srcs, landing_zones(), 3, _gather_ici_copies(i, False))

    gathered, saved = [], []
    send_sems, recv_sems, srcs, lands, _ = gather_start(0, srcs)
    for i in range(depth):
        srcs, lands = _split_wait(f"gather_wait_l{i}", srcs, lands, send_sems, recv_sems, h, _gather_ici_copies(i, True))
        sw = _small_kernel_shapes(w, i)
        if i + 1 < depth:
            send_sems, recv_sems, srcs, next_lands, token = gather_start(i + 1, srcs)
            sw["norm_mix"] = sw["norm_mix"] + token[0, 0]
        lands = _gather_d2d(f"gather_d2d_l{i}", srcs, i, lands)
        gw = {n: _place_own(me, src, i, land) for n, src, land in zip(BIG, srcs, lands)}
        gw["w_in"] = _pad_w_in(gw["w_in"])
        gw["w_out"] = gw["w_out"].reshape(d, d)
        gw["w_ple_gate"] = gw["w_ple_gate"].reshape(d, d)
        h, s = _layer_fwd(h, p_bf[i], sw, gw)
        gathered.append((sw, gw))
        saved.append(s)
        if i + 1 < depth:
            lands = next_lands

    dh, loss_part = _loss_call(h, loss_target[0])
    loss = lax.psum(loss_part[0, 0], ("x", "y", "c"))

    small_grads = [None] * depth
    stacked = {n: None for n in BIG}
    parts = [dict() for _ in range(depth)]
    inflight = [[] for _ in range(depth)]

    def start(tag, layer, names, grads):
        lands = [lax.empty(g.shape if g.ndim == 3 else (N_DEV, *g.shape), g.dtype) for g in grads]
        send_sems, recv_sems, grads, lands, token = _split_start(f"exchange_start_{tag}", grads, lands, 7, _exchange_copies(False))
        inflight[layer].append((tag, names, grads, lands, send_sems, recv_sems))
        return token

    def finish(layer, after):
        for tag, names, grads, lands, send_sems, recv_sems in inflight[layer]:
            grads, lands = _split_wait(f"exchange_wait_{tag}", grads, lands, send_sems, recv_sems, after, _exchange_copies(True))
            parts[layer].update(zip(names, zip(lands, grads)))
        for n in BIG:
            stacked[n] = _adamw_layer(f"adamw_{n}", me, *parts[layer][n], w[n], m[n], v[n], layer, stacked[n])

    token = (loss * 0.0).reshape(1, 1)
    for i in reversed(range(depth)):
        sw, gw = gathered[i]
        sw = dict(sw, norm_ple=sw["norm_ple"] + token[0, 0])
        (dh1, dh1_bf), big, small = _layer_bwd_ffn(dh, p_bf[i], sw, gw, saved[i])
        token = start(f"ffn_l{i}", i, FFN_SIDE, [big[n] for n in FFN_SIDE])
        sw = dict(sw, pool_scale=sw["pool_scale"] + token[0, 0])
        dh, big, small_mix = _layer_bwd_mix(dh1, dh1_bf, sw, gw, saved[i])
        small_grads[i] = _small_grad_shapes({**small, **small_mix}, w)
        names, grads = list(MIX_SIDE), [big[n] for n in MIX_SIDE]
        if i == 0:
            g_small = {n: jnp.stack([small_grads[q][n] for q in range(depth)]) for n in SMALL}
            names.append("small")
            grads.append(_pack_small(g_small))
        token = start(f"mix_l{i}", i, names, grads)
        if i + 1 < depth:
            finish(i + 1, token)
    finish(0, stacked[BIG[-1]][0] if depth > 1 else token)
    packed = _adamw_small(me, *parts[0]["small"], _pack_small(w), _pack_small(m), _pack_small(v))
    small_out = [_unpack_small(q, w) for q in packed]

    results = []
    for q in range(4):
        results.append({**{n: stacked[n][q] for n in BIG}, **small_out[q]})
    outs = [loss, dh[None]]
    for q in range(4):
        outs += [results[q][n] for n in WEIGHTS]
    return tuple(outs)
```

```python
import functools

import jax
import jax.numpy as jnp
from jax import lax
from jax.experimental import pallas as pl
from jax.experimental.pallas import tpu as pltpu

F32 = jnp.float32
BF16 = jnp.bfloat16
EPS = 1e-6
HEAD = 128
N_HEADS = 8
N_GROUPS = 4
POOL_WINDOWS = (2, 4, 8, 16)
N_DEV = 8
ATTN_Q_BLOCK = 256

OFF_Q, OFF_K, OFF_V, OFF_F, OFF_GU, OFF_GV, OFF_XP, PROJ_PAD = 0, 1024, 2048, 3072, 3584, 4096, 4608, 5120
F_COLS = 8
PROJ_RAW = 4616

ADAM_LR, ADAM_B1, ADAM_B2, ADAM_EPS, ADAM_WD, ADAM_STEP = 0.001, 0.9, 0.999, 1e-08, 0.01, 10

BIG = ("w_in", "w_out", "w_ffn_gate", "w_ffn_up", "w_ffn_down", "w_ple_gate", "w_ple_proj")
TRANSPOSED = ("w_ffn_gate", "w_ffn_up")
SMALL = ("norm_mix", "q_norm", "k_norm", "forget_bias", "gmlp_v_norm", "gmlp_w_s", "gmlp_b_s", "pool_w",
         "pool_scale", "norm_ffn", "norm_ple")
WEIGHTS = ("norm_mix", "w_in", "q_norm", "k_norm", "forget_bias", "gmlp_v_norm", "gmlp_w_s", "gmlp_b_s", "pool_w",
           "pool_scale", "w_out", "norm_ffn", "w_ffn_gate", "w_ffn_up", "w_ffn_down", "norm_ple", "w_ple_gate",
           "w_ple_proj")

VMEM_LIMIT = 56 * 1024 * 1024

BS = pl.BlockSpec
SDS = jax.ShapeDtypeStruct
ANY = pl.BlockSpec(memory_space=pl.ANY)


def _params(*sem):
    return pltpu.CompilerParams(dimension_semantics=sem, vmem_limit_bytes=VMEM_LIMIT)


def _dot(a, b, mode="nn", precision=None):
    ca, cb = {"nn": (1, 0), "nt": (1, 1), "tn": (0, 0)}[mode]
    return lax.dot_general(a, b, (((ca,), (cb,)), ((), ())), preferred_element_type=F32, precision=precision)


def _rms_scale(x):
    return lax.rsqrt(jnp.mean(x * x, axis=-1, keepdims=True) + EPS)


def _rms_bwd(g, xhat, r):
    return r * (g - xhat * jnp.mean(g * xhat, axis=-1, keepdims=True))


def _gelu(x, with_grad=False):
    k = 0.7978845608028654
    inner = k * (x + 0.044715 * x * x * x)
    t = jnp.tanh(inner)
    y = 0.5 * x * (1.0 + t)
    if not with_grad:
        return y
    dy = 0.5 * (1.0 + t) + 0.5 * x * (1.0 - t * t) * k * (1.0 + 3.0 * 0.044715 * x * x)
    return y, dy


def _sigmoid(x):
    return 1.0 / (1.0 + jnp.exp(-x))


def _tile(n, want):
    t = min(n, want)
    assert n % t == 0, (n, want)
    return t


def _matmul(name, a, b, mode, out_dtype, tm, tn, res=None, after=None):
    if mode == "nn":
        (m, k), n = a.shape, b.shape[1]
        a_spec, b_spec = BS((tm, k), lambda i, j: (i, 0)), BS((k, tn), lambda i, j: (0, j))
    elif mode == "nt":
        (m, k), n = a.shape, b.shape[0]
        a_spec, b_spec = BS((tm, k), lambda i, j: (i, 0)), BS((tn, k), lambda i, j: (j, 0))
    else:
        (k, m), n = a.shape, b.shape[1]
        a_spec, b_spec = BS((k, tm), lambda i, j: (0, i)), BS((k, tn), lambda i, j: (0, j))
    assert m % tm == 0 and n % tn == 0
    o_spec = BS((tm, tn), lambda i, j: (i, j))

    def body(a_ref, b_ref, *rest):
        o_ref = rest[-1]
        acc = _dot(a_ref[...], b_ref[...], mode)
        if res is not None:
            acc = acc + rest[0][...]
        o_ref[...] = acc.astype(out_dtype)

    ins, specs = [a, b], [a_spec, b_spec]
    if res is not None:
        ins.append(res)
        specs.append(o_spec)
    if after is not None:
        ins.append(after)
        specs.append(BS(after.shape, lambda i, j: (0, 0)))
    return pl.pallas_call(body, out_shape=SDS((m, n), out_dtype), grid=(m // tm, n // tn), in_specs=specs,
                          out_specs=o_spec, name=name, compiler_params=_params("parallel", "parallel"))(*ins)


def _rms_fwd(name, h, gain):
    t, d = h.shape
    tm = _tile(t, 256)

    def body(h_ref, g_ref, o_ref):
        x = h_ref[...]
        o_ref[...] = (x * _rms_scale(x) * g_ref[...]).astype(BF16)

    return pl.pallas_call(body, out_shape=SDS((t, d), BF16), grid=(t // tm,),
                          in_specs=[BS((tm, d), lambda i: (i, 0)), BS((1, d), lambda i: (0, 0))],
                          out_specs=BS((tm, d), lambda i: (i, 0)), name=name, compiler_params=_params("parallel"))(h, gain)


def _rms_bwd_call(name, dxn, h, gain, dres):
    t, d = h.shape
    tm = _tile(t, 256)

    def body(dxn_ref, h_ref, g_ref, dres_ref, dh_ref, dhb_ref, dg_ref):
        x = h_ref[...]
        r = _rms_scale(x)
        xhat = x * r
        dy = dxn_ref[...]
        dh = dres_ref[...] + _rms_bwd(dy * g_ref[...], xhat, r)
        dh_ref[...] = dh
        dhb_ref[...] = dh.astype(BF16)

        @pl.when(pl.program_id(0) == 0)
        def _():
            dg_ref[...] = jnp.zeros_like(dg_ref)

        dg_ref[...] += jnp.sum(dy * xhat, axis=0, keepdims=True)

    row = BS((tm, d), lambda i: (i, 0))
    vec = BS((1, d), lambda i: (0, 0))
    return pl.pallas_call(body, out_shape=(SDS((t, d), F32), SDS((t, d), BF16), SDS((1, d), F32)), grid=(t // tm,),
                          in_specs=[row, row, vec, row], out_specs=(row, row, vec), name=name,
                          compiler_params=_params("arbitrary"))(dxn, h, gain, dres)


def _fgate_fwd(proj, bias):
    t = proj.shape[0]
    nb = t // HEAD

    def body(f_ref, b_ref, c_ref):
        tri = (lax.broadcasted_iota(jnp.int32, (HEAD, HEAD), 0) >= lax.broadcasted_iota(jnp.int32, (HEAD, HEAD), 1)).astype(F32)
        carry = jnp.zeros((1, HEAD), F32)
        for n in range(nb):
            rows = slice(n * HEAD, (n + 1) * HEAD)
            x = f_ref[rows, :] + b_ref[...]
            lf = jnp.minimum(x, 0.0) - jnp.log(1.0 + jnp.exp(-jnp.abs(x)))
            cb = _dot(tri, lf, precision=lax.Precision.HIGHEST) + carry
            c_ref[rows, :] = cb
            carry = cb[HEAD - 1:HEAD, :]

    return pl.pallas_call(body, out_shape=SDS((t, HEAD), F32), grid=(1,),
                          in_specs=[BS((t, HEAD), lambda i: (0, OFF_F // HEAD)), BS((1, HEAD), lambda i: (0, 0))],
                          out_specs=BS((t, HEAD), lambda i: (0, 0)), name="fgate_fwd",
                          compiler_params=_params("arbitrary"))(proj, bias)


def _fgate_bwd(dc, proj, bias):
    t = proj.shape[0]
    nb = t // HEAD
    width = OFF_GU - OFF_F

    def body(dc_ref, f_ref, b_ref, df_ref, db_ref):
        tri = (lax.broadcasted_iota(jnp.int32, (HEAD, HEAD), 0) <= lax.broadcasted_iota(jnp.int32, (HEAD, HEAD), 1)).astype(F32)
        carry = jnp.zeros((1, HEAD), F32)
        db = jnp.zeros((1, HEAD), F32)
        df_ref[:, HEAD:] = jnp.zeros((t, width - HEAD), BF16)
        for n in reversed(range(nb)):
            rows = slice(n * HEAD, (n + 1) * HEAD)
            dlf = _dot(tri, dc_ref[rows, :], precision=lax.Precision.HIGHEST) + carry
            carry = dlf[0:1, :]
            x = f_ref[rows, :] + b_ref[...]
            df = dlf * _sigmoid(-x)
            df_ref[rows, 0:HEAD] = df.astype(BF16)
            db = db + jnp.sum(df, axis=0, keepdims=True)
        db_ref[...] = db

    return pl.pallas_call(body, out_shape=(SDS((t, width), BF16), SDS((1, HEAD), F32)), grid=(1,),
                          in_specs=[BS((t, HEAD), lambda i: (0, 0)), BS((t, HEAD), lambda i: (0, OFF_F // HEAD)),
                                    BS((1, HEAD), lambda i: (0, 0))],
                          out_specs=(BS((t, width), lambda i: (0, 0)), BS((1, HEAD), lambda i: (0, 0))),
                          name="fgate_bwd", compiler_params=_params("arbitrary"))(dc, proj, bias)


def _attn_masked_logits(qs_ref, kn_ref, cq_ref, ck_ref, i, tq):
    lo, hi = i * tq, (i + 1) * tq
    s = _dot(qs_ref[lo:hi, :], kn_ref[0:hi, :], "nt")
    s = s + cq_ref[0, lo:hi, :] - ck_ref[0, :, 0:hi]
    row = lax.broadcasted_iota(jnp.int32, (tq, hi), 0) + lo
    col = lax.broadcasted_iota(jnp.int32, (tq, hi), 1)
    return s, row >= col


def _attn_fwd(proj, cq, ck, qg, kg):
    t = proj.shape[0]
    tq = _tile(t, ATTN_Q_BLOCK)
    nq = t // tq
    scale = HEAD ** -0.5

    def body(q_ref, k_ref, v_ref, cq_ref, ck_ref, qg_ref, kg_ref, o_ref, lse_ref, qs_ref, kn_ref, vb_ref):
        q = q_ref[...]
        k = k_ref[...]
        qs_ref[...] = (q * _rms_scale(q) * qg_ref[...] * scale).astype(BF16)
        kn_ref[...] = (k * _rms_scale(k) * kg_ref[...]).astype(BF16)
        vb_ref[...] = v_ref[...].astype(BF16)
        for i in range(nq):
            lo, hi = i * tq, (i + 1) * tq
            s, keep = _attn_masked_logits(qs_ref, kn_ref, cq_ref, ck_ref, i, tq)
            s = jnp.where(keep, s, -1e30)
            m = jnp.max(s, axis=-1, keepdims=True)
            e = jnp.exp(s - m)
            l = jnp.sum(e, axis=-1, keepdims=True)
            o = _dot(e.astype(BF16), vb_ref[0:hi, :]) / l
            o_ref[lo:hi, :] = o.astype(BF16)
            lse_ref[0, lo:hi, :] = m + jnp.log(l)

    def col(off):
        return BS((t, HEAD), lambda h: (0, off // HEAD + h))

    vec = BS((1, HEAD), lambda h: (0, 0))
    return pl.pallas_call(
        body, out_shape=(SDS((t, N_HEADS * HEAD), BF16), SDS((N_HEADS, t, 1), F32)), grid=(N_HEADS,),
        in_specs=[col(OFF_Q), col(OFF_K), col(OFF_V), BS((1, t, 1), lambda h: (h, 0, 0)), BS((1, 1, t), lambda h: (h, 0, 0)),
                  vec, vec],
        out_specs=(BS((t, HEAD), lambda h: (0, h)), BS((1, t, 1), lambda h: (h, 0, 0))),
        scratch_shapes=[pltpu.VMEM((t, HEAD), BF16)] * 3, name="attn_fwd",
        compiler_params=_params("parallel"))(proj, proj, proj, cq, ck, qg, kg)


def _attn_bwd(proj, cq, ck, lse, dmix, qg, kg):
    t = proj.shape[0]
    tq = _tile(t, ATTN_Q_BLOCK)
    nq = t // tq
    scale = HEAD ** -0.5

    def body(q_ref, k_ref, v_ref, cq_ref, ck_ref, lse_ref, do_ref, qg_ref, kg_ref,
             dq_ref, dk_ref, dv_ref, dcq_ref, dck_ref, dqg_ref, dkg_ref,
             qs_ref, kn_ref, vb_ref, dob_ref, dqs_ref, dkn_ref, dva_ref):
        q = q_ref[...]
        k = k_ref[...]
        rq = _rms_scale(q)
        rk = _rms_scale(k)
        qs_ref[...] = (q * rq * qg_ref[...] * scale).astype(BF16)
        kn_ref[...] = (k * rk * kg_ref[...]).astype(BF16)
        vb_ref[...] = v_ref[...].astype(BF16)
        dob_ref[...] = do_ref[...].astype(BF16)
        dkn_ref[...] = jnp.zeros_like(dkn_ref)
        dva_ref[...] = jnp.zeros_like(dva_ref)
        dck_ref[...] = jnp.zeros_like(dck_ref)
        for i in range(nq):
            lo, hi = i * tq, (i + 1) * tq
            s, keep = _attn_masked_logits(qs_ref, kn_ref, cq_ref, ck_ref, i, tq)
            pr = jnp.where(keep, jnp.exp(s - lse_ref[0, lo:hi, :]), 0.0)
            dp = _dot(dob_ref[lo:hi, :], vb_ref[0:hi, :], "nt")
            delta = jnp.sum(pr * dp, axis=-1, keepdims=True)
            ds = pr * (dp - delta)
            dcq_ref[0, lo:hi, :] = jnp.sum(ds, axis=-1, keepdims=True)
            dck_ref[0, :, 0:hi] += -jnp.sum(ds, axis=0, keepdims=True)
            dsb = ds.astype(BF16)
            dqs_ref[lo:hi, :] = _dot(dsb, kn_ref[0:hi, :])
            dkn_ref[0:hi, :] += _dot(dsb, qs_ref[lo:hi, :], "tn")
            dva_ref[0:hi, :] += _dot(pr.astype(BF16), dob_ref[lo:hi, :], "tn")
        dv_ref[...] = dva_ref[...].astype(BF16)

        @pl.when(pl.program_id(0) == 0)
        def _():
            dqg_ref[...] = jnp.zeros_like(dqg_ref)
            dkg_ref[...] = jnp.zeros_like(dkg_ref)

        qhat = q * rq
        dqn = dqs_ref[...] * scale
        dqg_ref[...] += jnp.sum(dqn * qhat, axis=0, keepdims=True)
        dq_ref[...] = _rms_bwd(dqn * qg_ref[...], qhat, rq).astype(BF16)
        khat = k * rk
        dkn = dkn_ref[...]
        dkg_ref[...] += jnp.sum(dkn * khat, axis=0, keepdims=True)
        dk_ref[...] = _rms_bwd(dkn * kg_ref[...], khat, rk).astype(BF16)

    def col(off):
        return BS((t, HEAD), lambda h: (0, off // HEAD + h))

    vec = BS((1, HEAD), lambda h: (0, 0))
    c_col = BS((1, t, 1), lambda h: (h, 0, 0))
    c_row = BS((1, 1, t), lambda h: (h, 0, 0))
    head_out = BS((t, HEAD), lambda h: (0, h))
    width = N_HEADS * HEAD
    return pl.pallas_call(
        body,
        out_shape=(SDS((t, width), BF16), SDS((t, width), BF16), SDS((t, width), BF16), SDS((N_HEADS, t, 1), F32),
                   SDS((N_HEADS, 1, t), F32), SDS((1, HEAD), F32), SDS((1, HEAD), F32)),
        grid=(N_HEADS,),
        in_specs=[col(OFF_Q), col(OFF_K), col(OFF_V), c_col, c_row, c_col, head_out, vec, vec],
        out_specs=(head_out, head_out, head_out, c_col, c_row, vec, vec),
        scratch_shapes=[pltpu.VMEM((t, HEAD), BF16)] * 4 + [pltpu.VMEM((t, HEAD), F32)] * 3, name="attn_bwd",
        compiler_params=_params("arbitrary"))(proj, proj, proj, cq, ck, lse, dmix, qg, kg)


def _group_cols(off):
    width = N_GROUPS * HEAD
    return lambda t: BS((t, width), lambda i: (0, off // width))


def _tril():
    return lax.broadcasted_iota(jnp.int32, (HEAD, HEAD), 0) >= lax.broadcasted_iota(jnp.int32, (HEAD, HEAD), 1)


def _gmlp_fwd(proj, gain, w_s, b_s):
    t = proj.shape[0]
    width = N_GROUPS * HEAD
    nc = t // HEAD

    def body(gu_ref, gv_ref, gain_ref, ws_ref, bs_ref, o_ref):
        tril = _tril()
        for g in range(N_GROUPS):
            cols = slice(g * HEAD, (g + 1) * HEAD)
            u = _gelu(gu_ref[:, cols])
            vv = _gelu(gv_ref[:, cols])
            vn = (vv * _rms_scale(vv) * gain_ref[:, cols]).astype(BF16)
            w = jnp.where(tril, ws_ref[g], 0.0).astype(BF16)
            for n in range(nc):
                rows = slice(n * HEAD, (n + 1) * HEAD)
                mixed = _dot(w, vn[rows]) + bs_ref[g]
                o_ref[rows, cols] = (u[rows] * mixed).astype(BF16)

    full = lambda shape: BS(shape, lambda i: (0,) * len(shape))
    return pl.pallas_call(body, out_shape=SDS((t, width), BF16), grid=(1,),
                          in_specs=[_group_cols(OFF_GU)(t), _group_cols(OFF_GV)(t), full((1, width)),
                                    full((N_GROUPS, HEAD, HEAD)), full((N_GROUPS, HEAD, 1))],
                          out_specs=full((t, width)), name="gmlp_fwd",
                          compiler_params=_params("arbitrary"))(proj, proj, gain, w_s, b_s)


def _gmlp_bwd(dmix, proj, gain, w_s, b_s):
    t = proj.shape[0]
    width = N_GROUPS * HEAD
    nc = t // HEAD

    def body(dy_ref, gu_ref, gv_ref, gain_ref, ws_ref, bs_ref, dgu_ref, dgv_ref, dgain_ref, dws_ref, dbs_ref, dvn_ref):
        tril = _tril()
        for g in range(N_GROUPS):
            cols = slice(g * HEAD, (g + 1) * HEAD)
            u, du = _gelu(gu_ref[:, cols], with_grad=True)
            vv, dvv = _gelu(gv_ref[:, cols], with_grad=True)
            r = _rms_scale(vv)
            vhat = vv * r
            gain_g = gain_ref[:, cols]
            vn = (vhat * gain_g).astype(BF16)
            w = jnp.where(tril, ws_ref[g], 0.0).astype(BF16)
            dws = jnp.zeros((HEAD, HEAD), F32)
            dbs = jnp.zeros((HEAD, 1), F32)
            for n in range(nc):
                rows = slice(n * HEAD, (n + 1) * HEAD)
                mixed = _dot(w, vn[rows]) + bs_ref[g]
                dy = dy_ref[rows, cols]
                dgu_ref[rows, cols] = (dy * mixed * du[rows]).astype(BF16)
                dm = dy * u[rows]
                dmb = dm.astype(BF16)
                dbs = dbs + jnp.sum(dm, axis=-1, keepdims=True)
                dws = dws + _dot(dmb, vn[rows], "nt")
                dvn_ref[rows, :] = _dot(w, dmb, "tn")
            dvn = dvn_ref[...]
            dgain_ref[:, cols] = jnp.sum(dvn * vhat, axis=0, keepdims=True)
            dgv_ref[:, cols] = (_rms_bwd(dvn * gain_g, vhat, r) * dvv).astype(BF16)
            dws_ref[g] = jnp.where(tril, dws, 0.0)
            dbs_ref[g] = dbs

    full = lambda shape: BS(shape, lambda i: (0,) * len(shape))
    return pl.pallas_call(
        body,
        out_shape=(SDS((t, width), BF16), SDS((t, width), BF16), SDS((1, width), F32), SDS((N_GROUPS, HEAD, HEAD), F32),
                   SDS((N_GROUPS, HEAD, 1), F32)),
        grid=(1,),
        in_specs=[BS((t, width), lambda i: (0, 2)), _group_cols(OFF_GU)(t), _group_cols(OFF_GV)(t), full((1, width)),
                  full((N_GROUPS, HEAD, HEAD)), full((N_GROUPS, HEAD, 1))],
        out_specs=(full((t, width)), full((t, width)), full((1, width)), full((N_GROUPS, HEAD, HEAD)),
                   full((N_GROUPS, HEAD, 1))),
        scratch_shapes=[pltpu.VMEM((t, HEAD), F32)], name="gmlp_bwd",
        compiler_params=_params("arbitrary"))(dmix, proj, proj, gain, w_s, b_s)


def _pool_window_mean_minus_x(x, window, t_idx):
    s, span = x, 1
    while span < window:
        s = s + jnp.where(t_idx >= span, pltpu.roll(s, span, 0), 0.0)
        span *= 2
    cnt = jnp.minimum(t_idx + 1, window).astype(F32)
    return s / cnt - x, cnt


def _pool_fwd(proj, w_pool, scale):
    t = proj.shape[0]
    width = N_GROUPS * HEAD

    def body(x_ref, w_ref, sc_ref, o_ref):
        t_idx = lax.broadcasted_iota(jnp.int32, (t, HEAD), 0)
        for g in range(N_GROUPS):
            cols = slice(g * HEAD, (g + 1) * HEAD)
            d, _ = _pool_window_mean_minus_x(x_ref[:, cols], POOL_WINDOWS[g], t_idx)
            y = _dot(d.astype(BF16), w_ref[g].astype(BF16)) * sc_ref[:, cols]
            o_ref[:, cols] = y.astype(BF16)

    full = lambda shape: BS(shape, lambda i: (0,) * len(shape))
    return pl.pallas_call(body, out_shape=SDS((t, width), BF16), grid=(1,),
                          in_specs=[_group_cols(OFF_XP)(t), full((N_GROUPS, HEAD, HEAD)), full((1, width))],
                          out_specs=full((t, width)), name="pool_fwd",
                          compiler_params=_params("arbitrary"))(proj, w_pool, scale)


def _pool_bwd(dmix, proj, w_pool, scale):
    t = proj.shape[0]
    width = N_GROUPS * HEAD

    def body(dy_ref, x_ref, w_ref, sc_ref, dx_ref, dw_ref, dsc_ref):
        t_idx = lax.broadcasted_iota(jnp.int32, (t, HEAD), 0)
        for g in range(N_GROUPS):
            cols = slice(g * HEAD, (g + 1) * HEAD)
            window = POOL_WINDOWS[g]
            d, cnt = _pool_window_mean_minus_x(x_ref[:, cols], window, t_idx)
            db = d.astype(BF16)
            wb = w_ref[g].astype(BF16)
            dy = dy_ref[:, cols]
            dsc_ref[:, cols] = jnp.sum(dy * _dot(db, wb), axis=0, keepdims=True)
            dyw = (dy * sc_ref[:, cols]).astype(BF16)
            dw_ref[g] = _dot(db, dyw, "tn")
            dd = _dot(dyw, wb, "nt")
            rsum, span = dd / cnt, 1
            while span < window:
                rsum = rsum + jnp.where(t_idx < t - span, pltpu.roll(rsum, t - span, 0), 0.0)
                span *= 2
            dx_ref[:, cols] = (rsum - dd).astype(BF16)

    full = lambda shape: BS(shape, lambda i: (0,) * len(shape))
    return pl.pallas_call(
        body, out_shape=(SDS((t, width), BF16), SDS((N_GROUPS, HEAD, HEAD), F32), SDS((1, width), F32)), grid=(1,),
        in_specs=[BS((t, width), lambda i: (0, 3)), _group_cols(OFF_XP)(t), full((N_GROUPS, HEAD, HEAD)), full((1, width))],
        out_specs=(full((t, width)), full((N_GROUPS, HEAD, HEAD)), full((1, width))), name="pool_bwd",
        compiler_params=_params("arbitrary"))(dmix, proj, w_pool, scale)


def _ffn_fwd(xn, wg, wu):
    t, d = xn.shape
    fs = wg.shape[1]
    tm = _tile(t, 512)

    def body(x_ref, wg_ref, wu_ref, a_ref, b_ref, hh_ref):
        x = x_ref[...]
        a = _dot(x, wg_ref[0], "nt")
        b = _dot(x, wu_ref[0], "nt")
        a_ref[0] = a
        b_ref[0] = b
        hh_ref[0] = (a * _sigmoid(a) * b).astype(BF16)

    w_spec = BS((1, fs, d), lambda j, i: (j, 0, 0))
    o_spec = BS((1, tm, fs), lambda j, i: (j, i, 0))
    return pl.pallas_call(body, out_shape=(SDS((N_DEV, t, fs), F32), SDS((N_DEV, t, fs), F32), SDS((N_DEV, t, fs), BF16)),
                          grid=(N_DEV, t // tm), in_specs=[BS((tm, d), lambda j, i: (i, 0)), w_spec, w_spec],
                          out_specs=(o_spec, o_spec, o_spec), name="ffn_fwd",
                          compiler_params=_params("parallel", "parallel"))(xn, wg, wu)


def _ffn_down(hh, wd, res):
    _, t, fs = hh.shape
    d = wd.shape[2]
    tm, tn = _tile(t, 1024), _tile(d, 1024)

    def body(a_ref, b_ref, r_ref, o_ref, acc_ref):
        k = pl.program_id(2)

        @pl.when(k == 0)
        def _():
            acc_ref[...] = r_ref[...]

        acc_ref[...] += _dot(a_ref[0], b_ref[0])

        @pl.when(k == N_DEV - 1)
        def _():
            o_ref[...] = acc_ref[...]

    o_spec = BS((tm, tn), lambda i, j, k: (i, j))
    return pl.pallas_call(body, out_shape=SDS((t, d), F32), grid=(t // tm, d // tn, N_DEV),
                          in_specs=[BS((1, tm, fs), lambda i, j, k: (k, i, 0)), BS((1, fs, tn), lambda i, j, k: (k, 0, j)), o_spec],
                          out_specs=o_spec, scratch_shapes=[pltpu.VMEM((tm, tn), F32)], name="ffn_down",
                          compiler_params=_params("parallel", "parallel", "arbitrary"))(hh, wd, res)


def _ffn_bwd_hidden(dh_bf, wd, a, b):
    t, d = dh_bf.shape
    fs = wd.shape[1]
    tm = _tile(t, 512)

    def body(dh_ref, wd_ref, a_ref, b_ref, da_ref, db_ref):
        dhh = _dot(dh_ref[...], wd_ref[0], "nt")
        av = a_ref[0]
        sig = _sigmoid(av)
        da_ref[0] = (dhh * b_ref[0] * sig * (1.0 + av * (1.0 - sig))).astype(BF16)
        db_ref[0] = (dhh * av * sig).astype(BF16)

    o_spec = BS((1, tm, fs), lambda j, i: (j, i, 0))
    return pl.pallas_call(body, out_shape=(SDS((N_DEV, t, fs), BF16), SDS((N_DEV, t, fs), BF16)), grid=(N_DEV, t // tm),
                          in_specs=[BS((tm, d), lambda j, i: (i, 0)), BS((1, fs, d), lambda j, i: (j, 0, 0)), o_spec, o_spec],
                          out_specs=(o_spec, o_spec), name="ffn_bwd_hidden",
                          compiler_params=_params("parallel", "parallel"))(dh_bf, wd, a, b)


def _ffn_dwd(hh, dh_bf):
    _, t, fs = hh.shape
    d = dh_bf.shape[1]
    tn = _tile(d, 1024)

    def body(a_ref, b_ref, o_ref):
        o_ref[0] = _dot(a_ref[0], b_ref[...], "tn").astype(BF16)

    return pl.pallas_call(body, out_shape=SDS((N_DEV, fs, d), BF16), grid=(N_DEV, d // tn),
                          in_specs=[BS((1, t, fs), lambda j, n: (j, 0, 0)), BS((t, tn), lambda j, n: (0, n))],
                          out_specs=BS((1, fs, tn), lambda j, n: (j, 0, n)), name="ffn_dwd",
                          compiler_params=_params("parallel", "parallel"))(hh, dh_bf)


def _ffn_dwgu(xn, da, db):
    t, d = xn.shape
    fs = da.shape[2]
    tn = _tile(d, 1024)

    def body(x_ref, da_ref, db_ref, dg_ref, du_ref):
        x = x_ref[...]
        dg_ref[0] = _dot(da_ref[0], x, "tn").astype(BF16)
        du_ref[0] = _dot(db_ref[0], x, "tn").astype(BF16)

    g_spec = BS((1, t, fs), lambda j, i: (j, 0, 0))
    o_spec = BS((1, fs, tn), lambda j, i: (j, 0, i))
    return pl.pallas_call(body, out_shape=(SDS((N_DEV, fs, d), BF16), SDS((N_DEV, fs, d), BF16)), grid=(N_DEV, d // tn),
                          in_specs=[BS((t, tn), lambda j, i: (0, i)), g_spec, g_spec], out_specs=(o_spec, o_spec),
                          name="ffn_dwgu", compiler_params=_params("parallel", "parallel"))(xn, da, db)


def _ffn_dxn(da, db, wg, wu):
    _, t, fs = da.shape
    d = wg.shape[2]
    tm, tn = _tile(t, 1024), _tile(d, 1024)

    def body(da_ref, db_ref, wg_ref, wu_ref, o_ref, acc_ref):
        k = pl.program_id(2)

        @pl.when(k == 0)
        def _():
            acc_ref[...] = jnp.zeros_like(acc_ref)

        acc_ref[...] += _dot(da_ref[0], wg_ref[0]) + _dot(db_ref[0], wu_ref[0])

        @pl.when(k == N_DEV - 1)
        def _():
            o_ref[...] = acc_ref[...]

    g_spec = BS((1, tm, fs), lambda i, j, k: (k, i, 0))
    w_spec = BS((1, fs, tn), lambda i, j, k: (k, 0, j))
    return pl.pallas_call(body, out_shape=SDS((t, d), F32), grid=(t // tm, d // tn, N_DEV),
                          in_specs=[g_spec, g_spec, w_spec, w_spec], out_specs=BS((tm, tn), lambda i, j, k: (i, j)),
                          scratch_shapes=[pltpu.VMEM((tm, tn), F32)], name="ffn_dxn",
                          compiler_params=_params("parallel", "parallel", "arbitrary"))(da, db, wg, wu)


def _ple_fwd(xn, wpg, p_bf, wpp, h):
    t, d = xn.shape
    dp = p_bf.shape[1]
    tn = wpp.shape[2]
    tm = _tile(t, 1024)

    def body(x_ref, wg_ref, p_ref, wp_ref, h_ref, o_ref, z_ref, pp_ref):
        z = _dot(x_ref[...], wg_ref[...])
        pp = _dot(p_ref[...], wp_ref[0])
        z_ref[...] = z
        pp_ref[...] = pp
        o_ref[...] = h_ref[...] + pp * _sigmoid(z)

    o_spec = BS((tm, tn), lambda i, j: (i, j))
    out = SDS((t, d), F32)
    return pl.pallas_call(body, out_shape=(out, out, out), grid=(t // tm, N_DEV),
                          in_specs=[BS((tm, d), lambda i, j: (i, 0)), BS((d, tn), lambda i, j: (0, j)),
                                    BS((tm, dp), lambda i, j: (i, 0)), BS((1, dp, tn), lambda i, j: (j, 0, 0)), o_spec],
                          out_specs=(o_spec, o_spec, o_spec), name="ple_fwd",
                          compiler_params=_params("parallel", "parallel"))(xn, wpg, p_bf, wpp, h)


def _ple_bwd_gate(dh, z, pp, after):
    t, d = dh.shape
    tm = _tile(t, 256)

    def body(dh_ref, z_ref, pp_ref, after_ref, dpp_ref, dz_ref):
        g = _sigmoid(z_ref[...])
        dh_v = dh_ref[...]
        dpp_ref[...] = (dh_v * g).astype(BF16)
        dz_ref[...] = (dh_v * pp_ref[...] * g * (1.0 - g)).astype(BF16)

    row = BS((tm, d), lambda i: (i, 0))
    return pl.pallas_call(body, out_shape=(SDS((t, d), BF16), SDS((t, d), BF16)), grid=(t // tm,),
                          in_specs=[row, row, row, BS(after.shape, lambda i: (0, 0))], out_specs=(row, row),
                          name="ple_bwd_gate", compiler_params=_params("parallel"))(dh, z, pp, after)


def _ple_dwpp(p_bf, dpp):
    t, dp = p_bf.shape
    tn = dpp.shape[1] // N_DEV

    def body(p_ref, g_ref, o_ref):
        o_ref[0] = _dot(p_ref[...], g_ref[...], "tn").astype(BF16)

    return pl.pallas_call(body, out_shape=SDS((N_DEV, dp, tn), BF16), grid=(N_DEV,),
                          in_specs=[BS((t, dp), lambda j: (0, 0)), BS((t, tn), lambda j: (0, j))],
                          out_specs=BS((1, dp, tn), lambda j: (j, 0, 0)), name="ple_dwpp",
                          compiler_params=_params("parallel"))(p_bf, dpp)


def _loss_call(y, target):
    t, d = y.shape
    tm = _tile(t, 256)

    def body(y_ref, t_ref, dy_ref, loss_ref):
        diff = y_ref[...] - t_ref[...]
        dy_ref[...] = diff * (1.0 / d)

        @pl.when(pl.program_id(0) == 0)
        def _():
            loss_ref[...] = jnp.zeros_like(loss_ref)

        loss_ref[...] += 0.5 * jnp.sum(jnp.mean(diff * diff, axis=-1, keepdims=True), axis=0, keepdims=True)

    row = BS((tm, d), lambda i: (i, 0))
    return pl.pallas_call(body, out_shape=(SDS((t, d), F32), SDS((1, 1), F32)), grid=(t // tm,), in_specs=[row, row],
                          out_specs=(row, BS((1, 1), lambda i: (0, 0))), name="loss",
                          compiler_params=_params("arbitrary"))(y, target)


def _mesh_pos():
    return lax.axis_index("x"), lax.axis_index("y"), lax.axis_index("c")


def _dev_index(px, py, pc):
    return 4 * px + 2 * py + pc


DMA_CHUNK_BYTES = 256 * 1024


def _row_chunks(ref):
    r, c = ref.shape
    rows = max(16, DMA_CHUNK_BYTES // (c * jnp.dtype(ref.dtype).itemsize) // 16 * 16)
    return [(s, min(rows, r - s)) for s in range(0, r, rows)]


HBM = pl.BlockSpec(memory_space=pltpu.HBM)
SEM = pl.BlockSpec(memory_space=pltpu.SEMAPHORE)
DATAFLOW = pltpu.SideEffectType.DATAFLOW_SIDE_EFFECTING


def _in_hbm(arrs):
    return [pltpu.with_memory_space_constraint(a, pltpu.HBM) for a in arrs]


def _other_chips(x, y):
    return [(1 - x, y), (x, 1 - y), (1 - x, 1 - y)]


def _split_start(name, srcs, lands, n_sems, copies):
    n = len(srcs)

    def body(*refs):
        for cp in copies(refs[:n], refs[n:2 * n], refs[2 * n], refs[2 * n + 1]):
            cp.start()
        token = refs[-1]
        token[...] = jnp.zeros_like(token)

    thru = [pltpu.HBM(a.shape, a.dtype) for a in list(srcs) + list(lands)]
    outs = pl.pallas_call(
        body, name=name,
        out_shape=(pltpu.SemaphoreType.DMA((n * n_sems,)), pltpu.SemaphoreType.DMA((n * n_sems,)), *thru, SDS((8, HEAD), F32)),
        in_specs=[HBM] * (2 * n), out_specs=(SEM, SEM, *([HBM] * (2 * n)), pl.BlockSpec(memory_space=pltpu.VMEM)),
        input_output_aliases={q: 2 + q for q in range(2 * n)},
        compiler_params=pltpu.CompilerParams(has_side_effects=DATAFLOW))(*_in_hbm(list(srcs) + list(lands)))
    return outs[0], outs[1], list(outs[2:2 + n]), list(outs[2 + n:2 + 2 * n]), outs[-1]


def _split_wait(name, srcs, lands, send_sems, recv_sems, after, copies):
    n = len(srcs)

    def body(*refs):
        for cp in copies(refs[:n], refs[n:2 * n], refs[2 * n], refs[2 * n + 1]):
            cp.wait_send()
            cp.wait_recv()

    thru = [pltpu.HBM(a.shape, a.dtype) for a in list(srcs) + list(lands)]
    outs = pl.pallas_call(
        body, name=name, out_shape=tuple(thru), in_specs=[HBM] * (2 * n) + [SEM, SEM, ANY], out_specs=tuple([HBM] * (2 * n)),
        input_output_aliases={q: q for q in range(2 * n)},
        compiler_params=pltpu.CompilerParams(has_side_effects=DATAFLOW))(*list(srcs), *list(lands), send_sems, recv_sems, after)
    return list(outs[:n]), list(outs[n:])


def _gather_ici_copies(layer, waiting):
    def copies(src_refs, land_refs, send_sems, recv_sems):
        x, y, c = _mesh_pos()
        out = []
        for a in range(len(src_refs)):
            for j, chip in enumerate(_other_chips(x, y)):
                slot = _dev_index(*chip, c) if waiting else _dev_index(x, y, c)
                out.append(pltpu.make_async_remote_copy(
                    src_ref=src_refs[a].at[layer], dst_ref=land_refs[a].at[slot], send_sem=send_sems.at[3 * a + j],
                    recv_sem=recv_sems.at[3 * a + j], device_id=(*chip, c), device_id_type=pl.DeviceIdType.MESH))
        return out
    return copies


def _place_own(me, src, layer, land):
    _, r, c = land.shape
    tr = next(cand for cand in (512, 256, 176, 128, 64, 16) if r % cand == 0)

    def body(me_ref, s_ref, l_ref, o_ref):
        o_ref[...] = s_ref[...]

    grid_spec = pltpu.PrefetchScalarGridSpec(
        num_scalar_prefetch=1, grid=(r // tr,), in_specs=[BS((1, tr, c), lambda i, me: (layer, i, 0)), ANY],
        out_specs=BS((1, tr, c), lambda i, me: (me[0], i, 0)))
    return pl.pallas_call(body, grid_spec=grid_spec, out_shape=SDS(land.shape, land.dtype), input_output_aliases={2: 0},
                          name="place_own", compiler_params=_params("parallel"))(me, src, land)


def _gather_d2d(name, srcs, layer, lands):
    n = len(srcs)

    def body(*refs):
        src_refs, land_refs = refs[:n], refs[n:2 * n]
        send_sems, recv_sems = refs[3 * n:]
        x, y, c = _mesh_pos()
        sibling = (x, y, 1 - c)
        blocks = [(x, y)] + _other_chips(x, y)

        def copy(a, k, waiting, rows=None):
            pc = 1 - c if waiting else c
            slot = land_refs[a].at[_dev_index(*blocks[k], pc)]
            src = src_refs[a].at[layer] if (k == 0 and not waiting) else slot
            if rows is not None:
                src, slot = src.at[pl.ds(*rows)], slot.at[pl.ds(*rows)]
            return pltpu.make_async_remote_copy(src_ref=src, dst_ref=slot, send_sem=send_sems.at[a, k],
                                                recv_sem=recv_sems.at[a, k], device_id=sibling,
                                                device_id_type=pl.DeviceIdType.MESH)

        for a in range(n):
            for rows in _row_chunks(src_refs[a].at[layer]):
                for k in range(4):
                    copy(a, k, False, rows).start()
        for a in range(n):
            for k in range(4):
                copy(a, k, True).wait_recv()
        for a in range(n):
            for k in range(4):
                copy(a, k, False).wait_send()

    outs = pl.pallas_call(
        body, name=name, out_shape=tuple(SDS(l.shape, l.dtype) for l in lands), in_specs=[ANY] * (2 * n),
        out_specs=tuple([ANY] * n), input_output_aliases={n + q: q for q in range(n)},
        scratch_shapes=[pltpu.SemaphoreType.DMA((n, 4)), pltpu.SemaphoreType.DMA((n, 4))],
        compiler_params=pltpu.CompilerParams(has_side_effects=True))(*srcs, *lands)
    return list(outs)


def _sibling_copies(src_refs, land_refs, send_sems, recv_sems):
    x, y, c = _mesh_pos()
    out = []
    for a in range(len(src_refs)):
        whole = len(src_refs[a].shape) == 2
        for q in range(1 if whole else 4):
            src = src_refs[a] if whole else src_refs[a].at[2 * q + (1 - c)]
            dst = land_refs[a] if whole else land_refs[a].at[q]
            out.append(pltpu.make_async_remote_copy(
                src_ref=src, dst_ref=dst, send_sem=send_sems.at[4 * a + q], recv_sem=recv_sems.at[4 * a + q],
                device_id=(x, y, 1 - c), device_id_type=pl.DeviceIdType.MESH))
    return out


def _chip_sums(pos, own, got):
    whole = own.ndim == 2
    r, c = own.shape[-2:]
    tr = next(cand for cand in (256, 176, 128, 64, 16, 8) if r % cand == 0)

    def body(pos_ref, own_ref, got_ref, o_ref):
        mine = own_ref[...] if whole else own_ref[0]
        o_ref[...] = (mine.astype(F32) + got_ref[...].astype(F32)).astype(o_ref.dtype)

    if whole:
        grid = (r // tr,)
        blk = BS((tr, c), lambda i, pos: (i, 0))
        specs, o_spec, ins, out = [blk, blk], blk, [own, got], SDS((r, c), own.dtype)
    else:
        grid = (4, r // tr)
        blk = BS((1, tr, c), lambda q, i, pos: (q, i, 0))
        specs = [BS((1, 1, tr, c), lambda q, i, pos: (q, pos[2], i, 0)), blk]
        o_spec, ins, out = blk, [own.reshape(4, 2, r, c), got], SDS((4, r, c), own.dtype)
    grid_spec = pltpu.PrefetchScalarGridSpec(num_scalar_prefetch=1, grid=grid, in_specs=specs, out_specs=o_spec)
    return pl.pallas_call(body, out_shape=out, grid_spec=grid_spec, name="chip_sums",
                          compiler_params=_params(*["parallel"] * len(grid)))(pos, *ins)


def _chip_copies(waiting):
    def copies(src_refs, land_refs, send_sems, recv_sems):
        x, y, c = _mesh_pos()
        out = []
        for a in range(len(src_refs)):
            whole = len(src_refs[a].shape) == 2
            for j, (qx, qy) in enumerate(_other_chips(x, y)):
                src = src_refs[a] if whole else src_refs[a].at[2 * qx + qy]
                slot = 2 * qx + qy if waiting else 2 * x + y
                out.append(pltpu.make_async_remote_copy(
                    src_ref=src, dst_ref=land_refs[a].at[slot], send_sem=send_sems.at[3 * a + j],
                    recv_sem=recv_sems.at[3 * a + j], device_id=(qx, qy, c), device_id_type=pl.DeviceIdType.MESH))
        return out
    return copies


def _adamw_math(w, g, m, v):
    m = ADAM_B1 * m + (1.0 - ADAM_B1) * g
    v = ADAM_B2 * v + (1.0 - ADAM_B2) * (g * g)
    m_hat = m / (1.0 - ADAM_B1 ** ADAM_STEP)
    v_hat = v / (1.0 - ADAM_B2 ** ADAM_STEP)
    delta = -ADAM_LR * (m_hat / (jnp.sqrt(v_hat) + ADAM_EPS) + ADAM_WD * w)
    return delta, m, v


N_CHIPS = 4


def _sum_partials(my_chip, own, parts_ref):
    g = None
    for q in range(N_CHIPS):
        term = jnp.where(my_chip == q, own, parts_ref[q].astype(F32))
        g = term if g is None else g + term
    return g


def _adamw_layer(name, pos, parts, own, w, m, v, layer, prev):
    depth, r, c = w.shape
    tr = next(cand for cand in (256, 128, 64, 32, 16, 8) if r % cand == 0)

    def body(pos_ref, parts_ref, own_ref, w_ref, m_ref, v_ref, *rest):
        g_ref, d_ref, nm_ref, nv_ref = rest[-4:]
        g = _sum_partials(pos_ref[1], own_ref[0].astype(F32), parts_ref)
        delta, nm, nv = _adamw_math(w_ref[0], g, m_ref[0], v_ref[0])
        g_ref[0] = g
        d_ref[0] = delta
        nm_ref[0] = nm
        nv_ref[0] = nv

    lay = BS((1, tr, c), lambda i, pos: (layer, i, 0))
    stacked = SDS((depth, r, c), F32)
    ins = [pos, parts, own, w, m, v]
    specs = [BS((N_CHIPS, tr, c), lambda i, pos: (0, i, 0)), BS((1, tr, c), lambda i, pos: (pos[1], i, 0)), lay, lay, lay]
    aliases = {}
    if prev is not None:
        ins += list(prev)
        specs += [ANY] * 4
        aliases = {6 + q: q for q in range(4)}
    grid_spec = pltpu.PrefetchScalarGridSpec(num_scalar_prefetch=1, grid=(r // tr,), in_specs=specs, out_specs=(lay,) * 4)
    return pl.pallas_call(body, out_shape=(stacked,) * 4, grid_spec=grid_spec, input_output_aliases=aliases, name=name,
                          compiler_params=_params("parallel"))(*ins)


def _adamw_small(pos, parts, own, w, m, v):
    r, c = w.shape
    tr = _tile(r, 256)

    def body(pos_ref, parts_ref, own_ref, w_ref, m_ref, v_ref, g_ref, d_ref, nm_ref, nv_ref):
        g = _sum_partials(pos_ref[1], own_ref[...], parts_ref)
        delta, nm, nv = _adamw_math(w_ref[...], g, m_ref[...], v_ref[...])
        g_ref[...] = g
        d_ref[...] = delta
        nm_ref[...] = nm
        nv_ref[...] = nv

    row = BS((tr, c), lambda i, pos: (i, 0))
    out = SDS((r, c), F32)
    grid_spec = pltpu.PrefetchScalarGridSpec(
        num_scalar_prefetch=1, grid=(r // tr,), in_specs=[BS((N_CHIPS, tr, c), lambda i, pos: (0, i, 0)), row, row, row, row],
        out_specs=(row,) * 4)
    return pl.pallas_call(body, out_shape=(out,) * 4, grid_spec=grid_spec, name="adamw_small",
                          compiler_params=_params("parallel"))(pos, parts, own, w, m, v)


def _pad_w_in(gathered):
    _, d, _ = gathered.shape
    w = jnp.transpose(gathered, (1, 0, 2)).reshape(d, PROJ_RAW)
    real_f = OFF_F + F_COLS
    return jnp.concatenate([w[:, :real_f], jnp.zeros((d, OFF_GU - real_f), w.dtype), w[:, real_f:]], axis=1)


def _unpad_dw_in(dw):
    d = dw.shape[0]
    real_f = OFF_F + F_COLS
    w = jnp.concatenate([dw[:, :real_f], dw[:, OFF_GU:]], axis=1)
    return jnp.transpose(w.reshape(d, N_DEV, PROJ_RAW // N_DEV), (1, 0, 2))


def _pack_small(tree):
    flat = jnp.concatenate([tree[n].reshape(-1) for n in SMALL])
    rows = -(-flat.shape[0] // (256 * HEAD)) * 256
    return jnp.pad(flat, (0, rows * HEAD - flat.shape[0])).reshape(rows, HEAD)


def _unpack_small(packed, like):
    flat = packed.reshape(-1)
    out, off = {}, 0
    for n in SMALL:
        size = like[n].size
        out[n] = flat[off:off + size].reshape(like[n].shape)
        off += size
    return out


def _layer_fwd(h0, p_bf, sw, gw):
    t, d = h0.shape
    xn1 = _rms_fwd("rms_fwd", h0, sw["norm_mix"])
    proj = _matmul("proj_fwd", xn1, gw["w_in"], "nn", F32, t, 512)
    c = _fgate_fwd(proj, sw["forget_bias"])
    cq = c[:, :N_HEADS].T.reshape(N_HEADS, t, 1)
    ck = cq.reshape(N_HEADS, 1, t)
    y_attn, lse = _attn_fwd(proj, cq, ck, sw["q_norm"], sw["k_norm"])
    y_gmlp = _gmlp_fwd(proj, sw["gmlp_v_norm"], sw["gmlp_w_s"], sw["gmlp_b_s"])
    y_pool = _pool_fwd(proj, sw["pool_w"], sw["pool_scale"])
    mix = jnp.concatenate([y_attn, y_gmlp, y_pool], axis=1)
    h1 = _matmul("out_fwd", mix, gw["w_out"], "nn", F32, t, 512, res=h0)
    xn2 = _rms_fwd("rms_fwd", h1, sw["norm_ffn"])
    a, b, hh = _ffn_fwd(xn2, gw["w_ffn_gate"], gw["w_ffn_up"])
    h2 = _ffn_down(hh, gw["w_ffn_down"], h1)
    xn3 = _rms_fwd("rms_fwd", h2, sw["norm_ple"])
    h3, z, pp = _ple_fwd(xn3, gw["w_ple_gate"], p_bf, gw["w_ple_proj"], h2)
    saved = dict(h0=h0, xn1=xn1, proj=proj, cq=cq, ck=ck, lse=lse, mix=mix, h1=h1, xn2=xn2, a=a, b=b, hh=hh, h2=h2,
                 xn3=xn3, z=z, pp=pp)
    return h3, saved


FFN_SIDE = ("w_ple_proj", "w_ple_gate", "w_ffn_down", "w_ffn_gate", "w_ffn_up")
MIX_SIDE = ("w_out", "w_in")


def _layer_bwd_ffn(dh3, p_bf, sw, gw, s, hook, after):
    t, d = dh3.shape
    big, small = {}, {}
    dpp, dz = _ple_bwd_gate(dh3, s["z"], s["pp"], after)
    big["w_ple_proj"] = _ple_dwpp(p_bf, dpp)
    big["w_ple_gate"] = _matmul("dw_tn", s["xn3"], dz, "tn", BF16, d, 512).reshape(N_DEV, d // N_DEV, d)
    dxn3 = _matmul("dx_nt", dz, gw["w_ple_gate"], "nt", F32, t, 512)
    dh2, dh2_bf, small["norm_ple"] = _rms_bwd_call("rms_bwd", dxn3, s["h2"], sw["norm_ple"] + hook(dxn3)[0, 0], dh3)
    da, db = _ffn_bwd_hidden(dh2_bf, gw["w_ffn_down"], s["a"], s["b"])
    big["w_ffn_down"] = _ffn_dwd(s["hh"], dh2_bf)
    big["w_ffn_gate"], big["w_ffn_up"] = _ffn_dwgu(s["xn2"], da, db)
    dxn2 = _ffn_dxn(da, db, gw["w_ffn_gate"], gw["w_ffn_up"])
    dh1, dh1_bf, small["norm_ffn"] = _rms_bwd_call("rms_bwd", dxn2, s["h1"], sw["norm_ffn"], dh2)
    return (dh1, dh1_bf), big, small


def _layer_bwd_mix(dh1, dh1_bf, sw, gw, s, hook, after):
    t, d = dh1.shape
    big, small = {}, {}
    dmix = _matmul("dx_nt", dh1_bf, gw["w_out"], "nt", F32, t, 512, after=after)
    big["w_out"] = _matmul("dw_tn", s["mix"], dh1_bf, "tn", BF16, d, 512).reshape(N_DEV, d // N_DEV, d)
    proj = s["proj"]
    dxp, small["pool_w"], small["pool_scale"] = _pool_bwd(dmix, proj, sw["pool_w"], sw["pool_scale"] + hook(dmix)[0, 0])
    dgu, dgv, small["gmlp_v_norm"], small["gmlp_w_s"], small["gmlp_b_s"] = _gmlp_bwd(
        dmix, proj, sw["gmlp_v_norm"], sw["gmlp_w_s"], sw["gmlp_b_s"])
    dq, dk, dv, dcq, dck, small["q_norm"], small["k_norm"] = _attn_bwd(
        proj, s["cq"], s["ck"], s["lse"], dmix, sw["q_norm"], sw["k_norm"])
    dc = (dcq.reshape(N_HEADS, t) + dck.reshape(N_HEADS, t)).T
    dc = jnp.pad(dc, ((0, 0), (0, HEAD - N_HEADS)))
    df, small["forget_bias"] = _fgate_bwd(dc, proj, sw["forget_bias"])
    dproj = jnp.concatenate([dq, dk, dv, df, dgu, dgv, dxp], axis=1)
    big["w_in"] = _unpad_dw_in(_matmul("dw_in_tn", s["xn1"], dproj, "tn", BF16, d, 512))
    dxn1 = _matmul("dx_in_nt", dproj, gw["w_in"], "nt", F32, _tile(t, 512), 512)
    dh0, _, small["norm_mix"] = _rms_bwd_call("rms_bwd", dxn1, s["h0"], sw["norm_mix"], dh1)
    return dh0, big, small


def _small_kernel_shapes(sm, i):
    row = lambda a: a[i].reshape(1, -1)
    return dict(
        norm_mix=row(sm["norm_mix"]), norm_ffn=row(sm["norm_ffn"]), norm_ple=row(sm["norm_ple"]),
        q_norm=row(sm["q_norm"]), k_norm=row(sm["k_norm"]),
        forget_bias=jnp.pad(row(sm["forget_bias"]), ((0, 0), (0, HEAD - F_COLS))),
        gmlp_v_norm=row(sm["gmlp_v_norm"]), gmlp_w_s=sm["gmlp_w_s"][i], gmlp_b_s=sm["gmlp_b_s"][i].reshape(N_GROUPS, HEAD, 1),
        pool_w=sm["pool_w"][i], pool_scale=row(sm["pool_scale"]))


def _small_grad_shapes(g, like):
    out = {}
    for n in SMALL:
        v = g[n]
        if n == "forget_bias":
            v = v[:, :F_COLS]
        out[n] = v.reshape(like[n].shape[1:])
    return out


def kernel(x, p, norm_mix, w_in, q_norm, k_norm, forget_bias, gmlp_v_norm, gmlp_w_s, gmlp_b_s, pool_w, pool_scale, w_out, norm_ffn, w_ffn_gate, w_ffn_up, w_ffn_down, norm_ple, w_ple_gate, w_ple_proj, loss_target, m_norm_mix, m_w_in, m_q_norm, m_k_norm, m_forget_bias, m_gmlp_v_norm, m_gmlp_w_s, m_gmlp_b_s, m_pool_w, m_pool_scale, m_w_out, m_norm_ffn, m_w_ffn_gate, m_w_ffn_up, m_w_ffn_down, m_norm_ple, m_w_ple_gate, m_w_ple_proj, v_norm_mix, v_w_in, v_q_norm, v_k_norm, v_forget_bias, v_gmlp_v_norm, v_gmlp_w_s, v_gmlp_b_s, v_pool_w, v_pool_scale, v_w_out, v_norm_ffn, v_w_ffn_gate, v_w_ffn_up, v_w_ffn_down, v_norm_ple, v_w_ple_gate, v_w_ple_proj):
    w = dict(norm_mix=norm_mix, w_in=w_in, q_norm=q_norm, k_norm=k_norm, forget_bias=forget_bias, gmlp_v_norm=gmlp_v_norm,
             gmlp_w_s=gmlp_w_s, gmlp_b_s=gmlp_b_s, pool_w=pool_w, pool_scale=pool_scale, w_out=w_out, norm_ffn=norm_ffn,
             w_ffn_gate=w_ffn_gate, w_ffn_up=w_ffn_up, w_ffn_down=w_ffn_down, norm_ple=norm_ple, w_ple_gate=w_ple_gate,
             w_ple_proj=w_ple_proj)
    m = dict(norm_mix=m_norm_mix, w_in=m_w_in, q_norm=m_q_norm, k_norm=m_k_norm, forget_bias=m_forget_bias,
             gmlp_v_norm=m_gmlp_v_norm, gmlp_w_s=m_gmlp_w_s, gmlp_b_s=m_gmlp_b_s, pool_w=m_pool_w, pool_scale=m_pool_scale,
             w_out=m_w_out, norm_ffn=m_norm_ffn, w_ffn_gate=m_w_ffn_gate, w_ffn_up=m_w_ffn_up, w_ffn_down=m_w_ffn_down,
             norm_ple=m_norm_ple, w_ple_gate=m_w_ple_gate, w_ple_proj=m_w_ple_proj)
    v = dict(norm_mix=v_norm_mix, w_in=v_w_in, q_norm=v_q_norm, k_norm=v_k_norm, forget_bias=v_forget_bias,
             gmlp_v_norm=v_gmlp_v_norm, gmlp_w_s=v_gmlp_w_s, gmlp_b_s=v_gmlp_b_s, pool_w=v_pool_w, pool_scale=v_pool_scale,
             w_out=v_w_out, norm_ffn=v_norm_ffn, w_ffn_gate=v_w_ffn_gate, w_ffn_up=v_w_ffn_up, w_ffn_down=v_w_ffn_down,
             norm_ple=v_norm_ple, w_ple_gate=v_w_ple_gate, w_ple_proj=v_w_ple_proj)
    for tree in (w, m, v):
        for n in TRANSPOSED:
            tree[n] = jnp.transpose(tree[n], (0, 2, 1))
    depth = w_in.shape[0]
    t, d = x.shape[1], x.shape[2]
    h = x[0]
    mx, my, mc = _mesh_pos()
    pos = jnp.stack([_dev_index(mx, my, mc), 2 * mx + my, mc]).astype(jnp.int32)
    p_bf = p[:, 0].astype(BF16)
    w_bf = {n: w[n].astype(BF16) for n in BIG}

    srcs = [w_bf[n] for n in BIG]

    def landing_zones():
        return [lax.empty((N_DEV, *a.shape[1:]), a.dtype) for a in srcs]

    def gather_start(i, srcs):
        return _split_start(f"gather_start_l{i}", srcs, landing_zones(), 3, _gather_ici_copies(i, False))

    gathered, saved = [], []
    send_sems, recv_sems, srcs, lands, _ = gather_start(0, srcs)
    for i in range(depth):
        srcs, lands = _split_wait(f"gather_wait_l{i}", srcs, lands, send_sems, recv_sems, h, _gather_ici_copies(i, True))
        sw = _small_kernel_shapes(w, i)
        if i + 1 < depth:
            send_sems, recv_sems, srcs, next_lands, token = gather_start(i + 1, srcs)
            sw["norm_mix"] = sw["norm_mix"] + token[0, 0]
        lands = _gather_d2d(f"gather_d2d_l{i}", srcs, i, lands)
        gw = {n: _place_own(pos, src, i, land) for n, src, land in zip(BIG, srcs, lands)}
        gw["w_in"] = _pad_w_in(gw["w_in"])
        gw["w_out"] = gw["w_out"].reshape(d, d)
        gw["w_ple_gate"] = gw["w_ple_gate"].reshape(d, d)
        h, s = _layer_fwd(h, p_bf[i], sw, gw)
        gathered.append((sw, gw))
        saved.append(s)
        if i + 1 < depth:
            lands = next_lands

    dh, loss_part = _loss_call(h, loss_target[0])
    loss = lax.psum(loss_part[0, 0], ("x", "y", "c"))

    small_grads = [None] * depth
    stacked = {n: None for n in BIG}
    parts = [dict() for _ in range(depth)]
    on_d2d, on_ici = [], [[] for _ in range(depth)]

    def round_start(tag, layer, names, grads):
        lands = [lax.empty(g.shape if g.ndim == 2 else (N_CHIPS, *g.shape[1:]), g.dtype) for g in grads]
        send_sems, recv_sems, grads, lands, token = _split_start(f"sibling_start_{tag}", grads, lands, 4, _sibling_copies)
        on_d2d.append((tag, layer, names, grads, lands, send_sems, recv_sems))
        return token

    def hook(after):
        token = jnp.zeros((1, 1), F32)
        while on_d2d:
            tag, layer, names, grads, lands, send_sems, recv_sems = on_d2d.pop(0)
            grads, got = _split_wait(f"sibling_wait_{tag}", grads, lands, send_sems, recv_sems, after, _sibling_copies)
            sums = [_chip_sums(pos, g, r) for g, r in zip(grads, got)]
            lands = [lax.empty((N_CHIPS, *q.shape[-2:]), q.dtype) for q in sums]
            send_sems, recv_sems, sums, lands, token = _split_start(f"chip_start_{tag}", sums, lands, 3, _chip_copies(False))
            on_ici[layer].append((tag, names, sums, lands, send_sems, recv_sems))
        return token

    def finish(layer, after):
        for tag, names, sums, lands, send_sems, recv_sems in on_ici[layer]:
            sums, lands = _split_wait(f"chip_wait_{tag}", sums, lands, send_sems, recv_sems, after, _chip_copies(True))
            parts[layer].update(zip(names, zip(lands, sums)))
        for n in BIG:
            stacked[n] = _adamw_layer(f"adamw_{n}", pos, *parts[layer][n], w[n], m[n], v[n], layer, stacked[n])

    token = jnp.zeros((8, HEAD), F32) + loss * 0.0
    for i in reversed(range(depth)):
        sw, gw = gathered[i]
        (dh1, dh1_bf), big, small = _layer_bwd_ffn(dh, p_bf[i], sw, gw, saved[i], hook, token)
        token = round_start(f"ffn_l{i}", i, FFN_SIDE, [big[n] for n in FFN_SIDE])
        dh, big, small_mix = _layer_bwd_mix(dh1, dh1_bf, sw, gw, saved[i], hook, token)
        small_grads[i] = _small_grad_shapes({**small, **small_mix}, w)
        names, grads = list(MIX_SIDE), [big[n] for n in MIX_SIDE]
        if i == 0:
            g_small = {n: jnp.stack([small_grads[q][n] for q in range(depth)]) for n in SMALL}
            names.append("small")
            grads.append(_pack_small(g_small))
        token = round_start(f"mix_l{i}", i, names, grads)
        if i == 0:
            token = hook(token)
        if i + 1 < depth:
            finish(i + 1, token)
    finish(0, stacked[BIG[-1]][0] if depth > 1 else token)
    packed = _adamw_small(pos, *parts[0]["small"], _pack_small(w), _pack_small(m), _pack_small(v))
    small_out = [_unpack_small(q, w) for q in packed]

    results = []
    for q in range(4):
        big_out = {n: jnp.transpose(stacked[n][q], (0, 2, 1)) if n in TRANSPOSED else stacked[n][q] for n in BIG}
        results.append({**big_out, **small_out[q]})
    outs = [loss, dh[None]]
    for q in range(4):
        outs += [results[q][n] for n in WEIGHTS]
    return tuple(outs)
```

```python
import functools

import jax
import jax.numpy as jnp
from jax import lax
from jax.experimental import pallas as pl
from jax.experimental.pallas import tpu as pltpu

F32 = jnp.float32
BF16 = jnp.bfloat16
EPS = 1e-6
HEAD = 128
N_HEADS = 8
N_GROUPS = 4
POOL_WINDOWS = (2, 4, 8, 16)
N_DEV = 8
ATTN_Q_BLOCK = 256

OFF_Q, OFF_K, OFF_V, OFF_F, OFF_GU, OFF_GV, OFF_XP, PROJ_PAD = 0, 1024, 2048, 3072, 3584, 4096, 4608, 5120
F_COLS = 8
PROJ_RAW = 4616

ADAM_LR, ADAM_B1, ADAM_B2, ADAM_EPS, ADAM_WD, ADAM_STEP = 0.001, 0.9, 0.999, 1e-08, 0.01, 10

BIG = ("w_in", "w_out", "w_ffn_gate", "w_ffn_up", "w_ffn_down", "w_ple_gate", "w_ple_proj")
TRANSPOSED = ("w_ffn_gate", "w_ffn_up")
SMALL = ("norm_mix", "q_norm", "k_norm", "forget_bias", "gmlp_v_norm", "gmlp_w_s", "gmlp_b_s", "pool_w",
         "pool_scale", "norm_ffn", "norm_ple")
WEIGHTS = ("norm_mix", "w_in", "q_norm", "k_norm", "forget_bias", "gmlp_v_norm", "gmlp_w_s", "gmlp_b_s", "pool_w",
           "pool_scale", "w_out", "norm_ffn", "w_ffn_gate", "w_ffn_up", "w_ffn_down", "norm_ple", "w_ple_gate",
           "w_ple_proj")

VMEM_LIMIT = 56 * 1024 * 1024

BS = pl.BlockSpec
SDS = jax.ShapeDtypeStruct
ANY = pl.BlockSpec(memory_space=pl.ANY)


def _params(*sem):
    return pltpu.CompilerParams(dimension_semantics=sem, vmem_limit_bytes=VMEM_LIMIT)


def _dot(a, b, mode="nn", precision=None):
    ca, cb = {"nn": (1, 0), "nt": (1, 1), "tn": (0, 0)}[mode]
    return lax.dot_general(a, b, (((ca,), (cb,)), ((), ())), preferred_element_type=F32, precision=precision)


def _rms_scale(x):
    return lax.rsqrt(jnp.mean(x * x, axis=-1, keepdims=True) + EPS)


def _rms_bwd(g, xhat, r):
    return r * (g - xhat * jnp.mean(g * xhat, axis=-1, keepdims=True))


def _gelu(x, with_grad=False):
    k = 0.7978845608028654
    inner = k * (x + 0.044715 * x * x * x)
    t = jnp.tanh(inner)
    y = 0.5 * x * (1.0 + t)
    if not with_grad:
        return y
    dy = 0.5 * (1.0 + t) + 0.5 * x * (1.0 - t * t) * k * (1.0 + 3.0 * 0.044715 * x * x)
    return y, dy


def _sigmoid(x):
    return 1.0 / (1.0 + jnp.exp(-x))


def _tile(n, want):
    t = min(n, want)
    assert n % t == 0, (n, want)
    return t


def _matmul(name, a, b, mode, out_dtype, tm, tn, res=None, after=None):
    if mode == "nn":
        (m, k), n = a.shape, b.shape[1]
        a_spec, b_spec = BS((tm, k), lambda i, j: (i, 0)), BS((k, tn), lambda i, j: (0, j))
    elif mode == "nt":
        (m, k), n = a.shape, b.shape[0]
        a_spec, b_spec = BS((tm, k), lambda i, j: (i, 0)), BS((tn, k), lambda i, j: (j, 0))
    else:
        (k, m), n = a.shape, b.shape[1]
        a_spec, b_spec = BS((k, tm), lambda i, j: (0, i)), BS((k, tn), lambda i, j: (0, j))
    assert m % tm == 0 and n % tn == 0
    o_spec = BS((tm, tn), lambda i, j: (i, j))

    def body(a_ref, b_ref, *rest):
        o_ref = rest[-1]
        acc = _dot(a_ref[...], b_ref[...], mode)
        if res is not None:
            acc = acc + rest[0][...]
        o_ref[...] = acc.astype(out_dtype)

    ins, specs = [a, b], [a_spec, b_spec]
    if res is not None:
        ins.append(res)
        specs.append(o_spec)
    if after is not None:
        ins.append(after)
        specs.append(BS(after.shape, lambda i, j: (0, 0)))
    return pl.pallas_call(body, out_shape=SDS((m, n), out_dtype), grid=(m // tm, n // tn), in_specs=specs,
                          out_specs=o_spec, name=name, compiler_params=_params("parallel", "parallel"))(*ins)


def _rms_fwd(name, h, gain):
    t, d = h.shape
    tm = _tile(t, 256)

    def body(h_ref, g_ref, o_ref):
        x = h_ref[...]
        o_ref[...] = (x * _rms_scale(x) * g_ref[...]).astype(BF16)

    return pl.pallas_call(body, out_shape=SDS((t, d), BF16), grid=(t // tm,),
                          in_specs=[BS((tm, d), lambda i: (i, 0)), BS((1, d), lambda i: (0, 0))],
                          out_specs=BS((tm, d), lambda i: (i, 0)), name=name, compiler_params=_params("parallel"))(h, gain)


def _rms_bwd_call(name, dxn, h, gain, dres):
    t, d = h.shape
    tm = _tile(t, 256)

    def body(dxn_ref, h_ref, g_ref, dres_ref, dh_ref, dhb_ref, dg_ref):
        x = h_ref[...]
        r = _rms_scale(x)
        xhat = x * r
        dy = dxn_ref[...]
        dh = dres_ref[...] + _rms_bwd(dy * g_ref[...], xhat, r)
        dh_ref[...] = dh
        dhb_ref[...] = dh.astype(BF16)

        @pl.when(pl.program_id(0) == 0)
        def _():
            dg_ref[...] = jnp.zeros_like(dg_ref)

        dg_ref[...] += jnp.sum(dy * xhat, axis=0, keepdims=True)

    row = BS((tm, d), lambda i: (i, 0))
    vec = BS((1, d), lambda i: (0, 0))
    return pl.pallas_call(body, out_shape=(SDS((t, d), F32), SDS((t, d), BF16), SDS((1, d), F32)), grid=(t // tm,),
                          in_specs=[row, row, vec, row], out_specs=(row, row, vec), name=name,
                          compiler_params=_params("arbitrary"))(dxn, h, gain, dres)


def _fgate_fwd(proj, bias):
    t = proj.shape[0]
    nb = t // HEAD

    def body(f_ref, b_ref, c_ref):
        tri = (lax.broadcasted_iota(jnp.int32, (HEAD, HEAD), 0) >= lax.broadcasted_iota(jnp.int32, (HEAD, HEAD), 1)).astype(F32)
        carry = jnp.zeros((1, HEAD), F32)
        for n in range(nb):
            rows = slice(n * HEAD, (n + 1) * HEAD)
            x = f_ref[rows, :] + b_ref[...]
            lf = jnp.minimum(x, 0.0) - jnp.log(1.0 + jnp.exp(-jnp.abs(x)))
            cb = _dot(tri, lf, precision=lax.Precision.HIGHEST) + carry
            c_ref[rows, :] = cb
            carry = cb[HEAD - 1:HEAD, :]

    return pl.pallas_call(body, out_shape=SDS((t, HEAD), F32), grid=(1,),
                          in_specs=[BS((t, HEAD), lambda i: (0, OFF_F // HEAD)), BS((1, HEAD), lambda i: (0, 0))],
                          out_specs=BS((t, HEAD), lambda i: (0, 0)), name="fgate_fwd",
                          compiler_params=_params("arbitrary"))(proj, bias)


def _fgate_bwd(dc, proj, bias):
    t = proj.shape[0]
    nb = t // HEAD
    width = OFF_GU - OFF_F

    def body(dc_ref, f_ref, b_ref, df_ref, db_ref):
        tri = (lax.broadcasted_iota(jnp.int32, (HEAD, HEAD), 0) <= lax.broadcasted_iota(jnp.int32, (HEAD, HEAD), 1)).astype(F32)
        carry = jnp.zeros((1, HEAD), F32)
        db = jnp.zeros((1, HEAD), F32)
        df_ref[:, HEAD:] = jnp.zeros((t, width - HEAD), BF16)
        for n in reversed(range(nb)):
            rows = slice(n * HEAD, (n + 1) * HEAD)
            dlf = _dot(tri, dc_ref[rows, :], precision=lax.Precision.HIGHEST) + carry
            carry = dlf[0:1, :]
            x = f_ref[rows, :] + b_ref[...]
            df = dlf * _sigmoid(-x)
            df_ref[rows, 0:HEAD] = df.astype(BF16)
            db = db + jnp.sum(df, axis=0, keepdims=True)
        db_ref[...] = db

    return pl.pallas_call(body, out_shape=(SDS((t, width), BF16), SDS((1, HEAD), F32)), grid=(1,),
                          in_specs=[BS((t, HEAD), lambda i: (0, 0)), BS((t, HEAD), lambda i: (0, OFF_F // HEAD)),
                                    BS((1, HEAD), lambda i: (0, 0))],
                          out_specs=(BS((t, width), lambda i: (0, 0)), BS((1, HEAD), lambda i: (0, 0))),
                          name="fgate_bwd", compiler_params=_params("arbitrary"))(dc, proj, bias)


def _attn_masked_logits(qs_ref, kn_ref, cq_ref, ck_ref, i, tq):
    lo, hi = i * tq, (i + 1) * tq
    s = _dot(qs_ref[lo:hi, :], kn_ref[0:hi, :], "nt")
    s = s + cq_ref[0, lo:hi, :] - ck_ref[0, :, 0:hi]
    row = lax.broadcasted_iota(jnp.int32, (tq, hi), 0) + lo
    col = lax.broadcasted_iota(jnp.int32, (tq, hi), 1)
    return s, row >= col


def _attn_fwd(proj, cq, ck, qg, kg):
    t = proj.shape[0]
    tq = _tile(t, ATTN_Q_BLOCK)
    nq = t // tq
    scale = HEAD ** -0.5

    def body(q_ref, k_ref, v_ref, cq_ref, ck_ref, qg_ref, kg_ref, o_ref, lse_ref, qs_ref, kn_ref, vb_ref):
        q = q_ref[...]
        k = k_ref[...]
        qs_ref[...] = (q * _rms_scale(q) * qg_ref[...] * scale).astype(BF16)
        kn_ref[...] = (k * _rms_scale(k) * kg_ref[...]).astype(BF16)
        vb_ref[...] = v_ref[...].astype(BF16)
        for i in range(nq):
            lo, hi = i * tq, (i + 1) * tq
            s, keep = _attn_masked_logits(qs_ref, kn_ref, cq_ref, ck_ref, i, tq)
            s = jnp.where(keep, s, -1e30)
            m = jnp.max(s, axis=-1, keepdims=True)
            e = jnp.exp(s - m)
            l = jnp.sum(e, axis=-1, keepdims=True)
            o = _dot(e.astype(BF16), vb_ref[0:hi, :]) / l
            o_ref[lo:hi, :] = o.astype(BF16)
            lse_ref[0, lo:hi, :] = m + jnp.log(l)

    def col(off):
        return BS((t, HEAD), lambda h: (0, off // HEAD + h))

    vec = BS((1, HEAD), lambda h: (0, 0))
    return pl.pallas_call(
        body, out_shape=(SDS((t, N_HEADS * HEAD), BF16), SDS((N_HEADS, t, 1), F32)), grid=(N_HEADS,),
        in_specs=[col(OFF_Q), col(OFF_K), col(OFF_V), BS((1, t, 1), lambda h: (h, 0, 0)), BS((1, 1, t), lambda h: (h, 0, 0)),
                  vec, vec],
        out_specs=(BS((t, HEAD), lambda h: (0, h)), BS((1, t, 1), lambda h: (h, 0, 0))),
        scratch_shapes=[pltpu.VMEM((t, HEAD), BF16)] * 3, name="attn_fwd",
        compiler_params=_params("parallel"))(proj, proj, proj, cq, ck, qg, kg)


def _attn_bwd(proj, cq, ck, lse, dmix, qg, kg):
    t = proj.shape[0]
    tq = _tile(t, ATTN_Q_BLOCK)
    nq = t // tq
    scale = HEAD ** -0.5

    def body(q_ref, k_ref, v_ref, cq_ref, ck_ref, lse_ref, do_ref, qg_ref, kg_ref,
             dq_ref, dk_ref, dv_ref, dcq_ref, dck_ref, dqg_ref, dkg_ref,
             qs_ref, kn_ref, vb_ref, dob_ref, dqs_ref, dkn_ref, dva_ref):
        q = q_ref[...]
        k = k_ref[...]
        rq = _rms_scale(q)
        rk = _rms_scale(k)
        qs_ref[...] = (q * rq * qg_ref[...] * scale).astype(BF16)
        kn_ref[...] = (k * rk * kg_ref[...]).astype(BF16)
        vb_ref[...] = v_ref[...].astype(BF16)
        dob_ref[...] = do_ref[...].astype(BF16)
        dkn_ref[...] = jnp.zeros_like(dkn_ref)
        dva_ref[...] = jnp.zeros_like(dva_ref)
        dck_ref[...] = jnp.zeros_like(dck_ref)
        for i in range(nq):
            lo, hi = i * tq, (i + 1) * tq
            s, keep = _attn_masked_logits(qs_ref, kn_ref, cq_ref, ck_ref, i, tq)
            pr = jnp.where(keep, jnp.exp(s - lse_ref[0, lo:hi, :]), 0.0)
            dp = _dot(dob_ref[lo:hi, :], vb_ref[0:hi, :], "nt")
            delta = jnp.sum(pr * dp, axis=-1, keepdims=True)
            ds = pr * (dp - delta)
            dcq_ref[0, lo:hi, :] = jnp.sum(ds, axis=-1, keepdims=True)
            dck_ref[0, :, 0:hi] += -jnp.sum(ds, axis=0, keepdims=True)
            dsb = ds.astype(BF16)
            dqs_ref[lo:hi, :] = _dot(dsb, kn_ref[0:hi, :])
            dkn_ref[0:hi, :] += _dot(dsb, qs_ref[lo:hi, :], "tn")
            dva_ref[0:hi, :] += _dot(pr.astype(BF16), dob_ref[lo:hi, :], "tn")
        dv_ref[...] = dva_ref[...].astype(BF16)

        @pl.when(pl.program_id(0) == 0)
        def _():
            dqg_ref[...] = jnp.zeros_like(dqg_ref)
            dkg_ref[...] = jnp.zeros_like(dkg_ref)

        qhat = q * rq
        dqn = dqs_ref[...] * scale
        dqg_ref[...] += jnp.sum(dqn * qhat, axis=0, keepdims=True)
        dq_ref[...] = _rms_bwd(dqn * qg_ref[...], qhat, rq).astype(BF16)
        khat = k * rk
        dkn = dkn_ref[...]
        dkg_ref[...] += jnp.sum(dkn * khat, axis=0, keepdims=True)
        dk_ref[...] = _rms_bwd(dkn * kg_ref[...], khat, rk).astype(BF16)

    def col(off):
        return BS((t, HEAD), lambda h: (0, off // HEAD + h))

    vec = BS((1, HEAD), lambda h: (0, 0))
    c_col = BS((1, t, 1), lambda h: (h, 0, 0))
    c_row = BS((1, 1, t), lambda h: (h, 0, 0))
    head_out = BS((t, HEAD), lambda h: (0, h))
    width = N_HEADS * HEAD
    return pl.pallas_call(
        body,
        out_shape=(SDS((t, width), BF16), SDS((t, width), BF16), SDS((t, width), BF16), SDS((N_HEADS, t, 1), F32),
                   SDS((N_HEADS, 1, t), F32), SDS((1, HEAD), F32), SDS((1, HEAD), F32)),
        grid=(N_HEADS,),
        in_specs=[col(OFF_Q), col(OFF_K), col(OFF_V), c_col, c_row, c_col, head_out, vec, vec],
        out_specs=(head_out, head_out, head_out, c_col, c_row, vec, vec),
        scratch_shapes=[pltpu.VMEM((t, HEAD), BF16)] * 4 + [pltpu.VMEM((t, HEAD), F32)] * 3, name="attn_bwd",
        compiler_params=_params("arbitrary"))(proj, proj, proj, cq, ck, lse, dmix, qg, kg)


def _group_cols(off):
    width = N_GROUPS * HEAD
    return lambda t: BS((t, width), lambda i: (0, off // width))


def _tril():
    return lax.broadcasted_iota(jnp.int32, (HEAD, HEAD), 0) >= lax.broadcasted_iota(jnp.int32, (HEAD, HEAD), 1)


def _gmlp_fwd(proj, gain, w_s, b_s):
    t = proj.shape[0]
    width = N_GROUPS * HEAD
    nc = t // HEAD

    def body(gu_ref, gv_ref, gain_ref, ws_ref, bs_ref, o_ref):
        tril = _tril()
        for g in range(N_GROUPS):
            cols = slice(g * HEAD, (g + 1) * HEAD)
            u = _gelu(gu_ref[:, cols])
            vv = _gelu(gv_ref[:, cols])
            vn = (vv * _rms_scale(vv) * gain_ref[:, cols]).astype(BF16)
            w = jnp.where(tril, ws_ref[g], 0.0).astype(BF16)
            for n in range(nc):
                rows = slice(n * HEAD, (n + 1) * HEAD)
                mixed = _dot(w, vn[rows]) + bs_ref[g]
                o_ref[rows, cols] = (u[rows] * mixed).astype(BF16)

    full = lambda shape: BS(shape, lambda i: (0,) * len(shape))
    return pl.pallas_call(body, out_shape=SDS((t, width), BF16), grid=(1,),
                          in_specs=[_group_cols(OFF_GU)(t), _group_cols(OFF_GV)(t), full((1, width)),
                                    full((N_GROUPS, HEAD, HEAD)), full((N_GROUPS, HEAD, 1))],
                          out_specs=full((t, width)), name="gmlp_fwd",
                          compiler_params=_params("arbitrary"))(proj, proj, gain, w_s, b_s)


def _gmlp_bwd(dmix, proj, gain, w_s, b_s):
    t = proj.shape[0]
    width = N_GROUPS * HEAD
    nc = t // HEAD

    def body(dy_ref, gu_ref, gv_ref, gain_ref, ws_ref, bs_ref, dgu_ref, dgv_ref, dgain_ref, dws_ref, dbs_ref, dvn_ref):
        tril = _tril()
        for g in range(N_GROUPS):
            cols = slice(g * HEAD, (g + 1) * HEAD)
            u, du = _gelu(gu_ref[:, cols], with_grad=True)
            vv, dvv = _gelu(gv_ref[:, cols], with_grad=True)
            r = _rms_scale(vv)
            vhat = vv * r
            gain_g = gain_ref[:, cols]
            vn = (vhat * gain_g).astype(BF16)
            w = jnp.where(tril, ws_ref[g], 0.0).astype(BF16)
            dws = jnp.zeros((HEAD, HEAD), F32)
            dbs = jnp.zeros((HEAD, 1), F32)
            for n in range(nc):
                rows = slice(n * HEAD, (n + 1) * HEAD)
                mixed = _dot(w, vn[rows]) + bs_ref[g]
                dy = dy_ref[rows, cols]
                dgu_ref[rows, cols] = (dy * mixed * du[rows]).astype(BF16)
                dm = dy * u[rows]
                dmb = dm.astype(BF16)
                dbs = dbs + jnp.sum(dm, axis=-1, keepdims=True)
                dws = dws + _dot(dmb, vn[rows], "nt")
                dvn_ref[rows, :] = _dot(w, dmb, "tn")
            dvn = dvn_ref[...]
            dgain_ref[:, cols] = jnp.sum(dvn * vhat, axis=0, keepdims=True)
            dgv_ref[:, cols] = (_rms_bwd(dvn * gain_g, vhat, r) * dvv).astype(BF16)
            dws_ref[g] = jnp.where(tril, dws, 0.0)
            dbs_ref[g] = dbs

    full = lambda shape: BS(shape, lambda i: (0,) * len(shape))
    return pl.pallas_call(
        body,
        out_shape=(SDS((t, width), BF16), SDS((t, width), BF16), SDS((1, width), F32), SDS((N_GROUPS, HEAD, HEAD), F32),
                   SDS((N_GROUPS, HEAD, 1), F32)),
        grid=(1,),
        in_specs=[BS((t, width), lambda i: (0, 2)), _group_cols(OFF_GU)(t), _group_cols(OFF_GV)(t), full((1, width)),
                  full((N_GROUPS, HEAD, HEAD)), full((N_GROUPS, HEAD, 1))],
        out_specs=(full((t, width)), full((t, width)), full((1, width)), full((N_GROUPS, HEAD, HEAD)),
                   full((N_GROUPS, HEAD, 1))),
        scratch_shapes=[pltpu.VMEM((t, HEAD), F32)], name="gmlp_bwd",
        compiler_params=_params("arbitrary"))(dmix, proj, proj, gain, w_s, b_s)


def _pool_window_mean_minus_x(x, window, t_idx):
    s, span = x, 1
    while span < window:
        s = s + jnp.where(t_idx >= span, pltpu.roll(s, span, 0), 0.0)
        span *= 2
    cnt = jnp.minimum(t_idx + 1, window).astype(F32)
    return s / cnt - x, cnt


def _pool_fwd(proj, w_pool, scale):
    t = proj.shape[0]
    width = N_GROUPS * HEAD

    def body(x_ref, w_ref, sc_ref, o_ref):
        t_idx = lax.broadcasted_iota(jnp.int32, (t, HEAD), 0)
        for g in range(N_GROUPS):
            cols = slice(g * HEAD, (g + 1) * HEAD)
            d, _ = _pool_window_mean_minus_x(x_ref[:, cols], POOL_WINDOWS[g], t_idx)
            y = _dot(d.astype(BF16), w_ref[g].astype(BF16)) * sc_ref[:, cols]
            o_ref[:, cols] = y.astype(BF16)

    full = lambda shape: BS(shape, lambda i: (0,) * len(shape))
    return pl.pallas_call(body, out_shape=SDS((t, width), BF16), grid=(1,),
                          in_specs=[_group_cols(OFF_XP)(t), full((N_GROUPS, HEAD, HEAD)), full((1, width))],
                          out_specs=full((t, width)), name="pool_fwd",
                          compiler_params=_params("arbitrary"))(proj, w_pool, scale)


def _pool_bwd(dmix, proj, w_pool, scale):
    t = proj.shape[0]
    width = N_GROUPS * HEAD

    def body(dy_ref, x_ref, w_ref, sc_ref, dx_ref, dw_ref, dsc_ref):
        t_idx = lax.broadcasted_iota(jnp.int32, (t, HEAD), 0)
        for g in range(N_GROUPS):
            cols = slice(g * HEAD, (g + 1) * HEAD)
            window = POOL_WINDOWS[g]
            d, cnt = _pool_window_mean_minus_x(x_ref[:, cols], window, t_idx)
            db = d.astype(BF16)
            wb = w_ref[g].astype(BF16)
            dy = dy_ref[:, cols]
            dsc_ref[:, cols] = jnp.sum(dy * _dot(db, wb), axis=0, keepdims=True)
            dyw = (dy * sc_ref[:, cols]).astype(BF16)
            dw_ref[g] = _dot(db, dyw, "tn")
            dd = _dot(dyw, wb, "nt")
            rsum, span = dd / cnt, 1
            while span < window:
                rsum = rsum + jnp.where(t_idx < t - span, pltpu.roll(rsum, t - span, 0), 0.0)
                span *= 2
            dx_ref[:, cols] = (rsum - dd).astype(BF16)

    full = lambda shape: BS(shape, lambda i: (0,) * len(shape))
    return pl.pallas_call(
        body, out_shape=(SDS((t, width), BF16), SDS((N_GROUPS, HEAD, HEAD), F32), SDS((1, width), F32)), grid=(1,),
        in_specs=[BS((t, width), lambda i: (0, 3)), _group_cols(OFF_XP)(t), full((N_GROUPS, HEAD, HEAD)), full((1, width))],
        out_specs=(full((t, width)), full((N_GROUPS, HEAD, HEAD)), full((1, width))), name="pool_bwd",
        compiler_params=_params("arbitrary"))(dmix, proj, w_pool, scale)


def _ffn_fwd(xn, wg, wu):
    t, d = xn.shape
    fs = wg.shape[1]
    tm = _tile(t, 512)

    def body(x_ref, wg_ref, wu_ref, a_ref, b_ref, hh_ref):
        x = x_ref[...]
        a = _dot(x, wg_ref[0], "nt")
        b = _dot(x, wu_ref[0], "nt")
        a_ref[0] = a
        b_ref[0] = b
        hh_ref[0] = (a * _sigmoid(a) * b).astype(BF16)

    w_spec = BS((1, fs, d), lambda j, i: (j, 0, 0))
    o_spec = BS((1, tm, fs), lambda j, i: (j, i, 0))
    return pl.pallas_call(body, out_shape=(SDS((N_DEV, t, fs), F32), SDS((N_DEV, t, fs), F32), SDS((N_DEV, t, fs), BF16)),
                          grid=(N_DEV, t // tm), in_specs=[BS((tm, d), lambda j, i: (i, 0)), w_spec, w_spec],
                          out_specs=(o_spec, o_spec, o_spec), name="ffn_fwd",
                          compiler_params=_params("parallel", "parallel"))(xn, wg, wu)


def _ffn_down(hh, wd, res):
    _, t, fs = hh.shape
    d = wd.shape[2]
    tm, tn = _tile(t, 1024), _tile(d, 1024)

    def body(a_ref, b_ref, r_ref, o_ref, acc_ref):
        k = pl.program_id(2)

        @pl.when(k == 0)
        def _():
            acc_ref[...] = r_ref[...]

        acc_ref[...] += _dot(a_ref[0], b_ref[0])

        @pl.when(k == N_DEV - 1)
        def _():
            o_ref[...] = acc_ref[...]

    o_spec = BS((tm, tn), lambda i, j, k: (i, j))
    return pl.pallas_call(body, out_shape=SDS((t, d), F32), grid=(t // tm, d // tn, N_DEV),
                          in_specs=[BS((1, tm, fs), lambda i, j, k: (k, i, 0)), BS((1, fs, tn), lambda i, j, k: (k, 0, j)), o_spec],
                          out_specs=o_spec, scratch_shapes=[pltpu.VMEM((tm, tn), F32)], name="ffn_down",
                          compiler_params=_params("parallel", "parallel", "arbitrary"))(hh, wd, res)


def _ffn_bwd_hidden(dh_bf, wd, a, b):
    t, d = dh_bf.shape
    fs = wd.shape[1]
    tm = _tile(t, 512)

    def body(dh_ref, wd_ref, a_ref, b_ref, da_ref, db_ref):
        dhh = _dot(dh_ref[...], wd_ref[0], "nt")
        av = a_ref[0]
        sig = _sigmoid(av)
        da_ref[0] = (dhh * b_ref[0] * sig * (1.0 + av * (1.0 - sig))).astype(BF16)
        db_ref[0] = (dhh * av * sig).astype(BF16)

    o_spec = BS((1, tm, fs), lambda j, i: (j, i, 0))
    return pl.pallas_call(body, out_shape=(SDS((N_DEV, t, fs), BF16), SDS((N_DEV, t, fs), BF16)), grid=(N_DEV, t // tm),
                          in_specs=[BS((tm, d), lambda j, i: (i, 0)), BS((1, fs, d), lambda j, i: (j, 0, 0)), o_spec, o_spec],
                          out_specs=(o_spec, o_spec), name="ffn_bwd_hidden",
                          compiler_params=_params("parallel", "parallel"))(dh_bf, wd, a, b)


def _ffn_dwd(hh, dh_bf):
    _, t, fs = hh.shape
    d = dh_bf.shape[1]
    tn = _tile(d, 1024)

    def body(a_ref, b_ref, o_ref):
        o_ref[0] = _dot(a_ref[0], b_ref[...], "tn").astype(BF16)

    return pl.pallas_call(body, out_shape=SDS((N_DEV, fs, d), BF16), grid=(N_DEV, d // tn),
                          in_specs=[BS((1, t, fs), lambda j, n: (j, 0, 0)), BS((t, tn), lambda j, n: (0, n))],
                          out_specs=BS((1, fs, tn), lambda j, n: (j, 0, n)), name="ffn_dwd",
                          compiler_params=_params("parallel", "parallel"))(hh, dh_bf)


def _ffn_dwgu(xn, da, db):
    t, d = xn.shape
    fs = da.shape[2]
    tn = _tile(d, 1024)

    def body(x_ref, da_ref, db_ref, dg_ref, du_ref):
        x = x_ref[...]
        dg_ref[0] = _dot(da_ref[0], x, "tn").astype(BF16)
        du_ref[0] = _dot(db_ref[0], x, "tn").astype(BF16)

    g_spec = BS((1, t, fs), lambda j, i: (j, 0, 0))
    o_spec = BS((1, fs, tn), lambda j, i: (j, 0, i))
    return pl.pallas_call(body, out_shape=(SDS((N_DEV, fs, d), BF16), SDS((N_DEV, fs, d), BF16)), grid=(N_DEV, d // tn),
                          in_specs=[BS((t, tn), lambda j, i: (0, i)), g_spec, g_spec], out_specs=(o_spec, o_spec),
                          name="ffn_dwgu", compiler_params=_params("parallel", "parallel"))(xn, da, db)


def _ffn_dxn(da, db, wg, wu):
    _, t, fs = da.shape
    d = wg.shape[2]
    tm, tn = _tile(t, 1024), _tile(d, 1024)

    def body(da_ref, db_ref, wg_ref, wu_ref, o_ref, acc_ref):
        k = pl.program_id(2)

        @pl.when(k == 0)
        def _():
            acc_ref[...] = jnp.zeros_like(acc_ref)

        acc_ref[...] += _dot(da_ref[0], wg_ref[0]) + _dot(db_ref[0], wu_ref[0])

        @pl.when(k == N_DEV - 1)
        def _():
            o_ref[...] = acc_ref[...]

    g_spec = BS((1, tm, fs), lambda i, j, k: (k, i, 0))
    w_spec = BS((1, fs, tn), lambda i, j, k: (k, 0, j))
    return pl.pallas_call(body, out_shape=SDS((t, d), F32), grid=(t // tm, d // tn, N_DEV),
                          in_specs=[g_spec, g_spec, w_spec, w_spec], out_specs=BS((tm, tn), lambda i, j, k: (i, j)),
                          scratch_shapes=[pltpu.VMEM((tm, tn), F32)], name="ffn_dxn",
                          compiler_params=_params("parallel", "parallel", "arbitrary"))(da, db, wg, wu)


def _ple_fwd(xn, wpg, p_bf, wpp, h):
    t, d = xn.shape
    dp = p_bf.shape[1]
    tn = wpp.shape[2]
    tm = _tile(t, 1024)

    def body(x_ref, wg_ref, p_ref, wp_ref, h_ref, o_ref, z_ref, pp_ref):
        z = _dot(x_ref[...], wg_ref[...])
        pp = _dot(p_ref[...], wp_ref[0])
        z_ref[...] = z
        pp_ref[...] = pp
        o_ref[...] = h_ref[...] + pp * _sigmoid(z)

    o_spec = BS((tm, tn), lambda i, j: (i, j))
    out = SDS((t, d), F32)
    return pl.pallas_call(body, out_shape=(out, out, out), grid=(t // tm, N_DEV),
                          in_specs=[BS((tm, d), lambda i, j: (i, 0)), BS((d, tn), lambda i, j: (0, j)),
                                    BS((tm, dp), lambda i, j: (i, 0)), BS((1, dp, tn), lambda i, j: (j, 0, 0)), o_spec],
                          out_specs=(o_spec, o_spec, o_spec), name="ple_fwd",
                          compiler_params=_params("parallel", "parallel"))(xn, wpg, p_bf, wpp, h)


def _ple_bwd_gate(dh, z, pp, after):
    t, d = dh.shape
    tm = _tile(t, 256)

    def body(dh_ref, z_ref, pp_ref, after_ref, dpp_ref, dz_ref):
        g = _sigmoid(z_ref[...])
        dh_v = dh_ref[...]
        dpp_ref[...] = (dh_v * g).astype(BF16)
        dz_ref[...] = (dh_v * pp_ref[...] * g * (1.0 - g)).astype(BF16)

    row = BS((tm, d), lambda i: (i, 0))
    return pl.pallas_call(body, out_shape=(SDS((t, d), BF16), SDS((t, d), BF16)), grid=(t // tm,),
                          in_specs=[row, row, row, BS(after.shape, lambda i: (0, 0))], out_specs=(row, row),
                          name="ple_bwd_gate", compiler_params=_params("parallel"))(dh, z, pp, after)


def _ple_dwpp(p_bf, dpp):
    t, dp = p_bf.shape
    tn = dpp.shape[1] // N_DEV

    def body(p_ref, g_ref, o_ref):
        o_ref[0] = _dot(p_ref[...], g_ref[...], "tn").astype(BF16)

    return pl.pallas_call(body, out_shape=SDS((N_DEV, dp, tn), BF16), grid=(N_DEV,),
                          in_specs=[BS((t, dp), lambda j: (0, 0)), BS((t, tn), lambda j: (0, j))],
                          out_specs=BS((1, dp, tn), lambda j: (j, 0, 0)), name="ple_dwpp",
                          compiler_params=_params("parallel"))(p_bf, dpp)


def _loss_call(y, target):
    t, d = y.shape
    tm = _tile(t, 256)

    def body(y_ref, t_ref, dy_ref, loss_ref):
        diff = y_ref[...] - t_ref[...]
        dy_ref[...] = diff * (1.0 / d)

        @pl.when(pl.program_id(0) == 0)
        def _():
            loss_ref[...] = jnp.zeros_like(loss_ref)

        loss_ref[...] += 0.5 * jnp.sum(jnp.mean(diff * diff, axis=-1, keepdims=True), axis=0, keepdims=True)

    row = BS((tm, d), lambda i: (i, 0))
    return pl.pallas_call(body, out_shape=(SDS((t, d), F32), SDS((1, 1), F32)), grid=(t // tm,), in_specs=[row, row],
                          out_specs=(row, BS((1, 1), lambda i: (0, 0))), name="loss",
                          compiler_params=_params("arbitrary"))(y, target)


def _mesh_pos():
    return lax.axis_index("x"), lax.axis_index("y"), lax.axis_index("c")


def _dev_index(px, py, pc):
    return 4 * px + 2 * py + pc


HBM = pl.BlockSpec(memory_space=pltpu.HBM)
SEM = pl.BlockSpec(memory_space=pltpu.SEMAPHORE)
DATAFLOW = pltpu.SideEffectType.DATAFLOW_SIDE_EFFECTING


def _in_hbm(arrs):
    return [pltpu.with_memory_space_constraint(a, pltpu.HBM) for a in arrs]


def _other_chips(x, y):
    return [(1 - x, y), (x, 1 - y), (1 - x, 1 - y)]


def _split_start(name, srcs, lands, n_sems, copies, after=None):
    n = len(srcs)
    after = [] if after is None else [after]
    n_in = 2 * n + len(after)

    def body(*refs):
        for cp in copies(refs[:n], refs[n:2 * n], refs[n_in], refs[n_in + 1]):
            cp.start()
        token = refs[-1]
        token[...] = jnp.zeros_like(token)

    thru = [pltpu.HBM(a.shape, a.dtype) for a in list(srcs) + list(lands)]
    outs = pl.pallas_call(
        body, name=name,
        out_shape=(pltpu.SemaphoreType.DMA((n * n_sems,)), pltpu.SemaphoreType.DMA((n * n_sems,)), *thru, SDS((8, HEAD), F32)),
        in_specs=[HBM] * (2 * n) + [ANY] * len(after),
        out_specs=(SEM, SEM, *([HBM] * (2 * n)), pl.BlockSpec(memory_space=pltpu.VMEM)),
        input_output_aliases={q: 2 + q for q in range(2 * n)},
        compiler_params=pltpu.CompilerParams(has_side_effects=DATAFLOW))(*_in_hbm(list(srcs) + list(lands)), *after)
    return outs[0], outs[1], list(outs[2:2 + n]), list(outs[2 + n:2 + 2 * n]), outs[-1]


def _split_wait(name, srcs, lands, send_sems, recv_sems, after, copies):
    n = len(srcs)
    after = list(after) if isinstance(after, (list, tuple)) else [after]

    def body(*refs):
        for cp in copies(refs[:n], refs[n:2 * n], refs[2 * n], refs[2 * n + 1]):
            cp.wait_send()
            cp.wait_recv()

    thru = [pltpu.HBM(a.shape, a.dtype) for a in list(srcs) + list(lands)]
    outs = pl.pallas_call(
        body, name=name, out_shape=tuple(thru), in_specs=[HBM] * (2 * n) + [SEM, SEM] + [ANY] * len(after),
        out_specs=tuple([HBM] * (2 * n)), input_output_aliases={q: q for q in range(2 * n)},
        compiler_params=pltpu.CompilerParams(has_side_effects=DATAFLOW))(*list(srcs), *list(lands), send_sems, recv_sems, *after)
    return list(outs[:n]), list(outs[n:])


def _gather_ici_copies(layer, waiting):
    def copies(src_refs, land_refs, send_sems, recv_sems):
        x, y, c = _mesh_pos()
        out = []
        for a in range(len(src_refs)):
            for j, chip in enumerate(_other_chips(x, y)):
                slot = _dev_index(*chip, c) if waiting else _dev_index(x, y, c)
                out.append(pltpu.make_async_remote_copy(
                    src_ref=src_refs[a].at[layer], dst_ref=land_refs[a].at[slot], send_sem=send_sems.at[3 * a + j],
                    recv_sem=recv_sems.at[3 * a + j], device_id=(*chip, c), device_id_type=pl.DeviceIdType.MESH))
        return out
    return copies


def _slab_tile(r, c):
    for cand in (512, 256, 176, 128, 64, 16, 8):
        if r % cand == 0:
            return cand, c
    return r, 256


def _place_own(pos, src, layer, land):
    _, r, c = land.shape
    tr, tc = _slab_tile(r, c)

    def body(pos_ref, s_ref, l_ref, o_ref):
        o_ref[...] = s_ref[...]

    grid_spec = pltpu.PrefetchScalarGridSpec(
        num_scalar_prefetch=1, grid=(r // tr, c // tc), in_specs=[BS((1, tr, tc), lambda i, j, pos: (layer, i, j)), ANY],
        out_specs=BS((1, tr, tc), lambda i, j, pos: (pos[0], i, j)))
    return pl.pallas_call(body, grid_spec=grid_spec, out_shape=SDS(land.shape, land.dtype), input_output_aliases={2: 0},
                          name="place_own", compiler_params=_params("parallel", "parallel"))(pos, src, land)


def _gather_sibling_copies(layer, waiting):
    def copies(src_refs, land_refs, send_sems, recv_sems):
        x, y, c = _mesh_pos()
        blocks = [(x, y)] + _other_chips(x, y)
        out = []
        for a in range(len(src_refs)):
            for k in range(4):
                slot = land_refs[a].at[_dev_index(*blocks[k], 1 - c if waiting else c)]
                src = src_refs[a].at[layer] if (k == 0 and not waiting) else slot
                out.append(pltpu.make_async_remote_copy(
                    src_ref=src, dst_ref=slot, send_sem=send_sems.at[4 * a + k], recv_sem=recv_sems.at[4 * a + k],
                    device_id=(x, y, 1 - c), device_id_type=pl.DeviceIdType.MESH))
        return out
    return copies


def _gather_d2d(name, srcs, layer, lands):
    n = len(srcs)

    def body(*refs):
        src_refs, land_refs = refs[:n], refs[n:2 * n]
        send_sems, recv_sems = refs[3 * n:]
        sends = _gather_sibling_copies(layer, False)(src_refs, land_refs, send_sems, recv_sems)
        for cp in sends:
            cp.start()
        for cp in _gather_sibling_copies(layer, True)(src_refs, land_refs, send_sems, recv_sems):
            cp.wait_recv()
        for cp in sends:
            cp.wait_send()

    outs = pl.pallas_call(
        body, name=name, out_shape=tuple(SDS(l.shape, l.dtype) for l in lands), in_specs=[ANY] * (2 * n),
        out_specs=tuple([ANY] * n), input_output_aliases={n + q: q for q in range(n)},
        scratch_shapes=[pltpu.SemaphoreType.DMA((4 * n,)), pltpu.SemaphoreType.DMA((4 * n,))],
        compiler_params=pltpu.CompilerParams(has_side_effects=True))(*srcs, *lands)
    return list(outs)


def _sibling_copies(src_refs, land_refs, send_sems, recv_sems):
    x, y, c = _mesh_pos()
    out = []
    for a in range(len(src_refs)):
        whole = len(src_refs[a].shape) == 2
        for q in range(1 if whole else 4):
            src = src_refs[a] if whole else src_refs[a].at[2 * q + (1 - c)]
            dst = land_refs[a] if whole else land_refs[a].at[q]
            out.append(pltpu.make_async_remote_copy(
                src_ref=src, dst_ref=dst, send_sem=send_sems.at[4 * a + q], recv_sem=recv_sems.at[4 * a + q],
                device_id=(x, y, 1 - c), device_id_type=pl.DeviceIdType.MESH))
    return out


def _chip_sums(pos, own, got):
    whole = own.ndim == 2
    r, c = own.shape[-2:]
    tr, tc = _slab_tile(r, c)

    def body(pos_ref, own_ref, got_ref, o_ref):
        mine = own_ref[...] if whole else own_ref[0]
        o_ref[...] = (mine.astype(F32) + got_ref[...].astype(F32)).astype(o_ref.dtype)

    if whole:
        grid = (r // tr, c // tc)
        blk = BS((tr, tc), lambda i, j, pos: (i, j))
        specs, o_spec, ins, out = [blk, blk], blk, [own, got], SDS((r, c), own.dtype)
    else:
        grid = (N_CHIPS, r // tr, c // tc)
        blk = BS((1, tr, tc), lambda q, i, j, pos: (q, i, j))
        specs = [BS((1, 1, tr, tc), lambda q, i, j, pos: (q, pos[2], i, j)), blk]
        o_spec, ins, out = blk, [own.reshape(N_CHIPS, 2, r, c), got], SDS((N_CHIPS, r, c), own.dtype)
    grid_spec = pltpu.PrefetchScalarGridSpec(num_scalar_prefetch=1, grid=grid, in_specs=specs, out_specs=o_spec)
    return pl.pallas_call(body, out_shape=out, grid_spec=grid_spec, name="chip_sums",
                          compiler_params=_params(*["parallel"] * len(grid)))(pos, *ins)


def _chip_copies(waiting):
    def copies(src_refs, land_refs, send_sems, recv_sems):
        x, y, c = _mesh_pos()
        out = []
        for a in range(len(src_refs)):
            whole = len(src_refs[a].shape) == 2
            for j, (qx, qy) in enumerate(_other_chips(x, y)):
                src = src_refs[a] if whole else src_refs[a].at[2 * qx + qy]
                slot = 2 * qx + qy if waiting else 2 * x + y
                out.append(pltpu.make_async_remote_copy(
                    src_ref=src, dst_ref=land_refs[a].at[slot], send_sem=send_sems.at[3 * a + j],
                    recv_sem=recv_sems.at[3 * a + j], device_id=(qx, qy, c), device_id_type=pl.DeviceIdType.MESH))
        return out
    return copies


def _adamw_math(w, g, m, v):
    m = ADAM_B1 * m + (1.0 - ADAM_B1) * g
    v = ADAM_B2 * v + (1.0 - ADAM_B2) * (g * g)
    m_hat = m / (1.0 - ADAM_B1 ** ADAM_STEP)
    v_hat = v / (1.0 - ADAM_B2 ** ADAM_STEP)
    delta = -ADAM_LR * (m_hat / (jnp.sqrt(v_hat) + ADAM_EPS) + ADAM_WD * w)
    return delta, m, v


N_CHIPS = 4


def _sum_partials(my_chip, own, parts_ref):
    g = None
    for q in range(N_CHIPS):
        term = jnp.where(my_chip == q, own, parts_ref[q].astype(F32))
        g = term if g is None else g + term
    return g


def _adamw_layer(name, pos, parts, own, w, m, v, layer, prev):
    depth, r, c = w.shape
    tr = next(cand for cand in (256, 128, 64, 32, 16, 8) if r % cand == 0)

    def body(pos_ref, parts_ref, own_ref, w_ref, m_ref, v_ref, *rest):
        g_ref, d_ref, nm_ref, nv_ref = rest[-4:]
        g = _sum_partials(pos_ref[1], own_ref[0].astype(F32), parts_ref)
        delta, nm, nv = _adamw_math(w_ref[0], g, m_ref[0], v_ref[0])
        g_ref[0] = g
        d_ref[0] = delta
        nm_ref[0] = nm
        nv_ref[0] = nv

    lay = BS((1, tr, c), lambda i, pos: (layer, i, 0))
    stacked = SDS((depth, r, c), F32)
    ins = [pos, parts, own, w, m, v]
    specs = [BS((N_CHIPS, tr, c), lambda i, pos: (0, i, 0)), BS((1, tr, c), lambda i, pos: (pos[1], i, 0)), lay, lay, lay]
    aliases = {}
    if prev is not None:
        ins += list(prev)
        specs += [ANY] * 4
        aliases = {6 + q: q for q in range(4)}
    grid_spec = pltpu.PrefetchScalarGridSpec(num_scalar_prefetch=1, grid=(r // tr,), in_specs=specs, out_specs=(lay,) * 4)
    return pl.pallas_call(body, out_shape=(stacked,) * 4, grid_spec=grid_spec, input_output_aliases=aliases, name=name,
                          compiler_params=_params("parallel"))(*ins)


def _sum_chips(pos, parts, own):
    _, r, c = parts.shape
    tr, tc = _slab_tile(r, c)

    def body(pos_ref, parts_ref, own_ref, g_ref):
        g_ref[...] = _sum_partials(pos_ref[1], own_ref[0].astype(F32), parts_ref)

    grid_spec = pltpu.PrefetchScalarGridSpec(
        num_scalar_prefetch=1, grid=(r // tr, c // tc),
        in_specs=[BS((N_CHIPS, tr, tc), lambda i, j, pos: (0, i, j)), BS((1, tr, tc), lambda i, j, pos: (pos[1], i, j))],
        out_specs=BS((tr, tc), lambda i, j, pos: (i, j)))
    return pl.pallas_call(body, out_shape=SDS((r, c), F32), grid_spec=grid_spec, name="sum_chips",
                          compiler_params=_params("parallel", "parallel"))(pos, parts, own)


def _adamw_ready(name, g, w, m, v, layer, prev):
    depth, r, c = w.shape
    tr = next(cand for cand in (256, 128, 64, 32, 16, 8) if r % cand == 0)

    def body(g_in_ref, w_ref, m_ref, v_ref, *rest):
        g_ref, d_ref, nm_ref, nv_ref = rest[-4:]
        g = g_in_ref[...]
        delta, nm, nv = _adamw_math(w_ref[0], g, m_ref[0], v_ref[0])
        g_ref[0] = g
        d_ref[0] = delta
        nm_ref[0] = nm
        nv_ref[0] = nv

    lay = BS((1, tr, c), lambda i: (layer, i, 0))
    stacked = SDS((depth, r, c), F32)
    ins, specs, aliases = [g, w, m, v], [BS((tr, c), lambda i: (i, 0)), lay, lay, lay], {}
    if prev is not None:
        ins += list(prev)
        specs += [ANY] * 4
        aliases = {4 + q: q for q in range(4)}
    return pl.pallas_call(body, out_shape=(stacked,) * 4, grid=(r // tr,), in_specs=specs, out_specs=(lay,) * 4,
                          input_output_aliases=aliases, name=name, compiler_params=_params("parallel"))(*ins)


def _adamw_small(pos, parts, own, w, m, v):
    r, c = w.shape
    tr = _tile(r, 256)

    def body(pos_ref, parts_ref, own_ref, w_ref, m_ref, v_ref, g_ref, d_ref, nm_ref, nv_ref):
        g = _sum_partials(pos_ref[1], own_ref[...], parts_ref)
        delta, nm, nv = _adamw_math(w_ref[...], g, m_ref[...], v_ref[...])
        g_ref[...] = g
        d_ref[...] = delta
        nm_ref[...] = nm
        nv_ref[...] = nv

    row = BS((tr, c), lambda i, pos: (i, 0))
    out = SDS((r, c), F32)
    grid_spec = pltpu.PrefetchScalarGridSpec(
        num_scalar_prefetch=1, grid=(r // tr,), in_specs=[BS((N_CHIPS, tr, c), lambda i, pos: (0, i, 0)), row, row, row, row],
        out_specs=(row,) * 4)
    return pl.pallas_call(body, out_shape=(out,) * 4, grid_spec=grid_spec, name="adamw_small",
                          compiler_params=_params("parallel"))(pos, parts, own, w, m, v)


def _pad_w_in(gathered):
    d = gathered.shape[2]
    w = gathered.reshape(PROJ_RAW, d)
    real_f = OFF_F + F_COLS
    return jnp.concatenate([w[:real_f], jnp.zeros((OFF_GU - real_f, d), w.dtype), w[real_f:]], axis=0)


def _unpad_dw_in(dw):
    d = dw.shape[1]
    real_f = OFF_F + F_COLS
    return jnp.concatenate([dw[:real_f], dw[OFF_GU:]], axis=0).reshape(N_DEV, PROJ_RAW // N_DEV, d)


def _pack_small(tree):
    flat = jnp.concatenate([tree[n].reshape(-1) for n in SMALL])
    rows = -(-flat.shape[0] // (256 * HEAD)) * 256
    return jnp.pad(flat, (0, rows * HEAD - flat.shape[0])).reshape(rows, HEAD)


def _unpack_small(packed, like):
    flat = packed.reshape(-1)
    out, off = {}, 0
    for n in SMALL:
        size = like[n].size
        out[n] = flat[off:off + size].reshape(like[n].shape)
        off += size
    return out


def _layer_fwd_mix(h0, sw, gw):
    t, d = h0.shape
    xn1 = _rms_fwd("rms_fwd", h0, sw["norm_mix"])
    proj = _matmul("proj_fwd", xn1, gw["w_in"], "nt", F32, t, 512)
    c = _fgate_fwd(proj, sw["forget_bias"])
    cq = c[:, :N_HEADS].T.reshape(N_HEADS, t, 1)
    ck = cq.reshape(N_HEADS, 1, t)
    y_attn, lse = _attn_fwd(proj, cq, ck, sw["q_norm"], sw["k_norm"])
    y_gmlp = _gmlp_fwd(proj, sw["gmlp_v_norm"], sw["gmlp_w_s"], sw["gmlp_b_s"])
    y_pool = _pool_fwd(proj, sw["pool_w"], sw["pool_scale"])
    mix = jnp.concatenate([y_attn, y_gmlp, y_pool], axis=1)
    h1 = _matmul("out_fwd", mix, gw["w_out"], "nn", F32, t, 512, res=h0)
    return h1, dict(h0=h0, xn1=xn1, proj=proj, cq=cq, ck=ck, lse=lse, mix=mix, h1=h1)


def _layer_fwd_ffn(h1, p_bf, sw, gw):
    xn2 = _rms_fwd("rms_fwd", h1, sw["norm_ffn"])
    a, b, hh = _ffn_fwd(xn2, gw["w_ffn_gate"], gw["w_ffn_up"])
    h2 = _ffn_down(hh, gw["w_ffn_down"], h1)
    xn3 = _rms_fwd("rms_fwd", h2, sw["norm_ple"])
    h3, z, pp = _ple_fwd(xn3, gw["w_ple_gate"], p_bf, gw["w_ple_proj"], h2)
    return h3, dict(xn2=xn2, a=a, b=b, hh=hh, h2=h2, xn3=xn3, z=z, pp=pp)


FFN_SIDE = ("w_ple_proj", "w_ple_gate", "w_ffn_down", "w_ffn_gate", "w_ffn_up")
MIX_SIDE = ("w_out", "w_in")


def _layer_bwd_ffn(dh3, p_bf, sw, gw, s, hook, after):
    t, d = dh3.shape
    big, small = {}, {}
    dpp, dz = _ple_bwd_gate(dh3, s["z"], s["pp"], after)
    big["w_ple_proj"] = _ple_dwpp(p_bf, dpp)
    big["w_ple_gate"] = _matmul("dw_tn", s["xn3"], dz, "tn", BF16, d, 512).reshape(N_DEV, d // N_DEV, d)
    dxn3 = _matmul("dx_nt", dz, gw["w_ple_gate"], "nt", F32, t, 512)
    dh2, dh2_bf, small["norm_ple"] = _rms_bwd_call("rms_bwd", dxn3, s["h2"], sw["norm_ple"] + hook(dxn3)[0, 0], dh3)
    da, db = _ffn_bwd_hidden(dh2_bf, gw["w_ffn_down"], s["a"], s["b"])
    big["w_ffn_down"] = _ffn_dwd(s["hh"], dh2_bf)
    big["w_ffn_gate"], big["w_ffn_up"] = _ffn_dwgu(s["xn2"], da, db)
    dxn2 = _ffn_dxn(da, db, gw["w_ffn_gate"], gw["w_ffn_up"])
    dh1, dh1_bf, small["norm_ffn"] = _rms_bwd_call("rms_bwd", dxn2, s["h1"], sw["norm_ffn"], dh2)
    return (dh1, dh1_bf), big, small


def _layer_bwd_mix(dh1, dh1_bf, sw, gw, s, hook, after):
    t, d = dh1.shape
    big, small = {}, {}
    dmix = _matmul("dx_nt", dh1_bf, gw["w_out"], "nt", F32, t, 512, after=after)
    big["w_out"] = _matmul("dw_tn", s["mix"], dh1_bf, "tn", BF16, d, 512).reshape(N_DEV, d // N_DEV, d)
    proj = s["proj"]
    dxp, small["pool_w"], small["pool_scale"] = _pool_bwd(dmix, proj, sw["pool_w"], sw["pool_scale"] + hook(dmix)[0, 0])
    dgu, dgv, small["gmlp_v_norm"], small["gmlp_w_s"], small["gmlp_b_s"] = _gmlp_bwd(
        dmix, proj, sw["gmlp_v_norm"], sw["gmlp_w_s"], sw["gmlp_b_s"])
    dq, dk, dv, dcq, dck, small["q_norm"], small["k_norm"] = _attn_bwd(
        proj, s["cq"], s["ck"], s["lse"], dmix, sw["q_norm"], sw["k_norm"])
    dc = (dcq.reshape(N_HEADS, t) + dck.reshape(N_HEADS, t)).T
    dc = jnp.pad(dc, ((0, 0), (0, HEAD - N_HEADS)))
    df, small["forget_bias"] = _fgate_bwd(dc, proj, sw["forget_bias"])
    dproj = jnp.concatenate([dq, dk, dv, df, dgu, dgv, dxp], axis=1)
    big["w_in"] = _unpad_dw_in(_matmul("dw_in_tn", dproj, s["xn1"], "tn", BF16, 512, d))
    dxn1 = _matmul("dx_in_nn", dproj, gw["w_in"], "nn", F32, _tile(t, 512), 512)
    dh0, _, small["norm_mix"] = _rms_bwd_call("rms_bwd", dxn1, s["h0"], sw["norm_mix"], dh1)
    return dh0, big, small


def _small_kernel_shapes(sm, i):
    row = lambda a: a[i].reshape(1, -1)
    return dict(
        norm_mix=row(sm["norm_mix"]), norm_ffn=row(sm["norm_ffn"]), norm_ple=row(sm["norm_ple"]),
        q_norm=row(sm["q_norm"]), k_norm=row(sm["k_norm"]),
        forget_bias=jnp.pad(row(sm["forget_bias"]), ((0, 0), (0, HEAD - F_COLS))),
        gmlp_v_norm=row(sm["gmlp_v_norm"]), gmlp_w_s=sm["gmlp_w_s"][i], gmlp_b_s=sm["gmlp_b_s"][i].reshape(N_GROUPS, HEAD, 1),
        pool_w=sm["pool_w"][i], pool_scale=row(sm["pool_scale"]))


def _small_grad_shapes(g, like):
    out = {}
    for n in SMALL:
        v = g[n]
        if n == "forget_bias":
            v = v[:, :F_COLS]
        out[n] = v.reshape(like[n].shape[1:])
    return out


def kernel(x, p, norm_mix, w_in, q_norm, k_norm, forget_bias, gmlp_v_norm, gmlp_w_s, gmlp_b_s, pool_w, pool_scale, w_out, norm_ffn, w_ffn_gate, w_ffn_up, w_ffn_down, norm_ple, w_ple_gate, w_ple_proj, loss_target, m_norm_mix, m_w_in, m_q_norm, m_k_norm, m_forget_bias, m_gmlp_v_norm, m_gmlp_w_s, m_gmlp_b_s, m_pool_w, m_pool_scale, m_w_out, m_norm_ffn, m_w_ffn_gate, m_w_ffn_up, m_w_ffn_down, m_norm_ple, m_w_ple_gate, m_w_ple_proj, v_norm_mix, v_w_in, v_q_norm, v_k_norm, v_forget_bias, v_gmlp_v_norm, v_gmlp_w_s, v_gmlp_b_s, v_pool_w, v_pool_scale, v_w_out, v_norm_ffn, v_w_ffn_gate, v_w_ffn_up, v_w_ffn_down, v_norm_ple, v_w_ple_gate, v_w_ple_proj):
    w = dict(norm_mix=norm_mix, w_in=w_in, q_norm=q_norm, k_norm=k_norm, forget_bias=forget_bias, gmlp_v_norm=gmlp_v_norm,
             gmlp_w_s=gmlp_w_s, gmlp_b_s=gmlp_b_s, pool_w=pool_w, pool_scale=pool_scale, w_out=w_out, norm_ffn=norm_ffn,
             w_ffn_gate=w_ffn_gate, w_ffn_up=w_ffn_up, w_ffn_down=w_ffn_down, norm_ple=norm_ple, w_ple_gate=w_ple_gate,
             w_ple_proj=w_ple_proj)
    m = dict(norm_mix=m_norm_mix, w_in=m_w_in, q_norm=m_q_norm, k_norm=m_k_norm, forget_bias=m_forget_bias,
             gmlp_v_norm=m_gmlp_v_norm, gmlp_w_s=m_gmlp_w_s, gmlp_b_s=m_gmlp_b_s, pool_w=m_pool_w, pool_scale=m_pool_scale,
             w_out=m_w_out, norm_ffn=m_norm_ffn, w_ffn_gate=m_w_ffn_gate, w_ffn_up=m_w_ffn_up, w_ffn_down=m_w_ffn_down,
             norm_ple=m_norm_ple, w_ple_gate=m_w_ple_gate, w_ple_proj=m_w_ple_proj)
    v = dict(norm_mix=v_norm_mix, w_in=v_w_in, q_norm=v_q_norm, k_norm=v_k_norm, forget_bias=v_forget_bias,
             gmlp_v_norm=v_gmlp_v_norm, gmlp_w_s=v_gmlp_w_s, gmlp_b_s=v_gmlp_b_s, pool_w=v_pool_w, pool_scale=v_pool_scale,
             w_out=v_w_out, norm_ffn=v_norm_ffn, w_ffn_gate=v_w_ffn_gate, w_ffn_up=v_w_ffn_up, w_ffn_down=v_w_ffn_down,
             norm_ple=v_norm_ple, w_ple_gate=v_w_ple_gate, w_ple_proj=v_w_ple_proj)
    for tree in (w, m, v):
        for n in TRANSPOSED:
            tree[n] = jnp.transpose(tree[n], (0, 2, 1))
    depth = w_in.shape[0]
    t, d = x.shape[1], x.shape[2]
    h = x[0]
    mx, my, mc = _mesh_pos()
    pos = jnp.stack([_dev_index(mx, my, mc), 2 * mx + my, mc]).astype(jnp.int32)
    p_bf = p[:, 0].astype(BF16)
    w_bf = {n: w[n].astype(BF16) for n in BIG}
    w_bf["w_in"] = jnp.transpose(w["w_in"], (0, 2, 1)).astype(BF16)

    parts_of = {"A": MIX_SIDE[::-1], "B": FFN_SIDE[::-1]}
    src = {p_: [w_bf[n] for n in names] for p_, names in parts_of.items()}

    def ici_start(i, part, after=None):
        lands = [lax.empty((N_DEV, *a.shape[1:]), a.dtype) for a in src[part]]
        send_sems, recv_sems, src[part], lands, token = _split_start(
            f"gather_start_{part}_l{i}", src[part], lands, 3, _gather_ici_copies(i, False), after)
        return (send_sems, recv_sems, lands), token

    def both_start(i):
        state["A"], token = ici_start(i, "A")
        state["B"], token = ici_start(i, "B", token)
        return token

    def ici_wait(i, part, state, after):
        send_sems, recv_sems, lands = state
        src[part], lands = _split_wait(f"gather_wait_{part}_l{i}", src[part], lands, send_sems, recv_sems, after,
                                       _gather_ici_copies(i, True))
        return lands

    def next_layer_start(i):
        return both_start(i + 1)[0, 0]

    def placed(i, part, lands):
        return {n: _place_own(pos, s_, i, land) for n, s_, land in zip(parts_of[part], src[part], lands)}

    gathered, saved = [], []
    state = {}
    both_start(0)
    for i in range(depth):
        sw = _small_kernel_shapes(w, i)
        lands_a = ici_wait(i, "A", state["A"], h)
        if i > 0:
            lands_b = ici_wait(i, "B", state["B"], h)
        lands_a = _gather_d2d(f"gather_sibling_A_l{i}", src["A"], i, lands_a)
        if i > 0:
            send_sems, recv_sems, src["B"], lands_b, token = _split_start(
                f"gather_sibling_start_l{i}", src["B"], lands_b, 4, _gather_sibling_copies(i, False))
            sw["norm_mix"] = sw["norm_mix"] + (next_layer_start(i) if i + 1 < depth else token[0, 0])
        gw = placed(i, "A", lands_a)
        gw["w_in"] = _pad_w_in(gw["w_in"])
        gw["w_out"] = gw["w_out"].reshape(d, d)
        h, s = _layer_fwd_mix(h, sw, gw)
        if i > 0:
            src["B"], lands_b = _split_wait(f"gather_sibling_wait_l{i}", src["B"], lands_b, send_sems, recv_sems, h,
                                            _gather_sibling_copies(i, True))
        else:
            lands_b = _gather_d2d("gather_sibling_B_l0", src["B"], 0, ici_wait(0, "B", state["B"], h))
            if depth > 1:
                sw["norm_ffn"] = sw["norm_ffn"] + next_layer_start(0)
        gw.update(placed(i, "B", lands_b))
        gw["w_ple_gate"] = gw["w_ple_gate"].reshape(d, d)
        h, s_ffn = _layer_fwd_ffn(h, p_bf[i], sw, gw)
        gathered.append((sw, gw))
        saved.append({**s, **s_ffn})

    dh, loss_part = _loss_call(h, loss_target[0])
    loss = lax.psum(loss_part[0, 0], ("x", "y", "c"))

    small_grads = [None] * depth
    stacked = {n: None for n in BIG}
    parts = [dict() for _ in range(depth)]
    on_d2d, on_ici = [], [[] for _ in range(depth)]

    def round_start(tag, layer, names, grads):
        lands = [lax.empty(g.shape if g.ndim == 2 else (N_CHIPS, *g.shape[1:]), g.dtype) for g in grads]
        send_sems, recv_sems, grads, lands, token = _split_start(f"sibling_start_{tag}", grads, lands, 4, _sibling_copies)
        on_d2d.append((tag, layer, names, grads, lands, send_sems, recv_sems))
        return token

    def hook(after):
        token = jnp.zeros((1, 1), F32)
        while on_d2d:
            tag, layer, names, grads, lands, send_sems, recv_sems = on_d2d.pop(0)
            grads, got = _split_wait(f"sibling_wait_{tag}", grads, lands, send_sems, recv_sems, after, _sibling_copies)
            sums = [_chip_sums(pos, g, r) for g, r in zip(grads, got)]
            lands = [lax.empty((N_CHIPS, *q.shape[-2:]), q.dtype) for q in sums]
            send_sems, recv_sems, sums, lands, token = _split_start(f"chip_start_{tag}", sums, lands, 3, _chip_copies(False))
            on_ici[layer].append((tag, names, sums, lands, send_sems, recv_sems))
        return token

    def finish(layer, after):
        for tag, names, sums, lands, send_sems, recv_sems in on_ici[layer]:
            sums, lands = _split_wait(f"chip_wait_{tag}", sums, lands, send_sems, recv_sems, after, _chip_copies(True))
            parts[layer].update(zip(names, zip(lands, sums)))
        for n in BIG:
            if n == "w_in":
                g = _sum_chips(pos, *parts[layer][n]).T
                stacked[n] = _adamw_ready(f"adamw_{n}", g, w[n], m[n], v[n], layer, stacked[n])
            else:
                stacked[n] = _adamw_layer(f"adamw_{n}", pos, *parts[layer][n], w[n], m[n], v[n], layer, stacked[n])

    token = jnp.zeros((8, HEAD), F32) + loss * 0.0
    for i in reversed(range(depth)):
        sw, gw = gathered[i]
        (dh1, dh1_bf), big, small = _layer_bwd_ffn(dh, p_bf[i], sw, gw, saved[i], hook, token)
        token = round_start(f"ffn_l{i}", i, FFN_SIDE, [big[n] for n in FFN_SIDE])
        dh, big, small_mix = _layer_bwd_mix(dh1, dh1_bf, sw, gw, saved[i], hook, token)
        small_grads[i] = _small_grad_shapes({**small, **small_mix}, w)
        names, grads = list(MIX_SIDE), [big[n] for n in MIX_SIDE]
        if i == 0:
            g_small = {n: jnp.stack([small_grads[q][n] for q in range(depth)]) for n in SMALL}
            names.append("small")
            grads.append(_pack_small(g_small))
        token = round_start(f"mix_l{i}", i, names, grads)
        if i == 0:
            token = hook(token)
        if i + 1 < depth:
            finish(i + 1, token)
    finish(0, [stacked[n][0] for n in BIG] if depth > 1 else token)
    packed = _adamw_small(pos, *parts[0]["small"], _pack_small(w), _pack_small(m), _pack_small(v))
    small_out = [_unpack_small(q, w) for q in packed]

    results = []
    for q in range(4):
        big_out = {n: jnp.transpose(stacked[n][q], (0, 2, 1)) if n in TRANSPOSED else stacked[n][q] for n in BIG}
        results.append({**big_out, **small_out[q]})
    outs = [loss, dh[None]]
    for q in range(4):
        outs += [results[q][n] for n in WEIGHTS]
    return tuple(outs)
```

```python
import functools

import jax
import jax.numpy as jnp
from jax import lax
from jax.experimental import pallas as pl
from jax.experimental.pallas import tpu as pltpu

F32 = jnp.float32
BF16 = jnp.bfloat16
EPS = 1e-6
HEAD = 128
N_HEADS = 8
N_GROUPS = 4
POOL_WINDOWS = (2, 4, 8, 16)
N_DEV = 8
ATTN_Q_BLOCK = 256

OFF_Q, OFF_K, OFF_V, OFF_F, OFF_GU, OFF_GV, OFF_XP, PROJ_PAD = 0, 1024, 2048, 3072, 3584, 4096, 4608, 5120
F_COLS = 8
PROJ_RAW = 4616

ADAM_LR, ADAM_B1, ADAM_B2, ADAM_EPS, ADAM_WD, ADAM_STEP = 0.001, 0.9, 0.999, 1e-08, 0.01, 10

BIG = ("w_in", "w_out", "w_ffn_gate", "w_ffn_up", "w_ffn_down", "w_ple_gate", "w_ple_proj")
TRANSPOSED = ("w_ffn_gate", "w_ffn_up")
SMALL = ("norm_mix", "q_norm", "k_norm", "forget_bias", "gmlp_v_norm", "gmlp_w_s", "gmlp_b_s", "pool_w",
         "pool_scale", "norm_ffn", "norm_ple")
WEIGHTS = ("norm_mix", "w_in", "q_norm", "k_norm", "forget_bias", "gmlp_v_norm", "gmlp_w_s", "gmlp_b_s", "pool_w",
           "pool_scale", "w_out", "norm_ffn", "w_ffn_gate", "w_ffn_up", "w_ffn_down", "norm_ple", "w_ple_gate",
           "w_ple_proj")

VMEM_LIMIT = 56 * 1024 * 1024

BS = pl.BlockSpec
SDS = jax.ShapeDtypeStruct
ANY = pl.BlockSpec(memory_space=pl.ANY)


def _params(*sem):
    return pltpu.CompilerParams(dimension_semantics=sem, vmem_limit_bytes=VMEM_LIMIT)


def _dot(a, b, mode="nn", precision=None):
    ca, cb = {"nn": (1, 0), "nt": (1, 1), "tn": (0, 0)}[mode]
    return lax.dot_general(a, b, (((ca,), (cb,)), ((), ())), preferred_element_type=F32, precision=precision)


def _rms_scale(x):
    return lax.rsqrt(jnp.mean(x * x, axis=-1, keepdims=True) + EPS)


def _rms_bwd(g, xhat, r):
    return r * (g - xhat * jnp.mean(g * xhat, axis=-1, keepdims=True))


def _gelu(x, with_grad=False):
    k = 0.7978845608028654
    inner = k * (x + 0.044715 * x * x * x)
    t = jnp.tanh(inner)
    y = 0.5 * x * (1.0 + t)
    if not with_grad:
        return y
    dy = 0.5 * (1.0 + t) + 0.5 * x * (1.0 - t * t) * k * (1.0 + 3.0 * 0.044715 * x * x)
    return y, dy


def _sigmoid(x):
    return 1.0 / (1.0 + jnp.exp(-x))


def _tile(n, want):
    t = min(n, want)
    assert n % t == 0, (n, want)
    return t


def _matmul(name, a, b, mode, out_dtype, tm, tn, res=None, after=None):
    if mode == "nn":
        (m, k), n = a.shape, b.shape[1]
        a_spec, b_spec = BS((tm, k), lambda i, j: (i, 0)), BS((k, tn), lambda i, j: (0, j))
    elif mode == "nt":
        (m, k), n = a.shape, b.shape[0]
        a_spec, b_spec = BS((tm, k), lambda i, j: (i, 0)), BS((tn, k), lambda i, j: (j, 0))
    else:
        (k, m), n = a.shape, b.shape[1]
        a_spec, b_spec = BS((k, tm), lambda i, j: (0, i)), BS((k, tn), lambda i, j: (0, j))
    assert m % tm == 0 and n % tn == 0
    o_spec = BS((tm, tn), lambda i, j: (i, j))

    def body(a_ref, b_ref, *rest):
        o_ref = rest[-1]
        acc = _dot(a_ref[...], b_ref[...], mode)
        if res is not None:
            acc = acc + rest[0][...]
        o_ref[...] = acc.astype(out_dtype)

    ins, specs = [a, b], [a_spec, b_spec]
    if res is not None:
        ins.append(res)
        specs.append(o_spec)
    if after is not None:
        ins.append(after)
        specs.append(BS(after.shape, lambda i, j: (0, 0)))
    return pl.pallas_call(body, out_shape=SDS((m, n), out_dtype), grid=(m // tm, n // tn), in_specs=specs,
                          out_specs=o_spec, name=name, compiler_params=_params("parallel", "parallel"))(*ins)


def _rms_fwd(name, h, gain):
    t, d = h.shape
    tm = _tile(t, 256)

    def body(h_ref, g_ref, o_ref):
        x = h_ref[...]
        o_ref[...] = (x * _rms_scale(x) * g_ref[...]).astype(BF16)

    return pl.pallas_call(body, out_shape=SDS((t, d), BF16), grid=(t // tm,),
                          in_specs=[BS((tm, d), lambda i: (i, 0)), BS((1, d), lambda i: (0, 0))],
                          out_specs=BS((tm, d), lambda i: (i, 0)), name=name, compiler_params=_params("parallel"))(h, gain)


def _rms_bwd_call(name, dxn, h, gain, dres):
    t, d = h.shape
    tm = _tile(t, 256)

    def body(dxn_ref, h_ref, g_ref, dres_ref, dh_ref, dhb_ref, dg_ref):
        x = h_ref[...]
        r = _rms_scale(x)
        xhat = x * r
        dy = dxn_ref[...]
        dh = dres_ref[...] + _rms_bwd(dy * g_ref[...], xhat, r)
        dh_ref[...] = dh
        dhb_ref[...] = dh.astype(BF16)

        @pl.when(pl.program_id(0) == 0)
        def _():
            dg_ref[...] = jnp.zeros_like(dg_ref)

        dg_ref[...] += jnp.sum(dy * xhat, axis=0, keepdims=True)

    row = BS((tm, d), lambda i: (i, 0))
    vec = BS((1, d), lambda i: (0, 0))
    return pl.pallas_call(body, out_shape=(SDS((t, d), F32), SDS((t, d), BF16), SDS((1, d), F32)), grid=(t // tm,),
                          in_specs=[row, row, vec, row], out_specs=(row, row, vec), name=name,
                          compiler_params=_params("arbitrary"))(dxn, h, gain, dres)


def _fgate_fwd(proj, bias):
    t = proj.shape[0]
    nb = t // HEAD

    def body(f_ref, b_ref, c_ref):
        tri = (lax.broadcasted_iota(jnp.int32, (HEAD, HEAD), 0) >= lax.broadcasted_iota(jnp.int32, (HEAD, HEAD), 1)).astype(F32)
        carry = jnp.zeros((1, HEAD), F32)
        for n in range(nb):
            rows = slice(n * HEAD, (n + 1) * HEAD)
            x = f_ref[rows, :] + b_ref[...]
            lf = jnp.minimum(x, 0.0) - jnp.log(1.0 + jnp.exp(-jnp.abs(x)))
            cb = _dot(tri, lf, precision=lax.Precision.HIGHEST) + carry
            c_ref[rows, :] = cb
            carry = cb[HEAD - 1:HEAD, :]

    return pl.pallas_call(body, out_shape=SDS((t, HEAD), F32), grid=(1,),
                          in_specs=[BS((t, HEAD), lambda i: (0, OFF_F // HEAD)), BS((1, HEAD), lambda i: (0, 0))],
                          out_specs=BS((t, HEAD), lambda i: (0, 0)), name="fgate_fwd",
                          compiler_params=_params("arbitrary"))(proj, bias)


def _fgate_bwd(dc, proj, bias):
    t = proj.shape[0]
    nb = t // HEAD
    width = OFF_GU - OFF_F

    def body(dc_ref, f_ref, b_ref, df_ref, db_ref):
        tri = (lax.broadcasted_iota(jnp.int32, (HEAD, HEAD), 0) <= lax.broadcasted_iota(jnp.int32, (HEAD, HEAD), 1)).astype(F32)
        carry = jnp.zeros((1, HEAD), F32)
        db = jnp.zeros((1, HEAD), F32)
        df_ref[:, HEAD:] = jnp.zeros((t, width - HEAD), BF16)
        for n in reversed(range(nb)):
            rows = slice(n * HEAD, (n + 1) * HEAD)
            dlf = _dot(tri, dc_ref[rows, :], precision=lax.Precision.HIGHEST) + carry
            carry = dlf[0:1, :]
            x = f_ref[rows, :] + b_ref[...]
            df = dlf * _sigmoid(-x)
            df_ref[rows, 0:HEAD] = df.astype(BF16)
            db = db + jnp.sum(df, axis=0, keepdims=True)
        db_ref[...] = db

    return pl.pallas_call(body, out_shape=(SDS((t, width), BF16), SDS((1, HEAD), F32)), grid=(1,),
                          in_specs=[BS((t, HEAD), lambda i: (0, 0)), BS((t, HEAD), lambda i: (0, OFF_F // HEAD)),
                                    BS((1, HEAD), lambda i: (0, 0))],
                          out_specs=(BS((t, width), lambda i: (0, 0)), BS((1, HEAD), lambda i: (0, 0))),
                          name="fgate_bwd", compiler_params=_params("arbitrary"))(dc, proj, bias)


def _attn_masked_logits(qs_ref, kn_ref, cq_ref, ck_ref, i, tq):
    lo, hi = i * tq, (i + 1) * tq
    s = _dot(qs_ref[lo:hi, :], kn_ref[0:hi, :], "nt")
    s = s + cq_ref[0, lo:hi, :] - ck_ref[0, :, 0:hi]
    row = lax.broadcasted_iota(jnp.int32, (tq, hi), 0) + lo
    col = lax.broadcasted_iota(jnp.int32, (tq, hi), 1)
    return s, row >= col


def _attn_fwd(proj, cq, ck, qg, kg):
    t = proj.shape[0]
    tq = _tile(t, ATTN_Q_BLOCK)
    nq = t // tq
    scale = HEAD ** -0.5

    def body(q_ref, k_ref, v_ref, cq_ref, ck_ref, qg_ref, kg_ref, o_ref, lse_ref, qs_ref, kn_ref, vb_ref):
        q = q_ref[...]
        k = k_ref[...]
        qs_ref[...] = (q * _rms_scale(q) * qg_ref[...] * scale).astype(BF16)
        kn_ref[...] = (k * _rms_scale(k) * kg_ref[...]).astype(BF16)
        vb_ref[...] = v_ref[...].astype(BF16)
        for i in range(nq):
            lo, hi = i * tq, (i + 1) * tq
            s, keep = _attn_masked_logits(qs_ref, kn_ref, cq_ref, ck_ref, i, tq)
            s = jnp.where(keep, s, -1e30)
            m = jnp.max(s, axis=-1, keepdims=True)
            e = jnp.exp(s - m)
            l = jnp.sum(e, axis=-1, keepdims=True)
            o = _dot(e.astype(BF16), vb_ref[0:hi, :]) / l
            o_ref[lo:hi, :] = o.astype(BF16)
            lse_ref[0, lo:hi, :] = m + jnp.log(l)

    def col(off):
        return BS((t, HEAD), lambda h: (0, off // HEAD + h))

    vec = BS((1, HEAD), lambda h: (0, 0))
    return pl.pallas_call(
        body, out_shape=(SDS((t, N_HEADS * HEAD), BF16), SDS((N_HEADS, t, 1), F32)), grid=(N_HEADS,),
        in_specs=[col(OFF_Q), col(OFF_K), col(OFF_V), BS((1, t, 1), lambda h: (h, 0, 0)), BS((1, 1, t), lambda h: (h, 0, 0)),
                  vec, vec],
        out_specs=(BS((t, HEAD), lambda h: (0, h)), BS((1, t, 1), lambda h: (h, 0, 0))),
        scratch_shapes=[pltpu.VMEM((t, HEAD), BF16)] * 3, name="attn_fwd",
        compiler_params=_params("parallel"))(proj, proj, proj, cq, ck, qg, kg)


def _attn_bwd(proj, cq, ck, lse, dmix, qg, kg):
    t = proj.shape[0]
    tq = _tile(t, ATTN_Q_BLOCK)
    nq = t // tq
    scale = HEAD ** -0.5

    def body(q_ref, k_ref, v_ref, cq_ref, ck_ref, lse_ref, do_ref, qg_ref, kg_ref,
             dq_ref, dk_ref, dv_ref, dcq_ref, dck_ref, dqg_ref, dkg_ref,
             qs_ref, kn_ref, vb_ref, dob_ref, dqs_ref, dkn_ref, dva_ref):
        q = q_ref[...]
        k = k_ref[...]
        rq = _rms_scale(q)
        rk = _rms_scale(k)
        qs_ref[...] = (q * rq * qg_ref[...] * scale).astype(BF16)
        kn_ref[...] = (k * rk * kg_ref[...]).astype(BF16)
        vb_ref[...] = v_ref[...].astype(BF16)
        dob_ref[...] = do_ref[...].astype(BF16)
        dkn_ref[...] = jnp.zeros_like(dkn_ref)
        dva_ref[...] = jnp.zeros_like(dva_ref)
        dck_ref[...] = jnp.zeros_like(dck_ref)
        for i in range(nq):
            lo, hi = i * tq, (i + 1) * tq
            s, keep = _attn_masked_logits(qs_ref, kn_ref, cq_ref, ck_ref, i, tq)
            pr = jnp.where(keep, jnp.exp(s - lse_ref[0, lo:hi, :]), 0.0)
            dp = _dot(dob_ref[lo:hi, :], vb_ref[0:hi, :], "nt")
            delta = jnp.sum(pr * dp, axis=-1, keepdims=True)
            ds = pr * (dp - delta)
            dcq_ref[0, lo:hi, :] = jnp.sum(ds, axis=-1, keepdims=True)
            dck_ref[0, :, 0:hi] += -jnp.sum(ds, axis=0, keepdims=True)
            dsb = ds.astype(BF16)
            dqs_ref[lo:hi, :] = _dot(dsb, kn_ref[0:hi, :])
            dkn_ref[0:hi, :] += _dot(dsb, qs_ref[lo:hi, :], "tn")
            dva_ref[0:hi, :] += _dot(pr.astype(BF16), dob_ref[lo:hi, :], "tn")
        dv_ref[...] = dva_ref[...].astype(BF16)

        @pl.when(pl.program_id(0) == 0)
        def _():
            dqg_ref[...] = jnp.zeros_like(dqg_ref)
            dkg_ref[...] = jnp.zeros_like(dkg_ref)

        qhat = q * rq
        dqn = dqs_ref[...] * scale
        dqg_ref[...] += jnp.sum(dqn * qhat, axis=0, keepdims=True)
        dq_ref[...] = _rms_bwd(dqn * qg_ref[...], qhat, rq).astype(BF16)
        khat = k * rk
        dkn = dkn_ref[...]
        dkg_ref[...] += jnp.sum(dkn * khat, axis=0, keepdims=True)
        dk_ref[...] = _rms_bwd(dkn * kg_ref[...], khat, rk).astype(BF16)

    def col(off):
        return BS((t, HEAD), lambda h: (0, off // HEAD + h))

    vec = BS((1, HEAD), lambda h: (0, 0))
    c_col = BS((1, t, 1), lambda h: (h, 0, 0))
    c_row = BS((1, 1, t), lambda h: (h, 0, 0))
    head_out = BS((t, HEAD), lambda h: (0, h))
    width = N_HEADS * HEAD
    return pl.pallas_call(
        body,
        out_shape=(SDS((t, width), BF16), SDS((t, width), BF16), SDS((t, width), BF16), SDS((N_HEADS, t, 1), F32),
                   SDS((N_HEADS, 1, t), F32), SDS((1, HEAD), F32), SDS((1, HEAD), F32)),
        grid=(N_HEADS,),
        in_specs=[col(OFF_Q), col(OFF_K), col(OFF_V), c_col, c_row, c_col, head_out, vec, vec],
        out_specs=(head_out, head_out, head_out, c_col, c_row, vec, vec),
        scratch_shapes=[pltpu.VMEM((t, HEAD), BF16)] * 4 + [pltpu.VMEM((t, HEAD), F32)] * 3, name="attn_bwd",
        compiler_params=_params("arbitrary"))(proj, proj, proj, cq, ck, lse, dmix, qg, kg)


def _group_cols(off):
    width = N_GROUPS * HEAD
    return lambda t: BS((t, width), lambda i: (0, off // width))


def _tril():
    return lax.broadcasted_iota(jnp.int32, (HEAD, HEAD), 0) >= lax.broadcasted_iota(jnp.int32, (HEAD, HEAD), 1)


def _gmlp_fwd(proj, gain, w_s, b_s):
    t = proj.shape[0]
    width = N_GROUPS * HEAD
    nc = t // HEAD

    def body(gu_ref, gv_ref, gain_ref, ws_ref, bs_ref, o_ref):
        tril = _tril()
        for g in range(N_GROUPS):
            cols = slice(g * HEAD, (g + 1) * HEAD)
            u = _gelu(gu_ref[:, cols])
            vv = _gelu(gv_ref[:, cols])
            vn = (vv * _rms_scale(vv) * gain_ref[:, cols]).astype(BF16)
            w = jnp.where(tril, ws_ref[g], 0.0).astype(BF16)
            for n in range(nc):
                rows = slice(n * HEAD, (n + 1) * HEAD)
                mixed = _dot(w, vn[rows]) + bs_ref[g]
                o_ref[rows, cols] = (u[rows] * mixed).astype(BF16)

    full = lambda shape: BS(shape, lambda i: (0,) * len(shape))
    return pl.pallas_call(body, out_shape=SDS((t, width), BF16), grid=(1,),
                          in_specs=[_group_cols(OFF_GU)(t), _group_cols(OFF_GV)(t), full((1, width)),
                                    full((N_GROUPS, HEAD, HEAD)), full((N_GROUPS, HEAD, 1))],
                          out_specs=full((t, width)), name="gmlp_fwd",
                          compiler_params=_params("arbitrary"))(proj, proj, gain, w_s, b_s)


def _gmlp_bwd(dmix, proj, gain, w_s, b_s):
    t = proj.shape[0]
    width = N_GROUPS * HEAD
    nc = t // HEAD

    def body(dy_ref, gu_ref, gv_ref, gain_ref, ws_ref, bs_ref, dgu_ref, dgv_ref, dgain_ref, dws_ref, dbs_ref, dvn_ref):
        tril = _tril()
        for g in range(N_GROUPS):
            cols = slice(g * HEAD, (g + 1) * HEAD)
            u, du = _gelu(gu_ref[:, cols], with_grad=True)
            vv, dvv = _gelu(gv_ref[:, cols], with_grad=True)
            r = _rms_scale(vv)
            vhat = vv * r
            gain_g = gain_ref[:, cols]
            vn = (vhat * gain_g).astype(BF16)
            w = jnp.where(tril, ws_ref[g], 0.0).astype(BF16)
            dws = jnp.zeros((HEAD, HEAD), F32)
            dbs = jnp.zeros((HEAD, 1), F32)
            for n in range(nc):
                rows = slice(n * HEAD, (n + 1) * HEAD)
                mixed = _dot(w, vn[rows]) + bs_ref[g]
                dy = dy_ref[rows, cols]
                dgu_ref[rows, cols] = (dy * mixed * du[rows]).astype(BF16)
                dm = dy * u[rows]
                dmb = dm.astype(BF16)
                dbs = dbs + jnp.sum(dm, axis=-1, keepdims=True)
                dws = dws + _dot(dmb, vn[rows], "nt")
                dvn_ref[rows, :] = _dot(w, dmb, "tn")
            dvn = dvn_ref[...]
            dgain_ref[:, cols] = jnp.sum(dvn * vhat, axis=0, keepdims=True)
            dgv_ref[:, cols] = (_rms_bwd(dvn * gain_g, vhat, r) * dvv).astype(BF16)
            dws_ref[g] = jnp.where(tril, dws, 0.0)
            dbs_ref[g] = dbs

    full = lambda shape: BS(shape, lambda i: (0,) * len(shape))
    return pl.pallas_call(
        body,
        out_shape=(SDS((t, width), BF16), SDS((t, width), BF16), SDS((1, width), F32), SDS((N_GROUPS, HEAD, HEAD), F32),
                   SDS((N_GROUPS, HEAD, 1), F32)),
        grid=(1,),
        in_specs=[BS((t, width), lambda i: (0, 2)), _group_cols(OFF_GU)(t), _group_cols(OFF_GV)(t), full((1, width)),
                  full((N_GROUPS, HEAD, HEAD)), full((N_GROUPS, HEAD, 1))],
        out_specs=(full((t, width)), full((t, width)), full((1, width)), full((N_GROUPS, HEAD, HEAD)),
                   full((N_GROUPS, HEAD, 1))),
        scratch_shapes=[pltpu.VMEM((t, HEAD), F32)], name="gmlp_bwd",
        compiler_params=_params("arbitrary"))(dmix, proj, proj, gain, w_s, b_s)


def _pool_window_mean_minus_x(x, window, t_idx):
    s, span = x, 1
    while span < window:
        s = s + jnp.where(t_idx >= span, pltpu.roll(s, span, 0), 0.0)
        span *= 2
    cnt = jnp.minimum(t_idx + 1, window).astype(F32)
    return s / cnt - x, cnt


def _pool_fwd(proj, w_pool, scale):
    t = proj.shape[0]
    width = N_GROUPS * HEAD

    def body(x_ref, w_ref, sc_ref, o_ref):
        t_idx = lax.broadcasted_iota(jnp.int32, (t, HEAD), 0)
        for g in range(N_GROUPS):
            cols = slice(g * HEAD, (g + 1) * HEAD)
            d, _ = _pool_window_mean_minus_x(x_ref[:, cols], POOL_WINDOWS[g], t_idx)
            y = _dot(d.astype(BF16), w_ref[g].astype(BF16)) * sc_ref[:, cols]
            o_ref[:, cols] = y.astype(BF16)

    full = lambda shape: BS(shape, lambda i: (0,) * len(shape))
    return pl.pallas_call(body, out_shape=SDS((t, width), BF16), grid=(1,),
                          in_specs=[_group_cols(OFF_XP)(t), full((N_GROUPS, HEAD, HEAD)), full((1, width))],
                          out_specs=full((t, width)), name="pool_fwd",
                          compiler_params=_params("arbitrary"))(proj, w_pool, scale)


def _pool_bwd(dmix, proj, w_pool, scale):
    t = proj.shape[0]
    width = N_GROUPS * HEAD

    def body(dy_ref, x_ref, w_ref, sc_ref, dx_ref, dw_ref, dsc_ref):
        t_idx = lax.broadcasted_iota(jnp.int32, (t, HEAD), 0)
        for g in range(N_GROUPS):
            cols = slice(g * HEAD, (g + 1) * HEAD)
            window = POOL_WINDOWS[g]
            d, cnt = _pool_window_mean_minus_x(x_ref[:, cols], window, t_idx)
            db = d.astype(BF16)
            wb = w_ref[g].astype(BF16)
            dy = dy_ref[:, cols]
            dsc_ref[:, cols] = jnp.sum(dy * _dot(db, wb), axis=0, keepdims=True)
            dyw = (dy * sc_ref[:, cols]).astype(BF16)
            dw_ref[g] = _dot(db, dyw, "tn")
            dd = _dot(dyw, wb, "nt")
            rsum, span = dd / cnt, 1
            while span < window:
                rsum = rsum + jnp.where(t_idx < t - span, pltpu.roll(rsum, t - span, 0), 0.0)
                span *= 2
            dx_ref[:, cols] = (rsum - dd).astype(BF16)

    full = lambda shape: BS(shape, lambda i: (0,) * len(shape))
    return pl.pallas_call(
        body, out_shape=(SDS((t, width), BF16), SDS((N_GROUPS, HEAD, HEAD), F32), SDS((1, width), F32)), grid=(1,),
        in_specs=[BS((t, width), lambda i: (0, 3)), _group_cols(OFF_XP)(t), full((N_GROUPS, HEAD, HEAD)), full((1, width))],
        out_specs=(full((t, width)), full((N_GROUPS, HEAD, HEAD)), full((1, width))), name="pool_bwd",
        compiler_params=_params("arbitrary"))(dmix, proj, w_pool, scale)


def _ffn_fwd(xn, wg, wu):
    t, d = xn.shape
    fs = wg.shape[1]
    tm = _tile(t, 512)

    def body(x_ref, wg_ref, wu_ref, a_ref, b_ref, hh_ref):
        x = x_ref[...]
        a = _dot(x, wg_ref[0], "nt")
        b = _dot(x, wu_ref[0], "nt")
        a_ref[0] = a
        b_ref[0] = b
        hh_ref[0] = (a * _sigmoid(a) * b).astype(BF16)

    w_spec = BS((1, fs, d), lambda j, i: (j, 0, 0))
    o_spec = BS((1, tm, fs), lambda j, i: (j, i, 0))
    return pl.pallas_call(body, out_shape=(SDS((N_DEV, t, fs), F32), SDS((N_DEV, t, fs), F32), SDS((N_DEV, t, fs), BF16)),
                          grid=(N_DEV, t // tm), in_specs=[BS((tm, d), lambda j, i: (i, 0)), w_spec, w_spec],
                          out_specs=(o_spec, o_spec, o_spec), name="ffn_fwd",
                          compiler_params=_params("parallel", "parallel"))(xn, wg, wu)


def _ffn_down(hh, wd, res):
    _, t, fs = hh.shape
    d = wd.shape[2]
    tm, tn = _tile(t, 1024), _tile(d, 1024)

    def body(a_ref, b_ref, r_ref, o_ref, acc_ref):
        k = pl.program_id(2)

        @pl.when(k == 0)
        def _():
            acc_ref[...] = r_ref[...]

        acc_ref[...] += _dot(a_ref[0], b_ref[0])

        @pl.when(k == N_DEV - 1)
        def _():
            o_ref[...] = acc_ref[...]

    o_spec = BS((tm, tn), lambda i, j, k: (i, j))
    return pl.pallas_call(body, out_shape=SDS((t, d), F32), grid=(t // tm, d // tn, N_DEV),
                          in_specs=[BS((1, tm, fs), lambda i, j, k: (k, i, 0)), BS((1, fs, tn), lambda i, j, k: (k, 0, j)), o_spec],
                          out_specs=o_spec, scratch_shapes=[pltpu.VMEM((tm, tn), F32)], name="ffn_down",
                          compiler_params=_params("parallel", "parallel", "arbitrary"))(hh, wd, res)


def _ffn_bwd_hidden(dh_bf, wd, a, b):
    t, d = dh_bf.shape
    fs = wd.shape[1]
    tm = _tile(t, 512)

    def body(dh_ref, wd_ref, a_ref, b_ref, da_ref, db_ref):
        dhh = _dot(dh_ref[...], wd_ref[0], "nt")
        av = a_ref[0]
        sig = _sigmoid(av)
        da_ref[0] = (dhh * b_ref[0] * sig * (1.0 + av * (1.0 - sig))).astype(BF16)
        db_ref[0] = (dhh * av * sig).astype(BF16)

    o_spec = BS((1, tm, fs), lambda j, i: (j, i, 0))
    return pl.pallas_call(body, out_shape=(SDS((N_DEV, t, fs), BF16), SDS((N_DEV, t, fs), BF16)), grid=(N_DEV, t // tm),
                          in_specs=[BS((tm, d), lambda j, i: (i, 0)), BS((1, fs, d), lambda j, i: (j, 0, 0)), o_spec, o_spec],
                          out_specs=(o_spec, o_spec), name="ffn_bwd_hidden",
                          compiler_params=_params("parallel", "parallel"))(dh_bf, wd, a, b)


def _ffn_dwd(hh, dh_bf):
    _, t, fs = hh.shape
    d = dh_bf.shape[1]
    tn = _tile(d, 1024)

    def body(a_ref, b_ref, o_ref):
        o_ref[0] = _dot(a_ref[0], b_ref[...], "tn").astype(BF16)

    return pl.pallas_call(body, out_shape=SDS((N_DEV, fs, d), BF16), grid=(N_DEV, d // tn),
                          in_specs=[BS((1, t, fs), lambda j, n: (j, 0, 0)), BS((t, tn), lambda j, n: (0, n))],
                          out_specs=BS((1, fs, tn), lambda j, n: (j, 0, n)), name="ffn_dwd",
                          compiler_params=_params("parallel", "parallel"))(hh, dh_bf)


def _ffn_dwgu(xn, da, db):
    t, d = xn.shape
    fs = da.shape[2]
    tn = _tile(d, 1024)

    def body(x_ref, da_ref, db_ref, dg_ref, du_ref):
        x = x_ref[...]
        dg_ref[0] = _dot(da_ref[0], x, "tn").astype(BF16)
        du_ref[0] = _dot(db_ref[0], x, "tn").astype(BF16)

    g_spec = BS((1, t, fs), lambda j, i: (j, 0, 0))
    o_spec = BS((1, fs, tn), lambda j, i: (j, 0, i))
    return pl.pallas_call(body, out_shape=(SDS((N_DEV, fs, d), BF16), SDS((N_DEV, fs, d), BF16)), grid=(N_DEV, d // tn),
                          in_specs=[BS((t, tn), lambda j, i: (0, i)), g_spec, g_spec], out_specs=(o_spec, o_spec),
                          name="ffn_dwgu", compiler_params=_params("parallel", "parallel"))(xn, da, db)


def _ffn_dxn(da, db, wg, wu):
    _, t, fs = da.shape
    d = wg.shape[2]
    tm, tn = _tile(t, 1024), _tile(d, 1024)

    def body(da_ref, db_ref, wg_ref, wu_ref, o_ref, acc_ref):
        k = pl.program_id(2)

        @pl.when(k == 0)
        def _():
            acc_ref[...] = jnp.zeros_like(acc_ref)

        acc_ref[...] += _dot(da_ref[0], wg_ref[0]) + _dot(db_ref[0], wu_ref[0])

        @pl.when(k == N_DEV - 1)
        def _():
            o_ref[...] = acc_ref[...]

    g_spec = BS((1, tm, fs), lambda i, j, k: (k, i, 0))
    w_spec = BS((1, fs, tn), lambda i, j, k: (k, 0, j))
    return pl.pallas_call(body, out_shape=SDS((t, d), F32), grid=(t // tm, d // tn, N_DEV),
                          in_specs=[g_spec, g_spec, w_spec, w_spec], out_specs=BS((tm, tn), lambda i, j, k: (i, j)),
                          scratch_shapes=[pltpu.VMEM((tm, tn), F32)], name="ffn_dxn",
                          compiler_params=_params("parallel", "parallel", "arbitrary"))(da, db, wg, wu)


def _ple_fwd(xn, wpg, p_bf, wpp, h):
    t, d = xn.shape
    dp = p_bf.shape[1]
    tn = wpp.shape[2]
    tm = _tile(t, 1024)

    def body(x_ref, wg_ref, p_ref, wp_ref, h_ref, o_ref, z_ref, pp_ref):
        z = _dot(x_ref[...], wg_ref[...])
        pp = _dot(p_ref[...], wp_ref[0])
        z_ref[...] = z
        pp_ref[...] = pp
        o_ref[...] = h_ref[...] + pp * _sigmoid(z)

    o_spec = BS((tm, tn), lambda i, j: (i, j))
    out = SDS((t, d), F32)
    return pl.pallas_call(body, out_shape=(out, out, out), grid=(t // tm, N_DEV),
                          in_specs=[BS((tm, d), lambda i, j: (i, 0)), BS((d, tn), lambda i, j: (0, j)),
                                    BS((tm, dp), lambda i, j: (i, 0)), BS((1, dp, tn), lambda i, j: (j, 0, 0)), o_spec],
                          out_specs=(o_spec, o_spec, o_spec), name="ple_fwd",
                          compiler_params=_params("parallel", "parallel"))(xn, wpg, p_bf, wpp, h)


def _ple_bwd_gate(dh, z, pp, after):
    t, d = dh.shape
    tm = _tile(t, 256)

    def body(dh_ref, z_ref, pp_ref, after_ref, dpp_ref, dz_ref):
        g = _sigmoid(z_ref[...])
        dh_v = dh_ref[...]
        dpp_ref[...] = (dh_v * g).astype(BF16)
        dz_ref[...] = (dh_v * pp_ref[...] * g * (1.0 - g)).astype(BF16)

    row = BS((tm, d), lambda i: (i, 0))
    return pl.pallas_call(body, out_shape=(SDS((t, d), BF16), SDS((t, d), BF16)), grid=(t // tm,),
                          in_specs=[row, row, row, BS(after.shape, lambda i: (0, 0))], out_specs=(row, row),
                          name="ple_bwd_gate", compiler_params=_params("parallel"))(dh, z, pp, after)


def _ple_dwpp(p_bf, dpp):
    t, dp = p_bf.shape
    tn = dpp.shape[1] // N_DEV

    def body(p_ref, g_ref, o_ref):
        o_ref[0] = _dot(p_ref[...], g_ref[...], "tn").astype(BF16)

    return pl.pallas_call(body, out_shape=SDS((N_DEV, dp, tn), BF16), grid=(N_DEV,),
                          in_specs=[BS((t, dp), lambda j: (0, 0)), BS((t, tn), lambda j: (0, j))],
                          out_specs=BS((1, dp, tn), lambda j: (j, 0, 0)), name="ple_dwpp",
                          compiler_params=_params("parallel"))(p_bf, dpp)


def _loss_call(y, target):
    t, d = y.shape
    tm = _tile(t, 256)

    def body(y_ref, t_ref, dy_ref, loss_ref):
        diff = y_ref[...] - t_ref[...]
        dy_ref[...] = diff * (1.0 / d)

        @pl.when(pl.program_id(0) == 0)
        def _():
            loss_ref[...] = jnp.zeros_like(loss_ref)

        loss_ref[...] += 0.5 * jnp.sum(jnp.mean(diff * diff, axis=-1, keepdims=True), axis=0, keepdims=True)

    row = BS((tm, d), lambda i: (i, 0))
    return pl.pallas_call(body, out_shape=(SDS((t, d), F32), SDS((1, 1), F32)), grid=(t // tm,), in_specs=[row, row],
                          out_specs=(row, BS((1, 1), lambda i: (0, 0))), name="loss",
                          compiler_params=_params("arbitrary"))(y, target)


def _mesh_pos():
    return lax.axis_index("x"), lax.axis_index("y"), lax.axis_index("c")


def _dev_index(px, py, pc):
    return 4 * px + 2 * py + pc


HBM = pl.BlockSpec(memory_space=pltpu.HBM)
SEM = pl.BlockSpec(memory_space=pltpu.SEMAPHORE)
DATAFLOW = pltpu.SideEffectType.DATAFLOW_SIDE_EFFECTING


def _in_hbm(arrs):
    return [pltpu.with_memory_space_constraint(a, pltpu.HBM) for a in arrs]


def _other_chips(x, y):
    return [(1 - x, y), (x, 1 - y), (1 - x, 1 - y)]


def _split_start(name, srcs, lands, n_sems, copies, after=None):
    n = len(srcs)
    after = [] if after is None else [after]
    n_in = 2 * n + len(after)

    def body(*refs):
        for cp in copies(refs[:n], refs[n:2 * n], refs[n_in], refs[n_in + 1]):
            cp.start()
        token = refs[-1]
        token[...] = jnp.zeros_like(token)

    thru = [pltpu.HBM(a.shape, a.dtype) for a in list(srcs) + list(lands)]
    outs = pl.pallas_call(
        body, name=name,
        out_shape=(pltpu.SemaphoreType.DMA((n * n_sems,)), pltpu.SemaphoreType.DMA((n * n_sems,)), *thru, SDS((8, HEAD), F32)),
        in_specs=[HBM] * (2 * n) + [ANY] * len(after),
        out_specs=(SEM, SEM, *([HBM] * (2 * n)), pl.BlockSpec(memory_space=pltpu.VMEM)),
        input_output_aliases={q: 2 + q for q in range(2 * n)},
        compiler_params=pltpu.CompilerParams(has_side_effects=DATAFLOW))(*_in_hbm(list(srcs) + list(lands)), *after)
    return outs[0], outs[1], list(outs[2:2 + n]), list(outs[2 + n:2 + 2 * n]), outs[-1]


def _split_wait(name, srcs, lands, send_sems, recv_sems, after, copies):
    n = len(srcs)
    after = list(after) if isinstance(after, (list, tuple)) else [after]

    def body(*refs):
        for cp in copies(refs[:n], refs[n:2 * n], refs[2 * n], refs[2 * n + 1]):
            cp.wait_send()
            cp.wait_recv()

    thru = [pltpu.HBM(a.shape, a.dtype) for a in list(srcs) + list(lands)]
    outs = pl.pallas_call(
        body, name=name, out_shape=tuple(thru), in_specs=[HBM] * (2 * n) + [SEM, SEM] + [ANY] * len(after),
        out_specs=tuple([HBM] * (2 * n)), input_output_aliases={q: q for q in range(2 * n)},
        compiler_params=pltpu.CompilerParams(has_side_effects=DATAFLOW))(*list(srcs), *list(lands), send_sems, recv_sems, *after)
    return list(outs[:n]), list(outs[n:])


def _gather_ici_copies(layer, waiting):
    def copies(src_refs, land_refs, send_sems, recv_sems):
        x, y, c = _mesh_pos()
        out = []
        for a in range(len(src_refs)):
            for j, chip in enumerate(_other_chips(x, y)):
                slot = _dev_index(*chip, c) if waiting else _dev_index(x, y, c)
                out.append(pltpu.make_async_remote_copy(
                    src_ref=src_refs[a].at[layer], dst_ref=land_refs[a].at[slot], send_sem=send_sems.at[3 * a + j],
                    recv_sem=recv_sems.at[3 * a + j], device_id=(*chip, c), device_id_type=pl.DeviceIdType.MESH))
        return out
    return copies


def _slab_tile(r, c):
    for cand in (512, 256, 176, 128, 64, 16, 8):
        if r % cand == 0:
            return cand, c
    return r, 256


def _place_own(pos, src, layer, land):
    _, r, c = land.shape
    tr, tc = _slab_tile(r, c)

    def body(pos_ref, s_ref, l_ref, o_ref):
        o_ref[...] = s_ref[...]

    grid_spec = pltpu.PrefetchScalarGridSpec(
        num_scalar_prefetch=1, grid=(r // tr, c // tc), in_specs=[BS((1, tr, tc), lambda i, j, pos: (layer, i, j)), ANY],
        out_specs=BS((1, tr, tc), lambda i, j, pos: (pos[0], i, j)))
    return pl.pallas_call(body, grid_spec=grid_spec, out_shape=SDS(land.shape, land.dtype), input_output_aliases={2: 0},
                          name="place_own", compiler_params=_params("parallel", "parallel"))(pos, src, land)


def _gather_sibling_copies(layer, waiting):
    def copies(src_refs, land_refs, send_sems, recv_sems):
        x, y, c = _mesh_pos()
        blocks = [(x, y)] + _other_chips(x, y)
        out = []
        for a in range(len(src_refs)):
            for k in range(4):
                slot = land_refs[a].at[_dev_index(*blocks[k], 1 - c if waiting else c)]
                src = src_refs[a].at[layer] if (k == 0 and not waiting) else slot
                out.append(pltpu.make_async_remote_copy(
                    src_ref=src, dst_ref=slot, send_sem=send_sems.at[4 * a + k], recv_sem=recv_sems.at[4 * a + k],
                    device_id=(x, y, 1 - c), device_id_type=pl.DeviceIdType.MESH))
        return out
    return copies


def _gather_d2d(name, srcs, layer, lands):
    n = len(srcs)

    def body(*refs):
        src_refs, land_refs = refs[:n], refs[n:2 * n]
        send_sems, recv_sems = refs[3 * n:]
        sends = _gather_sibling_copies(layer, False)(src_refs, land_refs, send_sems, recv_sems)
        for cp in sends:
            cp.start()
        for cp in _gather_sibling_copies(layer, True)(src_refs, land_refs, send_sems, recv_sems):
            cp.wait_recv()
        for cp in sends:
            cp.wait_send()

    outs = pl.pallas_call(
        body, name=name, out_shape=tuple(SDS(l.shape, l.dtype) for l in lands), in_specs=[ANY] * (2 * n),
        out_specs=tuple([ANY] * n), input_output_aliases={n + q: q for q in range(n)},
        scratch_shapes=[pltpu.SemaphoreType.DMA((4 * n,)), pltpu.SemaphoreType.DMA((4 * n,))],
        compiler_params=pltpu.CompilerParams(has_side_effects=True))(*srcs, *lands)
    return list(outs)


def _sibling_copies(src_refs, land_refs, send_sems, recv_sems):
    x, y, c = _mesh_pos()
    out = []
    for a in range(len(src_refs)):
        whole = len(src_refs[a].shape) == 2
        for q in range(1 if whole else 4):
            src = src_refs[a] if whole else src_refs[a].at[2 * q + (1 - c)]
            dst = land_refs[a] if whole else land_refs[a].at[q]
            out.append(pltpu.make_async_remote_copy(
                src_ref=src, dst_ref=dst, send_sem=send_sems.at[4 * a + q], recv_sem=recv_sems.at[4 * a + q],
                device_id=(x, y, 1 - c), device_id_type=pl.DeviceIdType.MESH))
    return out


def _chip_sums(pos, own, got):
    whole = own.ndim == 2
    r, c = own.shape[-2:]
    tr, tc = _slab_tile(r, c)

    def body(pos_ref, own_ref, got_ref, o_ref):
        mine = own_ref[...] if whole else own_ref[0]
        o_ref[...] = (mine.astype(F32) + got_ref[...].astype(F32)).astype(o_ref.dtype)

    if whole:
        grid = (r // tr, c // tc)
        blk = BS((tr, tc), lambda i, j, pos: (i, j))
        specs, o_spec, ins, out = [blk, blk], blk, [own, got], SDS((r, c), own.dtype)
    else:
        grid = (N_CHIPS, r // tr, c // tc)
        blk = BS((1, tr, tc), lambda q, i, j, pos: (q, i, j))
        specs = [BS((1, 1, tr, tc), lambda q, i, j, pos: (q, pos[2], i, j)), blk]
        o_spec, ins, out = blk, [own.reshape(N_CHIPS, 2, r, c), got], SDS((N_CHIPS, r, c), own.dtype)
    grid_spec = pltpu.PrefetchScalarGridSpec(num_scalar_prefetch=1, grid=grid, in_specs=specs, out_specs=o_spec)
    return pl.pallas_call(body, out_shape=out, grid_spec=grid_spec, name="chip_sums",
                          compiler_params=_params(*["parallel"] * len(grid)))(pos, *ins)


def _chip_copies(waiting):
    def copies(src_refs, land_refs, send_sems, recv_sems):
        x, y, c = _mesh_pos()
        out = []
        for a in range(len(src_refs)):
            whole = len(src_refs[a].shape) == 2
            for j, (qx, qy) in enumerate(_other_chips(x, y)):
                src = src_refs[a] if whole else src_refs[a].at[2 * qx + qy]
                slot = 2 * qx + qy if waiting else 2 * x + y
                out.append(pltpu.make_async_remote_copy(
                    src_ref=src, dst_ref=land_refs[a].at[slot], send_sem=send_sems.at[3 * a + j],
                    recv_sem=recv_sems.at[3 * a + j], device_id=(qx, qy, c), device_id_type=pl.DeviceIdType.MESH))
        return out
    return copies


def _adamw_math(w, g, m, v):
    m = ADAM_B1 * m + (1.0 - ADAM_B1) * g
    v = ADAM_B2 * v + (1.0 - ADAM_B2) * (g * g)
    m_hat = m / (1.0 - ADAM_B1 ** ADAM_STEP)
    v_hat = v / (1.0 - ADAM_B2 ** ADAM_STEP)
    delta = -ADAM_LR * (m_hat / (jnp.sqrt(v_hat) + ADAM_EPS) + ADAM_WD * w)
    return delta, m, v


N_CHIPS = 4


def _sum_partials(my_chip, own, parts_ref):
    g = None
    for q in range(N_CHIPS):
        term = jnp.where(my_chip == q, own, parts_ref[q].astype(F32))
        g = term if g is None else g + term
    return g


def _adamw_layer(name, pos, parts, own, w, m, v, layer, prev):
    depth, r, c = w.shape
    tr = next(cand for cand in (256, 128, 64, 32, 16, 8) if r % cand == 0)

    def body(pos_ref, parts_ref, own_ref, w_ref, m_ref, v_ref, *rest):
        g_ref, d_ref, nm_ref, nv_ref = rest[-4:]
        g = _sum_partials(pos_ref[1], own_ref[0].astype(F32), parts_ref)
        delta, nm, nv = _adamw_math(w_ref[0], g, m_ref[0], v_ref[0])
        g_ref[0] = g
        d_ref[0] = delta
        nm_ref[0] = nm
        nv_ref[0] = nv

    lay = BS((1, tr, c), lambda i, pos: (layer, i, 0))
    stacked = SDS((depth, r, c), F32)
    ins = [pos, parts, own, w, m, v]
    specs = [BS((N_CHIPS, tr, c), lambda i, pos: (0, i, 0)), BS((1, tr, c), lambda i, pos: (pos[1], i, 0)), lay, lay, lay]
    aliases = {}
    if prev is not None:
        ins += list(prev)
        specs += [ANY] * 4
        aliases = {6 + q: q for q in range(4)}
    grid_spec = pltpu.PrefetchScalarGridSpec(num_scalar_prefetch=1, grid=(r // tr,), in_specs=specs, out_specs=(lay,) * 4)
    return pl.pallas_call(body, out_shape=(stacked,) * 4, grid_spec=grid_spec, input_output_aliases=aliases, name=name,
                          compiler_params=_params("parallel"))(*ins)


def _sum_chips(pos, parts, own):
    _, r, c = parts.shape
    tr, tc = _slab_tile(r, c)

    def body(pos_ref, parts_ref, own_ref, g_ref):
        g_ref[...] = _sum_partials(pos_ref[1], own_ref[0].astype(F32), parts_ref)

    grid_spec = pltpu.PrefetchScalarGridSpec(
        num_scalar_prefetch=1, grid=(r // tr, c // tc),
        in_specs=[BS((N_CHIPS, tr, tc), lambda i, j, pos: (0, i, j)), BS((1, tr, tc), lambda i, j, pos: (pos[1], i, j))],
        out_specs=BS((tr, tc), lambda i, j, pos: (i, j)))
    return pl.pallas_call(body, out_shape=SDS((r, c), F32), grid_spec=grid_spec, name="sum_chips",
                          compiler_params=_params("parallel", "parallel"))(pos, parts, own)


def _adamw_ready(name, g, w, m, v, layer, prev):
    depth, r, c = w.shape
    tr = next(cand for cand in (256, 128, 64, 32, 16, 8) if r % cand == 0)

    def body(g_in_ref, w_ref, m_ref, v_ref, *rest):
        g_ref, d_ref, nm_ref, nv_ref = rest[-4:]
        g = g_in_ref[...]
        delta, nm, nv = _adamw_math(w_ref[0], g, m_ref[0], v_ref[0])
        g_ref[0] = g
        d_ref[0] = delta
        nm_ref[0] = nm
        nv_ref[0] = nv

    lay = BS((1, tr, c), lambda i: (layer, i, 0))
    stacked = SDS((depth, r, c), F32)
    ins, specs, aliases = [g, w, m, v], [BS((tr, c), lambda i: (i, 0)), lay, lay, lay], {}
    if prev is not None:
        ins += list(prev)
        specs += [ANY] * 4
        aliases = {4 + q: q for q in range(4)}
    return pl.pallas_call(body, out_shape=(stacked,) * 4, grid=(r // tr,), in_specs=specs, out_specs=(lay,) * 4,
                          input_output_aliases=aliases, name=name, compiler_params=_params("parallel"))(*ins)


def _adamw_small(pos, parts, own, w, m, v):
    r, c = w.shape
    tr = _tile(r, 256)

    def body(pos_ref, parts_ref, own_ref, w_ref, m_ref, v_ref, g_ref, d_ref, nm_ref, nv_ref):
        g = _sum_partials(pos_ref[1], own_ref[...], parts_ref)
        delta, nm, nv = _adamw_math(w_ref[...], g, m_ref[...], v_ref[...])
        g_ref[...] = g
        d_ref[...] = delta
        nm_ref[...] = nm
        nv_ref[...] = nv

    row = BS((tr, c), lambda i, pos: (i, 0))
    out = SDS((r, c), F32)
    grid_spec = pltpu.PrefetchScalarGridSpec(
        num_scalar_prefetch=1, grid=(r // tr,), in_specs=[BS((N_CHIPS, tr, c), lambda i, pos: (0, i, 0)), row, row, row, row],
        out_specs=(row,) * 4)
    return pl.pallas_call(body, out_shape=(out,) * 4, grid_spec=grid_spec, name="adamw_small",
                          compiler_params=_params("parallel"))(pos, parts, own, w, m, v)


def _pad_w_in(gathered):
    d = gathered.shape[2]
    w = gathered.reshape(PROJ_RAW, d)
    real_f = OFF_F + F_COLS
    return jnp.concatenate([w[:real_f], jnp.zeros((OFF_GU - real_f, d), w.dtype), w[real_f:]], axis=0)


def _unpad_dw_in(dw):
    d = dw.shape[1]
    real_f = OFF_F + F_COLS
    return jnp.concatenate([dw[:real_f], dw[OFF_GU:]], axis=0).reshape(N_DEV, PROJ_RAW // N_DEV, d)


def _pack_small(tree, zero=0.0):
    flat = jnp.concatenate([tree[n].reshape(-1) for n in SMALL]) + zero
    rows = -(-flat.shape[0] // (256 * HEAD)) * 256
    return jnp.pad(flat, (0, rows * HEAD - flat.shape[0])).reshape(rows, HEAD)


def _unpack_small(packed, like):
    flat = packed.reshape(-1)
    out, off = {}, 0
    for n in SMALL:
        size = like[n].size
        out[n] = flat[off:off + size].reshape(like[n].shape)
        off += size
    return out


def _layer_fwd_mix(h0, sw, gw):
    t, d = h0.shape
    xn1 = _rms_fwd("rms_fwd", h0, sw["norm_mix"])
    proj = _matmul("proj_fwd", xn1, gw["w_in"], "nt", F32, t, 512)
    c = _fgate_fwd(proj, sw["forget_bias"])
    cq = c[:, :N_HEADS].T.reshape(N_HEADS, t, 1)
    ck = cq.reshape(N_HEADS, 1, t)
    y_attn, lse = _attn_fwd(proj, cq, ck, sw["q_norm"], sw["k_norm"])
    y_gmlp = _gmlp_fwd(proj, sw["gmlp_v_norm"], sw["gmlp_w_s"], sw["gmlp_b_s"])
    y_pool = _pool_fwd(proj, sw["pool_w"], sw["pool_scale"])
    mix = jnp.concatenate([y_attn, y_gmlp, y_pool], axis=1)
    h1 = _matmul("out_fwd", mix, gw["w_out"], "nn", F32, t, 512, res=h0)
    return h1, dict(h0=h0, xn1=xn1, proj=proj, cq=cq, ck=ck, lse=lse, mix=mix, h1=h1)


def _layer_fwd_ffn(h1, p_bf, sw, gw):
    xn2 = _rms_fwd("rms_fwd", h1, sw["norm_ffn"])
    a, b, hh = _ffn_fwd(xn2, gw["w_ffn_gate"], gw["w_ffn_up"])
    h2 = _ffn_down(hh, gw["w_ffn_down"], h1)
    xn3 = _rms_fwd("rms_fwd", h2, sw["norm_ple"])
    h3, z, pp = _ple_fwd(xn3, gw["w_ple_gate"], p_bf, gw["w_ple_proj"], h2)
    return h3, dict(xn2=xn2, a=a, b=b, hh=hh, h2=h2, xn3=xn3, z=z, pp=pp)


FFN_SIDE = ("w_ple_proj", "w_ple_gate", "w_ffn_down", "w_ffn_gate", "w_ffn_up")
MIX_SIDE = ("w_out", "w_in")


def _layer_bwd_ffn(dh3, p_bf, sw, gw, s, hook, after):
    t, d = dh3.shape
    big, small = {}, {}
    dpp, dz = _ple_bwd_gate(dh3, s["z"], s["pp"], after)
    big["w_ple_proj"] = _ple_dwpp(p_bf, dpp)
    big["w_ple_gate"] = _matmul("dw_tn", s["xn3"], dz, "tn", BF16, d, 512).reshape(N_DEV, d // N_DEV, d)
    dxn3 = _matmul("dx_nt", dz, gw["w_ple_gate"], "nt", F32, t, 512)
    dh2, dh2_bf, small["norm_ple"] = _rms_bwd_call("rms_bwd", dxn3, s["h2"], sw["norm_ple"] + hook(dxn3)[0, 0], dh3)
    da, db = _ffn_bwd_hidden(dh2_bf, gw["w_ffn_down"], s["a"], s["b"])
    big["w_ffn_down"] = _ffn_dwd(s["hh"], dh2_bf)
    big["w_ffn_gate"], big["w_ffn_up"] = _ffn_dwgu(s["xn2"], da, db)
    dxn2 = _ffn_dxn(da, db, gw["w_ffn_gate"], gw["w_ffn_up"])
    dh1, dh1_bf, small["norm_ffn"] = _rms_bwd_call("rms_bwd", dxn2, s["h1"], sw["norm_ffn"], dh2)
    return (dh1, dh1_bf), big, small


def _layer_bwd_mix(dh1, dh1_bf, sw, gw, s, hook, after):
    t, d = dh1.shape
    big, small = {}, {}
    dmix = _matmul("dx_nt", dh1_bf, gw["w_out"], "nt", F32, t, 512, after=after)
    big["w_out"] = _matmul("dw_tn", s["mix"], dh1_bf, "tn", BF16, d, 512).reshape(N_DEV, d // N_DEV, d)
    proj = s["proj"]
    dxp, small["pool_w"], small["pool_scale"] = _pool_bwd(dmix, proj, sw["pool_w"], sw["pool_scale"] + hook(dmix)[0, 0])
    dgu, dgv, small["gmlp_v_norm"], small["gmlp_w_s"], small["gmlp_b_s"] = _gmlp_bwd(
        dmix, proj, sw["gmlp_v_norm"], sw["gmlp_w_s"], sw["gmlp_b_s"])
    dq, dk, dv, dcq, dck, small["q_norm"], small["k_norm"] = _attn_bwd(
        proj, s["cq"], s["ck"], s["lse"], dmix, sw["q_norm"], sw["k_norm"])
    dc = (dcq.reshape(N_HEADS, t) + dck.reshape(N_HEADS, t)).T
    dc = jnp.pad(dc, ((0, 0), (0, HEAD - N_HEADS)))
    df, small["forget_bias"] = _fgate_bwd(dc, proj, sw["forget_bias"])
    dproj = jnp.concatenate([dq, dk, dv, df, dgu, dgv, dxp], axis=1)
    big["w_in"] = _unpad_dw_in(_matmul("dw_in_tn", dproj, s["xn1"], "tn", BF16, 512, d))
    dxn1 = _matmul("dx_in_nn", dproj, gw["w_in"], "nn", F32, _tile(t, 512), 512)
    dh0, _, small["norm_mix"] = _rms_bwd_call("rms_bwd", dxn1, s["h0"], sw["norm_mix"], dh1)
    return dh0, big, small


def _small_kernel_shapes(sm, i):
    row = lambda a: a[i].reshape(1, -1)
    return dict(
        norm_mix=row(sm["norm_mix"]), norm_ffn=row(sm["norm_ffn"]), norm_ple=row(sm["norm_ple"]),
        q_norm=row(sm["q_norm"]), k_norm=row(sm["k_norm"]),
        forget_bias=jnp.pad(row(sm["forget_bias"]), ((0, 0), (0, HEAD - F_COLS))),
        gmlp_v_norm=row(sm["gmlp_v_norm"]), gmlp_w_s=sm["gmlp_w_s"][i], gmlp_b_s=sm["gmlp_b_s"][i].reshape(N_GROUPS, HEAD, 1),
        pool_w=sm["pool_w"][i], pool_scale=row(sm["pool_scale"]))


def _small_grad_shapes(g, like):
    out = {}
    for n in SMALL:
        v = g[n]
        if n == "forget_bias":
            v = v[:, :F_COLS]
        out[n] = v.reshape(like[n].shape[1:])
    return out


def kernel(x, p, norm_mix, w_in, q_norm, k_norm, forget_bias, gmlp_v_norm, gmlp_w_s, gmlp_b_s, pool_w, pool_scale, w_out, norm_ffn, w_ffn_gate, w_ffn_up, w_ffn_down, norm_ple, w_ple_gate, w_ple_proj, loss_target, m_norm_mix, m_w_in, m_q_norm, m_k_norm, m_forget_bias, m_gmlp_v_norm, m_gmlp_w_s, m_gmlp_b_s, m_pool_w, m_pool_scale, m_w_out, m_norm_ffn, m_w_ffn_gate, m_w_ffn_up, m_w_ffn_down, m_norm_ple, m_w_ple_gate, m_w_ple_proj, v_norm_mix, v_w_in, v_q_norm, v_k_norm, v_forget_bias, v_gmlp_v_norm, v_gmlp_w_s, v_gmlp_b_s, v_pool_w, v_pool_scale, v_w_out, v_norm_ffn, v_w_ffn_gate, v_w_ffn_up, v_w_ffn_down, v_norm_ple, v_w_ple_gate, v_w_ple_proj):
    w = dict(norm_mix=norm_mix, w_in=w_in, q_norm=q_norm, k_norm=k_norm, forget_bias=forget_bias, gmlp_v_norm=gmlp_v_norm,
             gmlp_w_s=gmlp_w_s, gmlp_b_s=gmlp_b_s, pool_w=pool_w, pool_scale=pool_scale, w_out=w_out, norm_ffn=norm_ffn,
             w_ffn_gate=w_ffn_gate, w_ffn_up=w_ffn_up, w_ffn_down=w_ffn_down, norm_ple=norm_ple, w_ple_gate=w_ple_gate,
             w_ple_proj=w_ple_proj)
    m = dict(norm_mix=m_norm_mix, w_in=m_w_in, q_norm=m_q_norm, k_norm=m_k_norm, forget_bias=m_forget_bias,
             gmlp_v_norm=m_gmlp_v_norm, gmlp_w_s=m_gmlp_w_s, gmlp_b_s=m_gmlp_b_s, pool_w=m_pool_w, pool_scale=m_pool_scale,
             w_out=m_w_out, norm_ffn=m_norm_ffn, w_ffn_gate=m_w_ffn_gate, w_ffn_up=m_w_ffn_up, w_ffn_down=m_w_ffn_down,
             norm_ple=m_norm_ple, w_ple_gate=m_w_ple_gate, w_ple_proj=m_w_ple_proj)
    v = dict(norm_mix=v_norm_mix, w_in=v_w_in, q_norm=v_q_norm, k_norm=v_k_norm, forget_bias=v_forget_bias,
             gmlp_v_norm=v_gmlp_v_norm, gmlp_w_s=v_gmlp_w_s, gmlp_b_s=v_gmlp_b_s, pool_w=v_pool_w, pool_scale=v_pool_scale,
             w_out=v_w_out, norm_ffn=v_norm_ffn, w_ffn_gate=v_w_ffn_gate, w_ffn_up=v_w_ffn_up, w_ffn_down=v_w_ffn_down,
             norm_ple=v_norm_ple, w_ple_gate=v_w_ple_gate, w_ple_proj=v_w_ple_proj)
    for tree in (w, m, v):
        for n in TRANSPOSED:
            tree[n] = jnp.transpose(tree[n], (0, 2, 1))
    depth = w_in.shape[0]
    t, d = x.shape[1], x.shape[2]
    h = x[0]
    mx, my, mc = _mesh_pos()
    pos = jnp.stack([_dev_index(mx, my, mc), 2 * mx + my, mc]).astype(jnp.int32)
    parts_of = {"A": MIX_SIDE[::-1], "B": FFN_SIDE[::-1]}
    src = {"A": [jnp.transpose(w["w_in"], (0, 2, 1)).astype(BF16), w["w_out"].astype(BF16)]}

    def ici_start(i, part, after=None):
        lands = [lax.empty((N_DEV, *a.shape[1:]), a.dtype) for a in src[part]]
        send_sems, recv_sems, src[part], lands, token = _split_start(
            f"gather_start_{part}_l{i}", src[part], lands, 3, _gather_ici_copies(i, False), after)
        return (send_sems, recv_sems, lands), token

    def both_start(i):
        state["A"], token = ici_start(i, "A")
        state["B"], token = ici_start(i, "B", token)
        return token

    def ici_wait(i, part, state, after):
        send_sems, recv_sems, lands = state
        src[part], lands = _split_wait(f"gather_wait_{part}_l{i}", src[part], lands, send_sems, recv_sems, after,
                                       _gather_ici_copies(i, True))
        return lands

    def next_layer_start(i):
        return both_start(i + 1)[0, 0]

    def placed(i, part, lands):
        return {n: _place_own(pos, s_, i, land) for n, s_, land in zip(parts_of[part], src[part], lands)}

    gathered, saved = [], []
    state = {}
    state["A"], token = ici_start(0, "A")
    src["B"] = [(w[n] + token[0, 0]).astype(BF16) for n in parts_of["B"]]
    state["B"], token = ici_start(0, "B", token)
    p_bf = p[:, 0].astype(BF16)
    packed_wmv = [_pack_small(tree, token[0, 0]) for tree in (w, m, v)]
    for i in range(depth):
        sw = _small_kernel_shapes(w, i)
        lands_a = ici_wait(i, "A", state["A"], [h, token] if i == 0 else h)
        if i > 0:
            lands_b = ici_wait(i, "B", state["B"], h)
        lands_a = _gather_d2d(f"gather_sibling_A_l{i}", src["A"], i, lands_a)
        if i > 0:
            send_sems, recv_sems, src["B"], lands_b, token = _split_start(
                f"gather_sibling_start_l{i}", src["B"], lands_b, 4, _gather_sibling_copies(i, False))
            sw["norm_mix"] = sw["norm_mix"] + (next_layer_start(i) if i + 1 < depth else token[0, 0])
        gw = placed(i, "A", lands_a)
        gw["w_in"] = _pad_w_in(gw["w_in"])
        gw["w_out"] = gw["w_out"].reshape(d, d)
        h, s = _layer_fwd_mix(h, sw, gw)
        if i > 0:
            src["B"], lands_b = _split_wait(f"gather_sibling_wait_l{i}", src["B"], lands_b, send_sems, recv_sems, h,
                                            _gather_sibling_copies(i, True))
        else:
            lands_b = _gather_d2d("gather_sibling_B_l0", src["B"], 0, ici_wait(0, "B", state["B"], h))
            if depth > 1:
                sw["norm_ffn"] = sw["norm_ffn"] + next_layer_start(0)
        gw.update(placed(i, "B", lands_b))
        gw["w_ple_gate"] = gw["w_ple_gate"].reshape(d, d)
        h, s_ffn = _layer_fwd_ffn(h, p_bf[i], sw, gw)
        gathered.append((sw, gw))
        saved.append({**s, **s_ffn})

    dh, loss_part = _loss_call(h, loss_target[0])
    loss = lax.psum(loss_part[0, 0], ("x", "y", "c"))

    small_grads = [None] * depth
    stacked = {n: None for n in BIG}
    parts = [dict() for _ in range(depth)]
    on_d2d, on_ici = [], [[] for _ in range(depth)]

    def round_start(tag, layer, names, grads):
        lands = [lax.empty(g.shape if g.ndim == 2 else (N_CHIPS, *g.shape[1:]), g.dtype) for g in grads]
        send_sems, recv_sems, grads, lands, token = _split_start(f"sibling_start_{tag}", grads, lands, 4, _sibling_copies)
        on_d2d.append((tag, layer, names, grads, lands, send_sems, recv_sems))
        return token

    def hook(after):
        token = jnp.zeros((1, 1), F32)
        while on_d2d:
            tag, layer, names, grads, lands, send_sems, recv_sems = on_d2d.pop(0)
            grads, got = _split_wait(f"sibling_wait_{tag}", grads, lands, send_sems, recv_sems, after, _sibling_copies)
            sums = [_chip_sums(pos, g, r) for g, r in zip(grads, got)]
            lands = [lax.empty((N_CHIPS, *q.shape[-2:]), q.dtype) for q in sums]
            send_sems, recv_sems, sums, lands, token = _split_start(f"chip_start_{tag}", sums, lands, 3, _chip_copies(False))
            on_ici[layer].append((tag, names, sums, lands, send_sems, recv_sems))
        return token

    def finish(layer, after):
        for tag, names, sums, lands, send_sems, recv_sems in on_ici[layer]:
            sums, lands = _split_wait(f"chip_wait_{tag}", sums, lands, send_sems, recv_sems, after, _chip_copies(True))
            parts[layer].update(zip(names, zip(lands, sums)))
        for n in BIG:
            if n == "w_in":
                g = _sum_chips(pos, *parts[layer][n]).T
                stacked[n] = _adamw_ready(f"adamw_{n}", g, w[n], m[n], v[n], layer, stacked[n])
            else:
                stacked[n] = _adamw_layer(f"adamw_{n}", pos, *parts[layer][n], w[n], m[n], v[n], layer, stacked[n])

    token = jnp.zeros((8, HEAD), F32) + loss * 0.0
    for i in reversed(range(depth)):
        sw, gw = gathered[i]
        (dh1, dh1_bf), big, small = _layer_bwd_ffn(dh, p_bf[i], sw, gw, saved[i], hook, token)
        token = round_start(f"ffn_l{i}", i, FFN_SIDE, [big[n] for n in FFN_SIDE])
        dh, big, small_mix = _layer_bwd_mix(dh1, dh1_bf, sw, gw, saved[i], hook, token)
        small_grads[i] = _small_grad_shapes({**small, **small_mix}, w)
        names, grads = list(MIX_SIDE), [big[n] for n in MIX_SIDE]
        if i == 0:
            g_small = {n: jnp.stack([small_grads[q][n] for q in range(depth)]) for n in SMALL}
            names.append("small")
            grads.append(_pack_small(g_small))
        token = round_start(f"mix_l{i}", i, names, grads)
        if i == 0:
            token = hook(token)
        if i + 1 < depth:
            finish(i + 1, token)
    finish(0, [stacked[n][0] for n in BIG] if depth > 1 else token)
    packed = _adamw_small(pos, *parts[0]["small"], *packed_wmv)
    small_out = [_unpack_small(q, w) for q in packed]

    results = []
    for q in range(4):
        big_out = {n: jnp.transpose(stacked[n][q], (0, 2, 1)) if n in TRANSPOSED else stacked[n][q] for n in BIG}
        results.append({**big_out, **small_out[q]})
    outs = [loss, dh[None]]
    for q in range(4):
        outs += [results[q][n] for n in WEIGHTS]
    return tuple(outs)
```

```python
import functools

import jax
import jax.numpy as jnp
from jax import lax
from jax.experimental import pallas as pl
from jax.experimental.pallas import tpu as pltpu

F32 = jnp.float32
BF16 = jnp.bfloat16
EPS = 1e-6
HEAD = 128
N_HEADS = 8
N_GROUPS = 4
POOL_WINDOWS = (2, 4, 8, 16)
N_DEV = 8
ATTN_Q_BLOCK = 256

OFF_Q, OFF_K, OFF_V, OFF_F, OFF_GU, OFF_GV, OFF_XP, PROJ_PAD = 0, 1024, 2048, 3072, 3584, 4096, 4608, 5120
F_COLS = 8
PROJ_RAW = 4616

ADAM_LR, ADAM_B1, ADAM_B2, ADAM_EPS, ADAM_WD, ADAM_STEP = 0.001, 0.9, 0.999, 1e-08, 0.01, 10

BIG = ("w_in", "w_out", "w_ffn_gate", "w_ffn_up", "w_ffn_down", "w_ple_gate", "w_ple_proj")
TRANSPOSED = ("w_ffn_gate", "w_ffn_up")
SMALL = ("norm_mix", "q_norm", "k_norm", "forget_bias", "gmlp_v_norm", "gmlp_w_s", "gmlp_b_s", "pool_w",
         "pool_scale", "norm_ffn", "norm_ple")
WEIGHTS = ("norm_mix", "w_in", "q_norm", "k_norm", "forget_bias", "gmlp_v_norm", "gmlp_w_s", "gmlp_b_s", "pool_w",
           "pool_scale", "w_out", "norm_ffn", "w_ffn_gate", "w_ffn_up", "w_ffn_down", "norm_ple", "w_ple_gate",
           "w_ple_proj")

VMEM_LIMIT = 56 * 1024 * 1024

BS = pl.BlockSpec
SDS = jax.ShapeDtypeStruct
ANY = pl.BlockSpec(memory_space=pl.ANY)


def _params(*sem):
    return pltpu.CompilerParams(dimension_semantics=sem, vmem_limit_bytes=VMEM_LIMIT)


def _dot(a, b, mode="nn", precision=None):
    ca, cb = {"nn": (1, 0), "nt": (1, 1), "tn": (0, 0)}[mode]
    return lax.dot_general(a, b, (((ca,), (cb,)), ((), ())), preferred_element_type=F32, precision=precision)


def _rms_scale(x):
    return lax.rsqrt(jnp.mean(x * x, axis=-1, keepdims=True) + EPS)


def _rms_bwd(g, xhat, r):
    return r * (g - xhat * jnp.mean(g * xhat, axis=-1, keepdims=True))


def _gelu(x, with_grad=False):
    k = 0.7978845608028654
    inner = k * (x + 0.044715 * x * x * x)
    t = jnp.tanh(inner)
    y = 0.5 * x * (1.0 + t)
    if not with_grad:
        return y
    dy = 0.5 * (1.0 + t) + 0.5 * x * (1.0 - t * t) * k * (1.0 + 3.0 * 0.044715 * x * x)
    return y, dy


def _sigmoid(x):
    return 1.0 / (1.0 + jnp.exp(-x))


def _tile(n, want):
    t = min(n, want)
    assert n % t == 0, (n, want)
    return t


def _matmul(name, a, b, mode, out_dtype, tm, tn, res=None, after=None):
    if mode == "nn":
        (m, k), n = a.shape, b.shape[1]
        a_spec, b_spec = BS((tm, k), lambda i, j: (i, 0)), BS((k, tn), lambda i, j: (0, j))
    elif mode == "nt":
        (m, k), n = a.shape, b.shape[0]
        a_spec, b_spec = BS((tm, k), lambda i, j: (i, 0)), BS((tn, k), lambda i, j: (j, 0))
    else:
        (k, m), n = a.shape, b.shape[1]
        a_spec, b_spec = BS((k, tm), lambda i, j: (0, i)), BS((k, tn), lambda i, j: (0, j))
    assert m % tm == 0 and n % tn == 0
    o_spec = BS((tm, tn), lambda i, j: (i, j))

    def body(a_ref, b_ref, *rest):
        o_ref = rest[-1]
        acc = _dot(a_ref[...], b_ref[...], mode)
        if res is not None:
            acc = acc + rest[0][...]
        o_ref[...] = acc.astype(out_dtype)

    ins, specs = [a, b], [a_spec, b_spec]
    if res is not None:
        ins.append(res)
        specs.append(o_spec)
    if after is not None:
        ins.append(after)
        specs.append(BS(after.shape, lambda i, j: (0, 0)))
    return pl.pallas_call(body, out_shape=SDS((m, n), out_dtype), grid=(m // tm, n // tn), in_specs=specs,
                          out_specs=o_spec, name=name, compiler_params=_params("parallel", "parallel"))(*ins)


def _rms_fwd(name, h, gain):
    t, d = h.shape
    tm = _tile(t, 256)

    def body(h_ref, g_ref, o_ref):
        x = h_ref[...]
        o_ref[...] = (x * _rms_scale(x) * g_ref[...]).astype(BF16)

    return pl.pallas_call(body, out_shape=SDS((t, d), BF16), grid=(t // tm,),
                          in_specs=[BS((tm, d), lambda i: (i, 0)), BS((1, d), lambda i: (0, 0))],
                          out_specs=BS((tm, d), lambda i: (i, 0)), name=name, compiler_params=_params("parallel"))(h, gain)


def _rms_bwd_call(name, dxn, h, gain, dres):
    t, d = h.shape
    tm = _tile(t, 256)

    def body(dxn_ref, h_ref, g_ref, dres_ref, dh_ref, dhb_ref, dg_ref):
        x = h_ref[...]
        r = _rms_scale(x)
        xhat = x * r
        dy = dxn_ref[...]
        dh = dres_ref[...] + _rms_bwd(dy * g_ref[...], xhat, r)
        dh_ref[...] = dh
        dhb_ref[...] = dh.astype(BF16)

        @pl.when(pl.program_id(0) == 0)
        def _():
            dg_ref[...] = jnp.zeros_like(dg_ref)

        dg_ref[...] += jnp.sum(dy * xhat, axis=0, keepdims=True)

    row = BS((tm, d), lambda i: (i, 0))
    vec = BS((1, d), lambda i: (0, 0))
    return pl.pallas_call(body, out_shape=(SDS((t, d), F32), SDS((t, d), BF16), SDS((1, d), F32)), grid=(t // tm,),
                          in_specs=[row, row, vec, row], out_specs=(row, row, vec), name=name,
                          compiler_params=_params("arbitrary"))(dxn, h, gain, dres)


def _fgate_fwd(proj, bias):
    t = proj.shape[0]
    nb = t // HEAD

    def body(f_ref, b_ref, c_ref):
        tri = (lax.broadcasted_iota(jnp.int32, (HEAD, HEAD), 0) >= lax.broadcasted_iota(jnp.int32, (HEAD, HEAD), 1)).astype(F32)
        carry = jnp.zeros((1, HEAD), F32)
        for n in range(nb):
            rows = slice(n * HEAD, (n + 1) * HEAD)
            x = f_ref[rows, :] + b_ref[...]
            lf = jnp.minimum(x, 0.0) - jnp.log(1.0 + jnp.exp(-jnp.abs(x)))
            cb = _dot(tri, lf, precision=lax.Precision.HIGHEST) + carry
            c_ref[rows, :] = cb
            carry = cb[HEAD - 1:HEAD, :]

    return pl.pallas_call(body, out_shape=SDS((t, HEAD), F32), grid=(1,),
                          in_specs=[BS((t, HEAD), lambda i: (0, OFF_F // HEAD)), BS((1, HEAD), lambda i: (0, 0))],
                          out_specs=BS((t, HEAD), lambda i: (0, 0)), name="fgate_fwd",
                          compiler_params=_params("arbitrary"))(proj, bias)


def _fgate_bwd(dc, proj, bias):
    t = proj.shape[0]
    nb = t // HEAD
    width = OFF_GU - OFF_F

    def body(dc_ref, f_ref, b_ref, df_ref, db_ref):
        tri = (lax.broadcasted_iota(jnp.int32, (HEAD, HEAD), 0) <= lax.broadcasted_iota(jnp.int32, (HEAD, HEAD), 1)).astype(F32)
        carry = jnp.zeros((1, HEAD), F32)
        db = jnp.zeros((1, HEAD), F32)
        df_ref[:, HEAD:] = jnp.zeros((t, width - HEAD), BF16)
        for n in reversed(range(nb)):
            rows = slice(n * HEAD, (n + 1) * HEAD)
            dlf = _dot(tri, dc_ref[rows, :], precision=lax.Precision.HIGHEST) + carry
            carry = dlf[0:1, :]
            x = f_ref[rows, :] + b_ref[...]
            df = dlf * _sigmoid(-x)
            df_ref[rows, 0:HEAD] = df.astype(BF16)
            db = db + jnp.sum(df, axis=0, keepdims=True)
        db_ref[...] = db

    return pl.pallas_call(body, out_shape=(SDS((t, width), BF16), SDS((1, HEAD), F32)), grid=(1,),
                          in_specs=[BS((t, HEAD), lambda i: (0, 0)), BS((t, HEAD), lambda i: (0, OFF_F // HEAD)),
                                    BS((1, HEAD), lambda i: (0, 0))],
                          out_specs=(BS((t, width), lambda i: (0, 0)), BS((1, HEAD), lambda i: (0, 0))),
                          name="fgate_bwd", compiler_params=_params("arbitrary"))(dc, proj, bias)


def _attn_masked_logits(qs_ref, kn_ref, cq_ref, ck_ref, i, tq):
    lo, hi = i * tq, (i + 1) * tq
    s = _dot(qs_ref[lo:hi, :], kn_ref[0:hi, :], "nt")
    s = s + cq_ref[0, lo:hi, :] - ck_ref[0, :, 0:hi]
    row = lax.broadcasted_iota(jnp.int32, (tq, hi), 0) + lo
    col = lax.broadcasted_iota(jnp.int32, (tq, hi), 1)
    return s, row >= col


def _attn_fwd(proj, cq, ck, qg, kg):
    t = proj.shape[0]
    tq = _tile(t, ATTN_Q_BLOCK)
    nq = t // tq
    scale = HEAD ** -0.5

    def body(q_ref, k_ref, v_ref, cq_ref, ck_ref, qg_ref, kg_ref, o_ref, lse_ref, qs_ref, kn_ref, vb_ref):
        q = q_ref[...]
        k = k_ref[...]
        qs_ref[...] = (q * _rms_scale(q) * qg_ref[...] * scale).astype(BF16)
        kn_ref[...] = (k * _rms_scale(k) * kg_ref[...]).astype(BF16)
        vb_ref[...] = v_ref[...].astype(BF16)
        for i in range(nq):
            lo, hi = i * tq, (i + 1) * tq
            s, keep = _attn_masked_logits(qs_ref, kn_ref, cq_ref, ck_ref, i, tq)
            s = jnp.where(keep, s, -1e30)
            m = jnp.max(s, axis=-1, keepdims=True)
            e = jnp.exp(s - m)
            l = jnp.sum(e, axis=-1, keepdims=True)
            o = _dot(e.astype(BF16), vb_ref[0:hi, :]) / l
            o_ref[lo:hi, :] = o.astype(BF16)
            lse_ref[0, lo:hi, :] = m + jnp.log(l)

    def col(off):
        return BS((t, HEAD), lambda h: (0, off // HEAD + h))

    vec = BS((1, HEAD), lambda h: (0, 0))
    return pl.pallas_call(
        body, out_shape=(SDS((t, N_HEADS * HEAD), BF16), SDS((N_HEADS, t, 1), F32)), grid=(N_HEADS,),
        in_specs=[col(OFF_Q), col(OFF_K), col(OFF_V), BS((1, t, 1), lambda h: (h, 0, 0)), BS((1, 1, t), lambda h: (h, 0, 0)),
                  vec, vec],
        out_specs=(BS((t, HEAD), lambda h: (0, h)), BS((1, t, 1), lambda h: (h, 0, 0))),
        scratch_shapes=[pltpu.VMEM((t, HEAD), BF16)] * 3, name="attn_fwd",
        compiler_params=_params("parallel"))(proj, proj, proj, cq, ck, qg, kg)


def _attn_bwd(proj, cq, ck, lse, dmix, qg, kg):
    t = proj.shape[0]
    tq = _tile(t, ATTN_Q_BLOCK)
    nq = t // tq
    scale = HEAD ** -0.5

    def body(q_ref, k_ref, v_ref, cq_ref, ck_ref, lse_ref, do_ref, qg_ref, kg_ref,
             dq_ref, dk_ref, dv_ref, dcq_ref, dck_ref, dqg_ref, dkg_ref,
             qs_ref, kn_ref, vb_ref, dob_ref, dqs_ref, dkn_ref, dva_ref):
        q = q_ref[...]
        k = k_ref[...]
        rq = _rms_scale(q)
        rk = _rms_scale(k)
        qs_ref[...] = (q * rq * qg_ref[...] * scale).astype(BF16)
        kn_ref[...] = (k * rk * kg_ref[...]).astype(BF16)
        vb_ref[...] = v_ref[...].astype(BF16)
        dob_ref[...] = do_ref[...].astype(BF16)
        dkn_ref[...] = jnp.zeros_like(dkn_ref)
        dva_ref[...] = jnp.zeros_like(dva_ref)
        dck_ref[...] = jnp.zeros_like(dck_ref)
        for i in range(nq):
            lo, hi = i * tq, (i + 1) * tq
            s, keep = _attn_masked_logits(qs_ref, kn_ref, cq_ref, ck_ref, i, tq)
            pr = jnp.where(keep, jnp.exp(s - lse_ref[0, lo:hi, :]), 0.0)
            dp = _dot(dob_ref[lo:hi, :], vb_ref[0:hi, :], "nt")
            delta = jnp.sum(pr * dp, axis=-1, keepdims=True)
            ds = pr * (dp - delta)
            dcq_ref[0, lo:hi, :] = jnp.sum(ds, axis=-1, keepdims=True)
            dck_ref[0, :, 0:hi] += -jnp.sum(ds, axis=0, keepdims=True)
            dsb = ds.astype(BF16)
            dqs_ref[lo:hi, :] = _dot(dsb, kn_ref[0:hi, :])
            dkn_ref[0:hi, :] += _dot(dsb, qs_ref[lo:hi, :], "tn")
            dva_ref[0:hi, :] += _dot(pr.astype(BF16), dob_ref[lo:hi, :], "tn")
        dv_ref[...] = dva_ref[...].astype(BF16)

        @pl.when(pl.program_id(0) == 0)
        def _():
            dqg_ref[...] = jnp.zeros_like(dqg_ref)
            dkg_ref[...] = jnp.zeros_like(dkg_ref)

        qhat = q * rq
        dqn = dqs_ref[...] * scale
        dqg_ref[...] += jnp.sum(dqn * qhat, axis=0, keepdims=True)
        dq_ref[...] = _rms_bwd(dqn * qg_ref[...], qhat, rq).astype(BF16)
        khat = k * rk
        dkn = dkn_ref[...]
        dkg_ref[...] += jnp.sum(dkn * khat, axis=0, keepdims=True)
        dk_ref[...] = _rms_bwd(dkn * kg_ref[...], khat, rk).astype(BF16)

    def col(off):
        return BS((t, HEAD), lambda h: (0, off // HEAD + h))

    vec = BS((1, HEAD), lambda h: (0, 0))
    c_col = BS((1, t, 1), lambda h: (h, 0, 0))
    c_row = BS((1, 1, t), lambda h: (h, 0, 0))
    head_out = BS((t, HEAD), lambda h: (0, h))
    width = N_HEADS * HEAD
    return pl.pallas_call(
        body,
        out_shape=(SDS((t, width), BF16), SDS((t, width), BF16), SDS((t, width), BF16), SDS((N_HEADS, t, 1), F32),
                   SDS((N_HEADS, 1, t), F32), SDS((1, HEAD), F32), SDS((1, HEAD), F32)),
        grid=(N_HEADS,),
        in_specs=[col(OFF_Q), col(OFF_K), col(OFF_V), c_col, c_row, c_col, head_out, vec, vec],
        out_specs=(head_out, head_out, head_out, c_col, c_row, vec, vec),
        scratch_shapes=[pltpu.VMEM((t, HEAD), BF16)] * 4 + [pltpu.VMEM((t, HEAD), F32)] * 3, name="attn_bwd",
        compiler_params=_params("arbitrary"))(proj, proj, proj, cq, ck, lse, dmix, qg, kg)


def _group_cols(off):
    width = N_GROUPS * HEAD
    return lambda t: BS((t, width), lambda i: (0, off // width))


def _tril():
    return lax.broadcasted_iota(jnp.int32, (HEAD, HEAD), 0) >= lax.broadcasted_iota(jnp.int32, (HEAD, HEAD), 1)


def _gmlp_fwd(proj, gain, w_s, b_s):
    t = proj.shape[0]
    width = N_GROUPS * HEAD
    nc = t // HEAD

    def body(gu_ref, gv_ref, gain_ref, ws_ref, bs_ref, o_ref):
        tril = _tril()
        for g in range(N_GROUPS):
            cols = slice(g * HEAD, (g + 1) * HEAD)
            u = _gelu(gu_ref[:, cols])
            vv = _gelu(gv_ref[:, cols])
            vn = (vv * _rms_scale(vv) * gain_ref[:, cols]).astype(BF16)
            w = jnp.where(tril, ws_ref[g], 0.0).astype(BF16)
            for n in range(nc):
                rows = slice(n * HEAD, (n + 1) * HEAD)
                mixed = _dot(w, vn[rows]) + bs_ref[g]
                o_ref[rows, cols] = (u[rows] * mixed).astype(BF16)

    full = lambda shape: BS(shape, lambda i: (0,) * len(shape))
    return pl.pallas_call(body, out_shape=SDS((t, width), BF16), grid=(1,),
                          in_specs=[_group_cols(OFF_GU)(t), _group_cols(OFF_GV)(t), full((1, width)),
                                    full((N_GROUPS, HEAD, HEAD)), full((N_GROUPS, HEAD, 1))],
                          out_specs=full((t, width)), name="gmlp_fwd",
                          compiler_params=_params("arbitrary"))(proj, proj, gain, w_s, b_s)


def _gmlp_bwd(dmix, proj, gain, w_s, b_s):
    t = proj.shape[0]
    width = N_GROUPS * HEAD
    nc = t // HEAD

    def body(dy_ref, gu_ref, gv_ref, gain_ref, ws_ref, bs_ref, dgu_ref, dgv_ref, dgain_ref, dws_ref, dbs_ref, dvn_ref):
        tril = _tril()
        for g in range(N_GROUPS):
            cols = slice(g * HEAD, (g + 1) * HEAD)
            u, du = _gelu(gu_ref[:, cols], with_grad=True)
            vv, dvv = _gelu(gv_ref[:, cols], with_grad=True)
            r = _rms_scale(vv)
            vhat = vv * r
            gain_g = gain_ref[:, cols]
            vn = (vhat * gain_g).astype(BF16)
            w = jnp.where(tril, ws_ref[g], 0.0).astype(BF16)
            dws = jnp.zeros((HEAD, HEAD), F32)
            dbs = jnp.zeros((HEAD, 1), F32)
            for n in range(nc):
                rows = slice(n * HEAD, (n + 1) * HEAD)
                mixed = _dot(w, vn[rows]) + bs_ref[g]
                dy = dy_ref[rows, cols]
                dgu_ref[rows, cols] = (dy * mixed * du[rows]).astype(BF16)
                dm = dy * u[rows]
                dmb = dm.astype(BF16)
                dbs = dbs + jnp.sum(dm, axis=-1, keepdims=True)
                dws = dws + _dot(dmb, vn[rows], "nt")
                dvn_ref[rows, :] = _dot(w, dmb, "tn")
            dvn = dvn_ref[...]
            dgain_ref[:, cols] = jnp.sum(dvn * vhat, axis=0, keepdims=True)
            dgv_ref[:, cols] = (_rms_bwd(dvn * gain_g, vhat, r) * dvv).astype(BF16)
            dws_ref[g] = jnp.where(tril, dws, 0.0)
            dbs_ref[g] = dbs

    full = lambda shape: BS(shape, lambda i: (0,) * len(shape))
    return pl.pallas_call(
        body,
        out_shape=(SDS((t, width), BF16), SDS((t, width), BF16), SDS((1, width), F32), SDS((N_GROUPS, HEAD, HEAD), F32),
                   SDS((N_GROUPS, HEAD, 1), F32)),
        grid=(1,),
        in_specs=[BS((t, width), lambda i: (0, 2)), _group_cols(OFF_GU)(t), _group_cols(OFF_GV)(t), full((1, width)),
                  full((N_GROUPS, HEAD, HEAD)), full((N_GROUPS, HEAD, 1))],
        out_specs=(full((t, width)), full((t, width)), full((1, width)), full((N_GROUPS, HEAD, HEAD)),
                   full((N_GROUPS, HEAD, 1))),
        scratch_shapes=[pltpu.VMEM((t, HEAD), F32)], name="gmlp_bwd",
        compiler_params=_params("arbitrary"))(dmix, proj, proj, gain, w_s, b_s)


def _pool_window_mean_minus_x(x, window, t_idx):
    s, span = x, 1
    while span < window:
        s = s + jnp.where(t_idx >= span, pltpu.roll(s, span, 0), 0.0)
        span *= 2
    cnt = jnp.minimum(t_idx + 1, window).astype(F32)
    return s / cnt - x, cnt


def _pool_fwd(proj, w_pool, scale):
    t = proj.shape[0]
    width = N_GROUPS * HEAD

    def body(x_ref, w_ref, sc_ref, o_ref):
        t_idx = lax.broadcasted_iota(jnp.int32, (t, HEAD), 0)
        for g in range(N_GROUPS):
            cols = slice(g * HEAD, (g + 1) * HEAD)
            d, _ = _pool_window_mean_minus_x(x_ref[:, cols], POOL_WINDOWS[g], t_idx)
            y = _dot(d.astype(BF16), w_ref[g].astype(BF16)) * sc_ref[:, cols]
            o_ref[:, cols] = y.astype(BF16)

    full = lambda shape: BS(shape, lambda i: (0,) * len(shape))
    return pl.pallas_call(body, out_shape=SDS((t, width), BF16), grid=(1,),
                          in_specs=[_group_cols(OFF_XP)(t), full((N_GROUPS, HEAD, HEAD)), full((1, width))],
                          out_specs=full((t, width)), name="pool_fwd",
                          compiler_params=_params("arbitrary"))(proj, w_pool, scale)


def _pool_bwd(dmix, proj, w_pool, scale):
    t = proj.shape[0]
    width = N_GROUPS * HEAD

    def body(dy_ref, x_ref, w_ref, sc_ref, dx_ref, dw_ref, dsc_ref):
        t_idx = lax.broadcasted_iota(jnp.int32, (t, HEAD), 0)
        for g in range(N_GROUPS):
            cols = slice(g * HEAD, (g + 1) * HEAD)
            window = POOL_WINDOWS[g]
            d, cnt = _pool_window_mean_minus_x(x_ref[:, cols], window, t_idx)
            db = d.astype(BF16)
            wb = w_ref[g].astype(BF16)
            dy = dy_ref[:, cols]
            dsc_ref[:, cols] = jnp.sum(dy * _dot(db, wb), axis=0, keepdims=True)
            dyw = (dy * sc_ref[:, cols]).astype(BF16)
            dw_ref[g] = _dot(db, dyw, "tn")
            dd = _dot(dyw, wb, "nt")
            rsum, span = dd / cnt, 1
            while span < window:
                rsum = rsum + jnp.where(t_idx < t - span, pltpu.roll(rsum, t - span, 0), 0.0)
                span *= 2
            dx_ref[:, cols] = (rsum - dd).astype(BF16)

    full = lambda shape: BS(shape, lambda i: (0,) * len(shape))
    return pl.pallas_call(
        body, out_shape=(SDS((t, width), BF16), SDS((N_GROUPS, HEAD, HEAD), F32), SDS((1, width), F32)), grid=(1,),
        in_specs=[BS((t, width), lambda i: (0, 3)), _group_cols(OFF_XP)(t), full((N_GROUPS, HEAD, HEAD)), full((1, width))],
        out_specs=(full((t, width)), full((N_GROUPS, HEAD, HEAD)), full((1, width))), name="pool_bwd",
        compiler_params=_params("arbitrary"))(dmix, proj, w_pool, scale)


def _ffn_fwd(xn, wg, wu):
    t, d = xn.shape
    fs = wg.shape[1]
    tm = _tile(t, 512)

    def body(x_ref, wg_ref, wu_ref, a_ref, b_ref, hh_ref):
        x = x_ref[...]
        a = _dot(x, wg_ref[0], "nt")
        b = _dot(x, wu_ref[0], "nt")
        a_ref[0] = a
        b_ref[0] = b
        hh_ref[0] = (a * _sigmoid(a) * b).astype(BF16)

    w_spec = BS((1, fs, d), lambda j, i: (j, 0, 0))
    o_spec = BS((1, tm, fs), lambda j, i: (j, i, 0))
    return pl.pallas_call(body, out_shape=(SDS((N_DEV, t, fs), F32), SDS((N_DEV, t, fs), F32), SDS((N_DEV, t, fs), BF16)),
                          grid=(N_DEV, t // tm), in_specs=[BS((tm, d), lambda j, i: (i, 0)), w_spec, w_spec],
                          out_specs=(o_spec, o_spec, o_spec), name="ffn_fwd",
                          compiler_params=_params("parallel", "parallel"))(xn, wg, wu)


def _ffn_down(hh, wd, res):
    _, t, fs = hh.shape
    d = wd.shape[2]
    tm, tn = _tile(t, 1024), _tile(d, 1024)

    def body(a_ref, b_ref, r_ref, o_ref, acc_ref):
        k = pl.program_id(2)

        @pl.when(k == 0)
        def _():
            acc_ref[...] = r_ref[...]

        acc_ref[...] += _dot(a_ref[0], b_ref[0])

        @pl.when(k == N_DEV - 1)
        def _():
            o_ref[...] = acc_ref[...]

    o_spec = BS((tm, tn), lambda i, j, k: (i, j))
    return pl.pallas_call(body, out_shape=SDS((t, d), F32), grid=(t // tm, d // tn, N_DEV),
                          in_specs=[BS((1, tm, fs), lambda i, j, k: (k, i, 0)), BS((1, fs, tn), lambda i, j, k: (k, 0, j)), o_spec],
                          out_specs=o_spec, scratch_shapes=[pltpu.VMEM((tm, tn), F32)], name="ffn_down",
                          compiler_params=_params("parallel", "parallel", "arbitrary"))(hh, wd, res)


def _ffn_bwd_hidden(dh_bf, wd, a, b):
    t, d = dh_bf.shape
    fs = wd.shape[1]
    tm = _tile(t, 512)

    def body(dh_ref, wd_ref, a_ref, b_ref, da_ref, db_ref):
        dhh = _dot(dh_ref[...], wd_ref[0], "nt")
        av = a_ref[0]
        sig = _sigmoid(av)
        da_ref[0] = (dhh * b_ref[0] * sig * (1.0 + av * (1.0 - sig))).astype(BF16)
        db_ref[0] = (dhh * av * sig).astype(BF16)

    o_spec = BS((1, tm, fs), lambda j, i: (j, i, 0))
    return pl.pallas_call(body, out_shape=(SDS((N_DEV, t, fs), BF16), SDS((N_DEV, t, fs), BF16)), grid=(N_DEV, t // tm),
                          in_specs=[BS((tm, d), lambda j, i: (i, 0)), BS((1, fs, d), lambda j, i: (j, 0, 0)), o_spec, o_spec],
                          out_specs=(o_spec, o_spec), name="ffn_bwd_hidden",
                          compiler_params=_params("parallel", "parallel"))(dh_bf, wd, a, b)


def _ffn_dwd(hh, dh_bf):
    _, t, fs = hh.shape
    d = dh_bf.shape[1]
    tn = _tile(d, 1024)

    def body(a_ref, b_ref, o_ref):
        o_ref[0] = _dot(a_ref[0], b_ref[...], "tn").astype(BF16)

    return pl.pallas_call(body, out_shape=SDS((N_DEV, fs, d), BF16), grid=(N_DEV, d // tn),
                          in_specs=[BS((1, t, fs), lambda j, n: (j, 0, 0)), BS((t, tn), lambda j, n: (0, n))],
                          out_specs=BS((1, fs, tn), lambda j, n: (j, 0, n)), name="ffn_dwd",
                          compiler_params=_params("parallel", "parallel"))(hh, dh_bf)


def _ffn_dwgu(xn, da, db):
    t, d = xn.shape
    fs = da.shape[2]
    tn = _tile(d, 1024)

    def body(x_ref, da_ref, db_ref, dg_ref, du_ref):
        x = x_ref[...]
        dg_ref[0] = _dot(da_ref[0], x, "tn").astype(BF16)
        du_ref[0] = _dot(db_ref[0], x, "tn").astype(BF16)

    g_spec = BS((1, t, fs), lambda j, i: (j, 0, 0))
    o_spec = BS((1, fs, tn), lambda j, i: (j, 0, i))
    return pl.pallas_call(body, out_shape=(SDS((N_DEV, fs, d), BF16), SDS((N_DEV, fs, d), BF16)), grid=(N_DEV, d // tn),
                          in_specs=[BS((t, tn), lambda j, i: (0, i)), g_spec, g_spec], out_specs=(o_spec, o_spec),
                          name="ffn_dwgu", compiler_params=_params("parallel", "parallel"))(xn, da, db)


def _ffn_dxn(da, db, wg, wu):
    _, t, fs = da.shape
    d = wg.shape[2]
    tm, tn = _tile(t, 1024), _tile(d, 1024)

    def body(da_ref, db_ref, wg_ref, wu_ref, o_ref, acc_ref):
        k = pl.program_id(2)

        @pl.when(k == 0)
        def _():
            acc_ref[...] = jnp.zeros_like(acc_ref)

        acc_ref[...] += _dot(da_ref[0], wg_ref[0]) + _dot(db_ref[0], wu_ref[0])

        @pl.when(k == N_DEV - 1)
        def _():
            o_ref[...] = acc_ref[...]

    g_spec = BS((1, tm, fs), lambda i, j, k: (k, i, 0))
    w_spec = BS((1, fs, tn), lambda i, j, k: (k, 0, j))
    return pl.pallas_call(body, out_shape=SDS((t, d), F32), grid=(t // tm, d // tn, N_DEV),
                          in_specs=[g_spec, g_spec, w_spec, w_spec], out_specs=BS((tm, tn), lambda i, j, k: (i, j)),
                          scratch_shapes=[pltpu.VMEM((tm, tn), F32)], name="ffn_dxn",
                          compiler_params=_params("parallel", "parallel", "arbitrary"))(da, db, wg, wu)


def _ple_fwd(xn, wpg, p_bf, wpp, h):
    t, d = xn.shape
    dp = p_bf.shape[1]
    tn = wpp.shape[2]
    tm = _tile(t, 1024)

    def body(x_ref, wg_ref, p_ref, wp_ref, h_ref, o_ref, z_ref, pp_ref):
        z = _dot(x_ref[...], wg_ref[...])
        pp = _dot(p_ref[...], wp_ref[0])
        z_ref[...] = z
        pp_ref[...] = pp
        o_ref[...] = h_ref[...] + pp * _sigmoid(z)

    o_spec = BS((tm, tn), lambda i, j: (i, j))
    out = SDS((t, d), F32)
    return pl.pallas_call(body, out_shape=(out, out, out), grid=(t // tm, N_DEV),
                          in_specs=[BS((tm, d), lambda i, j: (i, 0)), BS((d, tn), lambda i, j: (0, j)),
                                    BS((tm, dp), lambda i, j: (i, 0)), BS((1, dp, tn), lambda i, j: (j, 0, 0)), o_spec],
                          out_specs=(o_spec, o_spec, o_spec), name="ple_fwd",
                          compiler_params=_params("parallel", "parallel"))(xn, wpg, p_bf, wpp, h)


def _ple_bwd_gate(dh, z, pp, after):
    t, d = dh.shape
    tm = _tile(t, 256)

    def body(dh_ref, z_ref, pp_ref, after_ref, dpp_ref, dz_ref):
        g = _sigmoid(z_ref[...])
        dh_v = dh_ref[...]
        dpp_ref[...] = (dh_v * g).astype(BF16)
        dz_ref[...] = (dh_v * pp_ref[...] * g * (1.0 - g)).astype(BF16)

    row = BS((tm, d), lambda i: (i, 0))
    return pl.pallas_call(body, out_shape=(SDS((t, d), BF16), SDS((t, d), BF16)), grid=(t // tm,),
                          in_specs=[row, row, row, BS(after.shape, lambda i: (0, 0))], out_specs=(row, row),
                          name="ple_bwd_gate", compiler_params=_params("parallel"))(dh, z, pp, after)


def _ple_dwpp(p_bf, dpp):
    t, dp = p_bf.shape
    tn = dpp.shape[1] // N_DEV

    def body(p_ref, g_ref, o_ref):
        o_ref[0] = _dot(p_ref[...], g_ref[...], "tn").astype(BF16)

    return pl.pallas_call(body, out_shape=SDS((N_DEV, dp, tn), BF16), grid=(N_DEV,),
                          in_specs=[BS((t, dp), lambda j: (0, 0)), BS((t, tn), lambda j: (0, j))],
                          out_specs=BS((1, dp, tn), lambda j: (j, 0, 0)), name="ple_dwpp",
                          compiler_params=_params("parallel"))(p_bf, dpp)


def _loss_call(y, target):
    t, d = y.shape
    tm = _tile(t, 256)

    def body(y_ref, t_ref, dy_ref, loss_ref):
        diff = y_ref[...] - t_ref[...]
        dy_ref[...] = diff * (1.0 / d)

        @pl.when(pl.program_id(0) == 0)
        def _():
            loss_ref[...] = jnp.zeros_like(loss_ref)

        loss_ref[...] += 0.5 * jnp.sum(jnp.mean(diff * diff, axis=-1, keepdims=True), axis=0, keepdims=True)

    row = BS((tm, d), lambda i: (i, 0))
    return pl.pallas_call(body, out_shape=(SDS((t, d), F32), SDS((1, 1), F32)), grid=(t // tm,), in_specs=[row, row],
                          out_specs=(row, BS((1, 1), lambda i: (0, 0))), name="loss",
                          compiler_params=_params("arbitrary"))(y, target)


def _mesh_pos():
    return lax.axis_index("x"), lax.axis_index("y"), lax.axis_index("c")


def _dev_index(px, py, pc):
    return 4 * px + 2 * py + pc


HBM = pl.BlockSpec(memory_space=pltpu.HBM)
SEM = pl.BlockSpec(memory_space=pltpu.SEMAPHORE)
DATAFLOW = pltpu.SideEffectType.DATAFLOW_SIDE_EFFECTING


def _in_hbm(arrs):
    return [pltpu.with_memory_space_constraint(a, pltpu.HBM) for a in arrs]


def _other_chips(x, y):
    return [(1 - x, y), (x, 1 - y), (1 - x, 1 - y)]


def _split_start(name, srcs, lands, n_sems, copies, after=None):
    n = len(srcs)
    after = [] if after is None else [after]
    n_in = 2 * n + len(after)

    def body(*refs):
        for cp in copies(refs[:n], refs[n:2 * n], refs[n_in], refs[n_in + 1]):
            cp.start()
        token = refs[-1]
        token[...] = jnp.zeros_like(token)

    thru = [pltpu.HBM(a.shape, a.dtype) for a in list(srcs) + list(lands)]
    outs = pl.pallas_call(
        body, name=name,
        out_shape=(pltpu.SemaphoreType.DMA((n * n_sems,)), pltpu.SemaphoreType.DMA((n * n_sems,)), *thru, SDS((8, HEAD), F32)),
        in_specs=[HBM] * (2 * n) + [ANY] * len(after),
        out_specs=(SEM, SEM, *([HBM] * (2 * n)), pl.BlockSpec(memory_space=pltpu.VMEM)),
        input_output_aliases={q: 2 + q for q in range(2 * n)},
        compiler_params=pltpu.CompilerParams(has_side_effects=DATAFLOW))(*_in_hbm(list(srcs) + list(lands)), *after)
    return outs[0], outs[1], list(outs[2:2 + n]), list(outs[2 + n:2 + 2 * n]), outs[-1]


def _split_wait(name, srcs, lands, send_sems, recv_sems, after, copies):
    n = len(srcs)
    after = list(after) if isinstance(after, (list, tuple)) else [after]

    def body(*refs):
        for cp in copies(refs[:n], refs[n:2 * n], refs[2 * n], refs[2 * n + 1]):
            cp.wait_send()
            cp.wait_recv()

    thru = [pltpu.HBM(a.shape, a.dtype) for a in list(srcs) + list(lands)]
    outs = pl.pallas_call(
        body, name=name, out_shape=tuple(thru), in_specs=[HBM] * (2 * n) + [SEM, SEM] + [ANY] * len(after),
        out_specs=tuple([HBM] * (2 * n)), input_output_aliases={q: q for q in range(2 * n)},
        compiler_params=pltpu.CompilerParams(has_side_effects=DATAFLOW))(*list(srcs), *list(lands), send_sems, recv_sems, *after)
    return list(outs[:n]), list(outs[n:])


def _gather_ici_copies(layer, waiting):
    def copies(src_refs, land_refs, send_sems, recv_sems):
        x, y, c = _mesh_pos()
        out = []
        for a in range(len(src_refs)):
            for j, chip in enumerate(_other_chips(x, y)):
                slot = _dev_index(*chip, c) if waiting else _dev_index(x, y, c)
                out.append(pltpu.make_async_remote_copy(
                    src_ref=src_refs[a].at[layer], dst_ref=land_refs[a].at[slot], send_sem=send_sems.at[3 * a + j],
                    recv_sem=recv_sems.at[3 * a + j], device_id=(*chip, c), device_id_type=pl.DeviceIdType.MESH))
        return out
    return copies


def _slab_tile(r, c):
    for cand in (512, 256, 176, 128, 64, 16, 8):
        if r % cand == 0:
            return cand, c
    return r, 256


def _place_own(pos, src, layer, land):
    _, r, c = land.shape
    tr, tc = _slab_tile(r, c)

    def body(pos_ref, s_ref, l_ref, o_ref):
        o_ref[...] = s_ref[...]

    grid_spec = pltpu.PrefetchScalarGridSpec(
        num_scalar_prefetch=1, grid=(r // tr, c // tc), in_specs=[BS((1, tr, tc), lambda i, j, pos: (layer, i, j)), ANY],
        out_specs=BS((1, tr, tc), lambda i, j, pos: (pos[0], i, j)))
    return pl.pallas_call(body, grid_spec=grid_spec, out_shape=SDS(land.shape, land.dtype), input_output_aliases={2: 0},
                          name="place_own", compiler_params=_params("parallel", "parallel"))(pos, src, land)


def _gather_sibling_copies(layer, waiting):
    def copies(src_refs, land_refs, send_sems, recv_sems):
        x, y, c = _mesh_pos()
        blocks = [(x, y)] + _other_chips(x, y)
        out = []
        for a in range(len(src_refs)):
            for k in range(4):
                slot = land_refs[a].at[_dev_index(*blocks[k], 1 - c if waiting else c)]
                src = src_refs[a].at[layer] if (k == 0 and not waiting) else slot
                out.append(pltpu.make_async_remote_copy(
                    src_ref=src, dst_ref=slot, send_sem=send_sems.at[4 * a + k], recv_sem=recv_sems.at[4 * a + k],
                    device_id=(x, y, 1 - c), device_id_type=pl.DeviceIdType.MESH))
        return out
    return copies


def _gather_d2d(name, srcs, layer, lands):
    n = len(srcs)

    def body(*refs):
        src_refs, land_refs = refs[:n], refs[n:2 * n]
        send_sems, recv_sems = refs[3 * n:]
        sends = _gather_sibling_copies(layer, False)(src_refs, land_refs, send_sems, recv_sems)
        for cp in sends:
            cp.start()
        for cp in _gather_sibling_copies(layer, True)(src_refs, land_refs, send_sems, recv_sems):
            cp.wait_recv()
        for cp in sends:
            cp.wait_send()

    outs = pl.pallas_call(
        body, name=name, out_shape=tuple(SDS(l.shape, l.dtype) for l in lands), in_specs=[ANY] * (2 * n),
        out_specs=tuple([ANY] * n), input_output_aliases={n + q: q for q in range(n)},
        scratch_shapes=[pltpu.SemaphoreType.DMA((4 * n,)), pltpu.SemaphoreType.DMA((4 * n,))],
        compiler_params=pltpu.CompilerParams(has_side_effects=True))(*srcs, *lands)
    return list(outs)


def _sibling_copies(src_refs, land_refs, send_sems, recv_sems):
    x, y, c = _mesh_pos()
    out = []
    for a in range(len(src_refs)):
        whole = len(src_refs[a].shape) == 2
        for q in range(1 if whole else 4):
            src = src_refs[a] if whole else src_refs[a].at[2 * q + (1 - c)]
            dst = land_refs[a] if whole else land_refs[a].at[q]
            out.append(pltpu.make_async_remote_copy(
                src_ref=src, dst_ref=dst, send_sem=send_sems.at[4 * a + q], recv_sem=recv_sems.at[4 * a + q],
                device_id=(x, y, 1 - c), device_id_type=pl.DeviceIdType.MESH))
    return out


def _chip_sums(pos, own, got):
    whole = own.ndim == 2
    r, c = own.shape[-2:]
    tr, tc = _slab_tile(r, c)

    def body(pos_ref, own_ref, got_ref, o_ref):
        mine = own_ref[...] if whole else own_ref[0]
        o_ref[...] = (mine.astype(F32) + got_ref[...].astype(F32)).astype(o_ref.dtype)

    if whole:
        grid = (r // tr, c // tc)
        blk = BS((tr, tc), lambda i, j, pos: (i, j))
        specs, o_spec, ins, out = [blk, blk], blk, [own, got], SDS((r, c), own.dtype)
    else:
        grid = (N_CHIPS, r // tr, c // tc)
        blk = BS((1, tr, tc), lambda q, i, j, pos: (q, i, j))
        specs = [BS((1, 1, tr, tc), lambda q, i, j, pos: (q, pos[2], i, j)), blk]
        o_spec, ins, out = blk, [own.reshape(N_CHIPS, 2, r, c), got], SDS((N_CHIPS, r, c), own.dtype)
    grid_spec = pltpu.PrefetchScalarGridSpec(num_scalar_prefetch=1, grid=grid, in_specs=specs, out_specs=o_spec)
    return pl.pallas_call(body, out_shape=out, grid_spec=grid_spec, name="chip_sums",
                          compiler_params=_params(*["parallel"] * len(grid)))(pos, *ins)


def _chip_copies(waiting):
    def copies(src_refs, land_refs, send_sems, recv_sems):
        x, y, c = _mesh_pos()
        out = []
        for a in range(len(src_refs)):
            whole = len(src_refs[a].shape) == 2
            for j, (qx, qy) in enumerate(_other_chips(x, y)):
                src = src_refs[a] if whole else src_refs[a].at[2 * qx + qy]
                slot = 2 * qx + qy if waiting else 2 * x + y
                out.append(pltpu.make_async_remote_copy(
                    src_ref=src, dst_ref=land_refs[a].at[slot], send_sem=send_sems.at[3 * a + j],
                    recv_sem=recv_sems.at[3 * a + j], device_id=(qx, qy, c), device_id_type=pl.DeviceIdType.MESH))
        return out
    return copies


def _adamw_math(w, g, m, v):
    m = ADAM_B1 * m + (1.0 - ADAM_B1) * g
    v = ADAM_B2 * v + (1.0 - ADAM_B2) * (g * g)
    m_hat = m / (1.0 - ADAM_B1 ** ADAM_STEP)
    v_hat = v / (1.0 - ADAM_B2 ** ADAM_STEP)
    delta = -ADAM_LR * (m_hat / (jnp.sqrt(v_hat) + ADAM_EPS) + ADAM_WD * w)
    return delta, m, v


N_CHIPS = 4


def _sum_partials(my_chip, own, parts_ref):
    g = None
    for q in range(N_CHIPS):
        term = jnp.where(my_chip == q, own, parts_ref[q].astype(F32))
        g = term if g is None else g + term
    return g


def _adamw_layer(name, pos, parts, own, w, m, v, layer, prev):
    depth, r, c = w.shape
    tr = next(cand for cand in (256, 128, 64, 32, 16, 8) if r % cand == 0)

    def body(pos_ref, parts_ref, own_ref, w_ref, m_ref, v_ref, *rest):
        g_ref, d_ref, nm_ref, nv_ref = rest[-4:]
        g = _sum_partials(pos_ref[1], own_ref[0].astype(F32), parts_ref)
        delta, nm, nv = _adamw_math(w_ref[0], g, m_ref[0], v_ref[0])
        g_ref[0] = g
        d_ref[0] = delta
        nm_ref[0] = nm
        nv_ref[0] = nv

    lay = BS((1, tr, c), lambda i, pos: (layer, i, 0))
    stacked = SDS((depth, r, c), F32)
    ins = [pos, parts, own, w, m, v]
    specs = [BS((N_CHIPS, tr, c), lambda i, pos: (0, i, 0)), BS((1, tr, c), lambda i, pos: (pos[1], i, 0)), lay, lay, lay]
    aliases = {}
    if prev is not None:
        ins += list(prev)
        specs += [ANY] * 4
        aliases = {6 + q: q for q in range(4)}
    grid_spec = pltpu.PrefetchScalarGridSpec(num_scalar_prefetch=1, grid=(r // tr,), in_specs=specs, out_specs=(lay,) * 4)
    return pl.pallas_call(body, out_shape=(stacked,) * 4, grid_spec=grid_spec, input_output_aliases=aliases, name=name,
                          compiler_params=_params("parallel"))(*ins)


def _sum_chips(pos, parts, own):
    _, r, c = parts.shape
    tr, tc = _slab_tile(r, c)

    def body(pos_ref, parts_ref, own_ref, g_ref):
        g_ref[...] = _sum_partials(pos_ref[1], own_ref[0].astype(F32), parts_ref)

    grid_spec = pltpu.PrefetchScalarGridSpec(
        num_scalar_prefetch=1, grid=(r // tr, c // tc),
        in_specs=[BS((N_CHIPS, tr, tc), lambda i, j, pos: (0, i, j)), BS((1, tr, tc), lambda i, j, pos: (pos[1], i, j))],
        out_specs=BS((tr, tc), lambda i, j, pos: (i, j)))
    return pl.pallas_call(body, out_shape=SDS((r, c), F32), grid_spec=grid_spec, name="sum_chips",
                          compiler_params=_params("parallel", "parallel"))(pos, parts, own)


def _adamw_ready(name, g, w, m, v, layer, prev):
    depth, r, c = w.shape
    tr = next(cand for cand in (256, 128, 64, 32, 16, 8) if r % cand == 0)

    def body(g_in_ref, w_ref, m_ref, v_ref, *rest):
        g_ref, d_ref, nm_ref, nv_ref = rest[-4:]
        g = g_in_ref[...]
        delta, nm, nv = _adamw_math(w_ref[0], g, m_ref[0], v_ref[0])
        g_ref[0] = g
        d_ref[0] = delta
        nm_ref[0] = nm
        nv_ref[0] = nv

    lay = BS((1, tr, c), lambda i: (layer, i, 0))
    stacked = SDS((depth, r, c), F32)
    ins, specs, aliases = [g, w, m, v], [BS((tr, c), lambda i: (i, 0)), lay, lay, lay], {}
    if prev is not None:
        ins += list(prev)
        specs += [ANY] * 4
        aliases = {4 + q: q for q in range(4)}
    return pl.pallas_call(body, out_shape=(stacked,) * 4, grid=(r // tr,), in_specs=specs, out_specs=(lay,) * 4,
                          input_output_aliases=aliases, name=name, compiler_params=_params("parallel"))(*ins)


def _adamw_small(pos, parts, own, w, m, v):
    r, c = w.shape
    tr = _tile(r, 256)

    def body(pos_ref, parts_ref, own_ref, w_ref, m_ref, v_ref, g_ref, d_ref, nm_ref, nv_ref):
        g = _sum_partials(pos_ref[1], own_ref[...], parts_ref)
        delta, nm, nv = _adamw_math(w_ref[...], g, m_ref[...], v_ref[...])
        g_ref[...] = g
        d_ref[...] = delta
        nm_ref[...] = nm
        nv_ref[...] = nv

    row = BS((tr, c), lambda i, pos: (i, 0))
    out = SDS((r, c), F32)
    grid_spec = pltpu.PrefetchScalarGridSpec(
        num_scalar_prefetch=1, grid=(r // tr,), in_specs=[BS((N_CHIPS, tr, c), lambda i, pos: (0, i, 0)), row, row, row, row],
        out_specs=(row,) * 4)
    return pl.pallas_call(body, out_shape=(out,) * 4, grid_spec=grid_spec, name="adamw_small",
                          compiler_params=_params("parallel"))(pos, parts, own, w, m, v)


def _pad_w_in(gathered):
    d = gathered.shape[2]
    w = gathered.reshape(PROJ_RAW, d)
    real_f = OFF_F + F_COLS
    return jnp.concatenate([w[:real_f], jnp.zeros((OFF_GU - real_f, d), w.dtype), w[real_f:]], axis=0)


def _unpad_dw_in(dw):
    d = dw.shape[1]
    real_f = OFF_F + F_COLS
    return jnp.concatenate([dw[:real_f], dw[OFF_GU:]], axis=0).reshape(N_DEV, PROJ_RAW // N_DEV, d)


def _pack_small(tree, zero=0.0):
    flat = jnp.concatenate([tree[n].reshape(-1) + zero for n in SMALL])
    rows = -(-flat.shape[0] // (256 * HEAD)) * 256
    return jnp.pad(flat, (0, rows * HEAD - flat.shape[0])).reshape(rows, HEAD)


def _unpack_small(packed, like):
    flat = packed.reshape(-1)
    out, off = {}, 0
    for n in SMALL:
        size = like[n].size
        out[n] = flat[off:off + size].reshape(like[n].shape)
        off += size
    return out


def _layer_fwd_mix(h0, sw, gw):
    t, d = h0.shape
    xn1 = _rms_fwd("rms_fwd", h0, sw["norm_mix"])
    proj = _matmul("proj_fwd", xn1, gw["w_in"], "nt", F32, t, 512)
    c = _fgate_fwd(proj, sw["forget_bias"])
    cq = c[:, :N_HEADS].T.reshape(N_HEADS, t, 1)
    ck = cq.reshape(N_HEADS, 1, t)
    y_attn, lse = _attn_fwd(proj, cq, ck, sw["q_norm"], sw["k_norm"])
    y_gmlp = _gmlp_fwd(proj, sw["gmlp_v_norm"], sw["gmlp_w_s"], sw["gmlp_b_s"])
    y_pool = _pool_fwd(proj, sw["pool_w"], sw["pool_scale"])
    mix = jnp.concatenate([y_attn, y_gmlp, y_pool], axis=1)
    h1 = _matmul("out_fwd", mix, gw["w_out"], "nn", F32, t, 512, res=h0)
    return h1, dict(h0=h0, xn1=xn1, proj=proj, cq=cq, ck=ck, lse=lse, mix=mix, h1=h1)


def _layer_fwd_ffn(h1, p_bf, sw, gw):
    xn2 = _rms_fwd("rms_fwd", h1, sw["norm_ffn"])
    a, b, hh = _ffn_fwd(xn2, gw["w_ffn_gate"], gw["w_ffn_up"])
    h2 = _ffn_down(hh, gw["w_ffn_down"], h1)
    xn3 = _rms_fwd("rms_fwd", h2, sw["norm_ple"])
    h3, z, pp = _ple_fwd(xn3, gw["w_ple_gate"], p_bf, gw["w_ple_proj"], h2)
    return h3, dict(xn2=xn2, a=a, b=b, hh=hh, h2=h2, xn3=xn3, z=z, pp=pp)


FFN_SIDE = ("w_ple_proj", "w_ple_gate", "w_ffn_down", "w_ffn_gate", "w_ffn_up")
MIX_SIDE = ("w_out", "w_in")


def _layer_bwd_ffn(dh3, p_bf, sw, gw, s, hook, after):
    t, d = dh3.shape
    big, small = {}, {}
    dpp, dz = _ple_bwd_gate(dh3, s["z"], s["pp"], after)
    big["w_ple_proj"] = _ple_dwpp(p_bf, dpp)
    big["w_ple_gate"] = _matmul("dw_tn", s["xn3"], dz, "tn", BF16, d, 512).reshape(N_DEV, d // N_DEV, d)
    dxn3 = _matmul("dx_nt", dz, gw["w_ple_gate"], "nt", F32, t, 512)
    dh2, dh2_bf, small["norm_ple"] = _rms_bwd_call("rms_bwd", dxn3, s["h2"], sw["norm_ple"] + hook(dxn3)[0, 0], dh3)
    da, db = _ffn_bwd_hidden(dh2_bf, gw["w_ffn_down"], s["a"], s["b"])
    big["w_ffn_down"] = _ffn_dwd(s["hh"], dh2_bf)
    big["w_ffn_gate"], big["w_ffn_up"] = _ffn_dwgu(s["xn2"], da, db)
    dxn2 = _ffn_dxn(da, db, gw["w_ffn_gate"], gw["w_ffn_up"])
    dh1, dh1_bf, small["norm_ffn"] = _rms_bwd_call("rms_bwd", dxn2, s["h1"], sw["norm_ffn"], dh2)
    return (dh1, dh1_bf), big, small


def _layer_bwd_mix(dh1, dh1_bf, sw, gw, s, hook, after):
    t, d = dh1.shape
    big, small = {}, {}
    dmix = _matmul("dx_nt", dh1_bf, gw["w_out"], "nt", F32, t, 512, after=after)
    big["w_out"] = _matmul("dw_tn", s["mix"], dh1_bf, "tn", BF16, d, 512).reshape(N_DEV, d // N_DEV, d)
    proj = s["proj"]
    dxp, small["pool_w"], small["pool_scale"] = _pool_bwd(dmix, proj, sw["pool_w"], sw["pool_scale"] + hook(dmix)[0, 0])
    dgu, dgv, small["gmlp_v_norm"], small["gmlp_w_s"], small["gmlp_b_s"] = _gmlp_bwd(
        dmix, proj, sw["gmlp_v_norm"], sw["gmlp_w_s"], sw["gmlp_b_s"])
    dq, dk, dv, dcq, dck, small["q_norm"], small["k_norm"] = _attn_bwd(
        proj, s["cq"], s["ck"], s["lse"], dmix, sw["q_norm"], sw["k_norm"])
    dc = (dcq.reshape(N_HEADS, t) + dck.reshape(N_HEADS, t)).T
    dc = jnp.pad(dc, ((0, 0), (0, HEAD - N_HEADS)))
    df, small["forget_bias"] = _fgate_bwd(dc, proj, sw["forget_bias"])
    dproj = jnp.concatenate([dq, dk, dv, df, dgu, dgv, dxp], axis=1)
    big["w_in"] = _unpad_dw_in(_matmul("dw_in_tn", dproj, s["xn1"], "tn", BF16, 512, d))
    dxn1 = _matmul("dx_in_nn", dproj, gw["w_in"], "nn", F32, _tile(t, 512), 512)
    dh0, _, small["norm_mix"] = _rms_bwd_call("rms_bwd", dxn1, s["h0"], sw["norm_mix"], dh1)
    return dh0, big, small


def _small_kernel_shapes(sm, i):
    row = lambda a: a[i].reshape(1, -1)
    return dict(
        norm_mix=row(sm["norm_mix"]), norm_ffn=row(sm["norm_ffn"]), norm_ple=row(sm["norm_ple"]),
        q_norm=row(sm["q_norm"]), k_norm=row(sm["k_norm"]),
        forget_bias=jnp.pad(row(sm["forget_bias"]), ((0, 0), (0, HEAD - F_COLS))),
        gmlp_v_norm=row(sm["gmlp_v_norm"]), gmlp_w_s=sm["gmlp_w_s"][i], gmlp_b_s=sm["gmlp_b_s"][i].reshape(N_GROUPS, HEAD, 1),
        pool_w=sm["pool_w"][i], pool_scale=row(sm["pool_scale"]))


def _small_grad_shapes(g, like):
    out = {}
    for n in SMALL:
        v = g[n]
        if n == "forget_bias":
            v = v[:, :F_COLS]
        out[n] = v.reshape(like[n].shape[1:])
    return out


def kernel(x, p, norm_mix, w_in, q_norm, k_norm, forget_bias, gmlp_v_norm, gmlp_w_s, gmlp_b_s, pool_w, pool_scale, w_out, norm_ffn, w_ffn_gate, w_ffn_up, w_ffn_down, norm_ple, w_ple_gate, w_ple_proj, loss_target, m_norm_mix, m_w_in, m_q_norm, m_k_norm, m_forget_bias, m_gmlp_v_norm, m_gmlp_w_s, m_gmlp_b_s, m_pool_w, m_pool_scale, m_w_out, m_norm_ffn, m_w_ffn_gate, m_w_ffn_up, m_w_ffn_down, m_norm_ple, m_w_ple_gate, m_w_ple_proj, v_norm_mix, v_w_in, v_q_norm, v_k_norm, v_forget_bias, v_gmlp_v_norm, v_gmlp_w_s, v_gmlp_b_s, v_pool_w, v_pool_scale, v_w_out, v_norm_ffn, v_w_ffn_gate, v_w_ffn_up, v_w_ffn_down, v_norm_ple, v_w_ple_gate, v_w_ple_proj):
    w = dict(norm_mix=norm_mix, w_in=w_in, q_norm=q_norm, k_norm=k_norm, forget_bias=forget_bias, gmlp_v_norm=gmlp_v_norm,
             gmlp_w_s=gmlp_w_s, gmlp_b_s=gmlp_b_s, pool_w=pool_w, pool_scale=pool_scale, w_out=w_out, norm_ffn=norm_ffn,
             w_ffn_gate=w_ffn_gate, w_ffn_up=w_ffn_up, w_ffn_down=w_ffn_down, norm_ple=norm_ple, w_ple_gate=w_ple_gate,
             w_ple_proj=w_ple_proj)
    m = dict(norm_mix=m_norm_mix, w_in=m_w_in, q_norm=m_q_norm, k_norm=m_k_norm, forget_bias=m_forget_bias,
             gmlp_v_norm=m_gmlp_v_norm, gmlp_w_s=m_gmlp_w_s, gmlp_b_s=m_gmlp_b_s, pool_w=m_pool_w, pool_scale=m_pool_scale,
             w_out=m_w_out, norm_ffn=m_norm_ffn, w_ffn_gate=m_w_ffn_gate, w_ffn_up=m_w_ffn_up, w_ffn_down=m_w_ffn_down,
             norm_ple=m_norm_ple, w_ple_gate=m_w_ple_gate, w_ple_proj=m_w_ple_proj)
    v = dict(norm_mix=v_norm_mix, w_in=v_w_in, q_norm=v_q_norm, k_norm=v_k_norm, forget_bias=v_forget_bias,
             gmlp_v_norm=v_gmlp_v_norm, gmlp_w_s=v_gmlp_w_s, gmlp_b_s=v_gmlp_b_s, pool_w=v_pool_w, pool_scale=v_pool_scale,
             w_out=v_w_out, norm_ffn=v_norm_ffn, w_ffn_gate=v_w_ffn_gate, w_ffn_up=v_w_ffn_up, w_ffn_down=v_w_ffn_down,
             norm_ple=v_norm_ple, w_ple_gate=v_w_ple_gate, w_ple_proj=v_w_ple_proj)
    for tree in (w, m, v):
        for n in TRANSPOSED:
            tree[n] = jnp.transpose(tree[n], (0, 2, 1))
    depth = w_in.shape[0]
    t, d = x.shape[1], x.shape[2]
    h = x[0]
    mx, my, mc = _mesh_pos()
    pos = jnp.stack([_dev_index(mx, my, mc), 2 * mx + my, mc]).astype(jnp.int32)
    parts_of = {"A": MIX_SIDE[::-1], "B": FFN_SIDE[::-1]}
    src = {"A": [jnp.transpose(w["w_in"], (0, 2, 1)).astype(BF16), w["w_out"].astype(BF16)]}

    def ici_start(i, part, after=None):
        lands = [lax.empty((N_DEV, *a.shape[1:]), a.dtype) for a in src[part]]
        send_sems, recv_sems, src[part], lands, token = _split_start(
            f"gather_start_{part}_l{i}", src[part], lands, 3, _gather_ici_copies(i, False), after)
        return (send_sems, recv_sems, lands), token

    def both_start(i):
        state["A"], token = ici_start(i, "A")
        state["B"], token = ici_start(i, "B", token)
        return token

    def ici_wait(i, part, state, after):
        send_sems, recv_sems, lands = state
        src[part], lands = _split_wait(f"gather_wait_{part}_l{i}", src[part], lands, send_sems, recv_sems, after,
                                       _gather_ici_copies(i, True))
        return lands

    def next_layer_start(i):
        return both_start(i + 1)[0, 0]

    def placed(i, part, lands):
        return {n: _place_own(pos, s_, i, land) for n, s_, land in zip(parts_of[part], src[part], lands)}

    gathered, saved = [], []
    state = {}
    state["A"], token = ici_start(0, "A")
    src["B"] = [(w[n] + token[0, 0]).astype(BF16) for n in parts_of["B"]]
    state["B"], token = ici_start(0, "B", token)
    p_bf = p[:, 0].astype(BF16)
    packed_wmv = [_pack_small(tree, token[0, 0]) for tree in (w, m, v)]
    for tree in (w, m, v):
        tree["w_in"] = tree["w_in"] + token[0, 0]
    for i in range(depth):
        sw = _small_kernel_shapes(w, i)
        lands_a = ici_wait(i, "A", state["A"], [h, token] if i == 0 else h)
        if i > 0:
            lands_b = ici_wait(i, "B", state["B"], h)
        lands_a = _gather_d2d(f"gather_sibling_A_l{i}", src["A"], i, lands_a)
        if i > 0:
            send_sems, recv_sems, src["B"], lands_b, token = _split_start(
                f"gather_sibling_start_l{i}", src["B"], lands_b, 4, _gather_sibling_copies(i, False))
            sw["norm_mix"] = sw["norm_mix"] + (next_layer_start(i) if i + 1 < depth else token[0, 0])
        gw = placed(i, "A", lands_a)
        gw["w_in"] = _pad_w_in(gw["w_in"])
        gw["w_out"] = gw["w_out"].reshape(d, d)
        h, s = _layer_fwd_mix(h, sw, gw)
        if i > 0:
            src["B"], lands_b = _split_wait(f"gather_sibling_wait_l{i}", src["B"], lands_b, send_sems, recv_sems, h,
                                            _gather_sibling_copies(i, True))
        else:
            lands_b = _gather_d2d("gather_sibling_B_l0", src["B"], 0, ici_wait(0, "B", state["B"], h))
            if depth > 1:
                sw["norm_ffn"] = sw["norm_ffn"] + next_layer_start(0)
        gw.update(placed(i, "B", lands_b))
        gw["w_ple_gate"] = gw["w_ple_gate"].reshape(d, d)
        h, s_ffn = _layer_fwd_ffn(h, p_bf[i], sw, gw)
        gathered.append((sw, gw))
        saved.append({**s, **s_ffn})

    dh, loss_part = _loss_call(h, loss_target[0])
    loss = lax.psum(loss_part[0, 0], ("x", "y", "c"))

    small_grads = [None] * depth
    stacked = {n: None for n in BIG}
    parts = [dict() for _ in range(depth)]
    on_d2d, on_ici = [], [[] for _ in range(depth)]

    def round_start(tag, layer, names, grads):
        lands = [lax.empty(g.shape if g.ndim == 2 else (N_CHIPS, *g.shape[1:]), g.dtype) for g in grads]
        send_sems, recv_sems, grads, lands, token = _split_start(f"sibling_start_{tag}", grads, lands, 4, _sibling_copies)
        on_d2d.append((tag, layer, names, grads, lands, send_sems, recv_sems))
        return token

    def hook(after):
        token = jnp.zeros((1, 1), F32)
        while on_d2d:
            tag, layer, names, grads, lands, send_sems, recv_sems = on_d2d.pop(0)
            grads, got = _split_wait(f"sibling_wait_{tag}", grads, lands, send_sems, recv_sems, after, _sibling_copies)
            sums = [_chip_sums(pos, g, r) for g, r in zip(grads, got)]
            lands = [lax.empty((N_CHIPS, *q.shape[-2:]), q.dtype) for q in sums]
            send_sems, recv_sems, sums, lands, token = _split_start(f"chip_start_{tag}", sums, lands, 3, _chip_copies(False))
            on_ici[layer].append((tag, names, sums, lands, send_sems, recv_sems))
        return token

    def finish(layer, after):
        for tag, names, sums, lands, send_sems, recv_sems in on_ici[layer]:
            sums, lands = _split_wait(f"chip_wait_{tag}", sums, lands, send_sems, recv_sems, after, _chip_copies(True))
            parts[layer].update(zip(names, zip(lands, sums)))
        for n in BIG:
            if n == "w_in":
                g = _sum_chips(pos, *parts[layer][n]).T
                stacked[n] = _adamw_ready(f"adamw_{n}", g, w[n], m[n], v[n], layer, stacked[n])
            else:
                stacked[n] = _adamw_layer(f"adamw_{n}", pos, *parts[layer][n], w[n], m[n], v[n], layer, stacked[n])

    token = jnp.zeros((8, HEAD), F32) + loss * 0.0
    for i in reversed(range(depth)):
        sw, gw = gathered[i]
        (dh1, dh1_bf), big, small = _layer_bwd_ffn(dh, p_bf[i], sw, gw, saved[i], hook, token)
        token = round_start(f"ffn_l{i}", i, FFN_SIDE, [big[n] for n in FFN_SIDE])
        dh, big, small_mix = _layer_bwd_mix(dh1, dh1_bf, sw, gw, saved[i], hook, token)
        small_grads[i] = _small_grad_shapes({**small, **small_mix}, w)
        names, grads = list(MIX_SIDE), [big[n] for n in MIX_SIDE]
        if i == 0:
            g_small = {n: jnp.stack([small_grads[q][n] for q in range(depth)]) for n in SMALL}
            names.append("small")
            grads.append(_pack_small(g_small))
        token = round_start(f"mix_l{i}", i, names, grads)
        if i == 0:
            token = hook(token)
        if i + 1 < depth:
            finish(i + 1, token)
    finish(0, [stacked[n][0] for n in BIG] if depth > 1 else token)
    packed = _adamw_small(pos, *parts[0]["small"], *packed_wmv)
    small_out = [_unpack_small(q, w) for q in packed]

    results = []
    for q in range(4):
        big_out = {n: jnp.transpose(stacked[n][q], (0, 2, 1)) if n in TRANSPOSED else stacked[n][q] for n in BIG}
        results.append({**big_out, **small_out[q]})
    outs = [loss, dh[None]]
    for q in range(4):
        outs += [results[q][n] for n in WEIGHTS]
    return tuple(outs)
```
